```python
import math
import jax, jax.numpy as jnp
from jax import lax
import numpy as np

D_MODEL = 1024
BATCH = 16
SEQ = 2048
DEPTH = 2

CHUNK = 64
Q_BLOCK = 128

SSD_HEADS = 16
SSD_HEAD_DIM = 64
SSD_INNER = SSD_HEADS * SSD_HEAD_DIM
SSD_GROUPS = 2
SSD_STATE = 128
SSD_CONV = 4
SSD_CONV_DIM = SSD_INNER + 2 * SSD_GROUPS * SSD_STATE

SB_HEADS = 8
SB_HEAD_DIM = 64
SB_WIDTH = SB_HEADS * SB_HEAD_DIM

MLA_HEADS = 8
MLA_Q_RANK = 256
MLA_KV_RANK = 128
MLA_NOPE = 64
MLA_ROPE = 32
MLA_V = 64
ROPE_THETA = 10000.0

FOX_HEADS = 8
FOX_HEAD_DIM = 64
FOX_WIDTH = FOX_HEADS * FOX_HEAD_DIM

EVEN_IN = SSD_INNER + SSD_CONV_DIM + SSD_HEADS + 3 * SB_WIDTH
EVEN_OUT = SSD_INNER + SB_WIDTH
ODD_IN = MLA_Q_RANK + MLA_KV_RANK + MLA_ROPE + 3 * FOX_WIDTH + FOX_HEADS
ODD_OUT = MLA_HEADS * MLA_V + FOX_WIDTH

MOE_GROUPS = 4
MOE_EXPERTS_PER_GROUP = 4
MOE_EXPERTS = MOE_GROUPS * MOE_EXPERTS_PER_GROUP
MOE_TOP_K = 2
MOE_FF = 256

DEEPNORM_ALPHA = (2.0 * DEPTH) ** 0.25
DEEPNORM_BETA = (8.0 * DEPTH) ** -0.25
LN_EPS = 1e-5
RMS_EPS = 1e-6

N_EVEN = (DEPTH + 1) // 2
N_ODD = DEPTH // 2

kernel_name = 'hybrid_ssd_stickbreak_mla_fox_hmoe'


def _layer_norm(x, g, b):
    x32 = x.astype(jnp.float32)
    mu = jnp.mean(x32, -1, keepdims=True)
    var = jnp.mean(jnp.square(x32 - mu), -1, keepdims=True)
    return ((x32 - mu) * lax.rsqrt(var + LN_EPS) * g + b).astype(x.dtype)


def _rms_norm(x, g):
    x32 = x.astype(jnp.float32)
    return (x32 * lax.rsqrt(jnp.mean(x32 * x32, -1, keepdims=True) + RMS_EPS) * g).astype(x.dtype)


def _split(h, sizes):
    idx = [int(i) for i in np.cumsum(sizes)[:-1]]
    return jnp.split(h, idx, axis=-1)


def _rope(x, cos, sin):
    x1, x2 = jnp.split(x, 2, axis=-1)
    return jnp.concatenate([x1 * cos - x2 * sin, x2 * cos + x1 * sin], axis=-1).astype(x.dtype)


def _causal_depthwise_conv(x, w, b):
    k, c = w.shape
    y = lax.conv_general_dilated(x, w[:, None, :], window_strides=(1,), padding=[(k - 1, 0)],
                                 dimension_numbers=('NWC', 'WIO', 'NWC'), feature_group_count=c)
    return y + b


def _ssd_scan(x, dt, a, b_in, c_in):
    f32 = jnp.float32
    bsz, s, h, p = x.shape
    g, n = b_in.shape[2], b_in.shape[3]
    r = h // g
    nc = s // CHUNK
    xs = (x.astype(f32) * dt[..., None]).reshape(bsz, nc, CHUNK, g, r, p)
    da = (dt * a).reshape(bsz, nc, CHUNK, g, r).transpose(0, 3, 4, 1, 2)
    a_cum = jnp.cumsum(da, axis=-1)
    bc = b_in.astype(f32).reshape(bsz, nc, CHUNK, g, n)
    cc = c_in.astype(f32).reshape(bsz, nc, CHUNK, g, n)
    causal = jnp.tril(jnp.ones((CHUNK, CHUNK), dtype=bool))
    seg = a_cum[..., :, None] - a_cum[..., None, :]
    decay_in = jnp.exp(jnp.where(causal, seg, -jnp.inf))
    cb = jnp.einsum('bclgn,bcsgn->bgcls', cc, bc)
    y_diag = jnp.einsum('bgcls,bgrcls,bcsgrp->bclgrp', cb, decay_in, xs)
    decay_to_end = jnp.exp(a_cum[..., -1:] - a_cum)
    states = jnp.einsum('bclgn,bgrcl,bclgrp->bcgrpn', bc, decay_to_end, xs)
    chunk_decay = jnp.exp(a_cum[..., -1])

    def step(hstate, inp):
        st, dec = inp
        return hstate * dec[..., None, None] + st, hstate

    h0 = jnp.zeros((bsz, g, r, p, n), f32)
    _, prev = lax.scan(step, h0, (jnp.moveaxis(states, 1, 0), jnp.moveaxis(chunk_decay, -1, 0)))
    y_off = jnp.einsum('bclgn,cbgrpn,bgrcl->bclgrp', cc, prev, jnp.exp(a_cum))
    return (y_diag + y_off).reshape(bsz, s, h, p)


def _stick_breaking_attention(q, k, v):
    s, d = q.shape[1], q.shape[3]
    scale = 1.0 / math.sqrt(d)
    outs = []
    for q0 in range(0, s, Q_BLOCK):
        end = q0 + Q_BLOCK
        z = jnp.einsum('bthd,bshd->bhts', q[:, q0:end], k[:, :end]).astype(jnp.float32) * scale
        t_pos = q0 + jnp.arange(Q_BLOCK)[:, None]
        s_pos = jnp.arange(end)[None, :]
        strict = s_pos < t_pos
        log_1m = jnp.where(strict, jax.nn.log_sigmoid(-z), 0.0)
        later = lax.cumsum(log_1m, axis=3, reverse=True) - log_1m
        w = jnp.where(strict, jnp.exp(jax.nn.log_sigmoid(z) + later), 0.0)
        outs.append(jnp.einsum('bhts,bshd->bthd', w.astype(v.dtype), v[:, :end]))
    return jnp.concatenate(outs, axis=1)


def _mla_attention(q_nope, q_pe, k_nope, k_pe, v):
    s = q_nope.shape[1]
    scale = 1.0 / math.sqrt(MLA_NOPE + MLA_ROPE)
    outs = []
    for q0 in range(0, s, Q_BLOCK):
        end = q0 + Q_BLOCK
        z = (jnp.einsum('bthd,bshd->bhts', q_nope[:, q0:end], k_nope[:, :end])
             + jnp.einsum('bthd,bsd->bhts', q_pe[:, q0:end], k_pe[:, :end])).astype(jnp.float32) * scale
        t_pos = q0 + jnp.arange(Q_BLOCK)[:, None]
        s_pos = jnp.arange(end)[None, :]
        allowed = (s_pos // CHUNK) <= (t_pos // CHUNK)
        prob = jax.nn.softmax(jnp.where(allowed, z, -jnp.inf), axis=-1)
        outs.append(jnp.einsum('bhts,bshd->bthd', prob.astype(v.dtype), v[:, :end]))
    return jnp.concatenate(outs, axis=1)


def _forgetting_attention(q, k, v, log_f):
    s, d = q.shape[1], q.shape[3]
    scale = 1.0 / math.sqrt(d)
    fh = jnp.cumsum(log_f, axis=1).transpose(0, 2, 1)
    outs = []
    for q0 in range(0, s, Q_BLOCK):
        end = q0 + Q_BLOCK
        z = jnp.einsum('bthd,bshd->bhts', q[:, q0:end], k[:, :end]).astype(jnp.float32) * scale
        z = z + fh[:, :, q0:end, None] - fh[:, :, None, :end]
        t_pos = q0 + jnp.arange(Q_BLOCK)[:, None]
        s_pos = jnp.arange(end)[None, :]
        prob = jax.nn.softmax(jnp.where(s_pos <= t_pos, z, -jnp.inf), axis=-1)
        outs.append(jnp.einsum('bhts,bshd->bthd', prob.astype(v.dtype), v[:, :end]))
    return jnp.concatenate(outs, axis=1)


def _even_mixer(x, w_in, conv_w, conv_b, dt_bias, a_log, d_skip, norm_g, w_out):
    f32 = jnp.float32
    bsz, s, _ = x.shape
    h = x @ w_in
    z, xbc, dt_raw, q, k, v = _split(h, [SSD_INNER, SSD_CONV_DIM, SSD_HEADS, SB_WIDTH, SB_WIDTH, SB_WIDTH])
    xbc = jax.nn.silu(_causal_depthwise_conv(xbc, conv_w, conv_b))
    xs, bs, cs = _split(xbc, [SSD_INNER, SSD_GROUPS * SSD_STATE, SSD_GROUPS * SSD_STATE])
    xs = xs.reshape(bsz, s, SSD_HEADS, SSD_HEAD_DIM)
    dt = jax.nn.softplus((dt_raw + dt_bias).astype(f32))
    a = -jnp.exp(a_log.astype(f32))
    y = _ssd_scan(xs, dt, a, bs.reshape(bsz, s, SSD_GROUPS, SSD_STATE), cs.reshape(bsz, s, SSD_GROUPS, SSD_STATE))
    y = (y + d_skip[:, None] * xs).reshape(bsz, s, SSD_INNER).astype(x.dtype)
    y = _rms_norm(y * jax.nn.silu(z), norm_g)
    sb = _stick_breaking_attention(q.reshape(bsz, s, SB_HEADS, SB_HEAD_DIM), k.reshape(bsz, s, SB_HEADS, SB_HEAD_DIM),
                                   v.reshape(bsz, s, SB_HEADS, SB_HEAD_DIM)).reshape(bsz, s, SB_WIDTH)
    return jnp.concatenate([y, sb], axis=-1) @ w_out


def _odd_mixer(x, w_in, q_norm_g, w_q_up, kv_norm_g, w_kv_up, f_bias, w_out, cos, sin):
    bsz, s, _ = x.shape
    h = x @ w_in
    c_q, c_kv, k_pe, q, k, v, f_raw = _split(h, [MLA_Q_RANK, MLA_KV_RANK, MLA_ROPE, FOX_WIDTH, FOX_WIDTH, FOX_WIDTH, FOX_HEADS])
    q_m = (_rms_norm(c_q, q_norm_g) @ w_q_up).reshape(bsz, s, MLA_HEADS, MLA_NOPE + MLA_ROPE)
    q_nope = q_m[..., :MLA_NOPE]
    q_pe = _rope(q_m[..., MLA_NOPE:], cos[:, None, :], sin[:, None, :])
    kv = (_rms_norm(c_kv, kv_norm_g) @ w_kv_up).reshape(bsz, s, MLA_HEADS, MLA_NOPE + MLA_V)
    k_nope, v_m = kv[..., :MLA_NOPE], kv[..., MLA_NOPE:]
    k_pe = _rope(k_pe, cos, sin)
    mla = _mla_attention(q_nope, q_pe, k_nope, k_pe, v_m).reshape(bsz, s, MLA_HEADS * MLA_V)
    log_f = jax.nn.log_sigmoid((f_raw + f_bias).astype(jnp.float32))
    fox = _forgetting_attention(q.reshape(bsz, s, FOX_HEADS, FOX_HEAD_DIM), k.reshape(bsz, s, FOX_HEADS, FOX_HEAD_DIM),
                                v.reshape(bsz, s, FOX_HEADS, FOX_HEAD_DIM), log_f).reshape(bsz, s, FOX_WIDTH)
    return jnp.concatenate([mla, fox], axis=-1) @ w_out


def _hier_moe(x, w_group, b_group, w_expert, b_expert, w_gate, w_up, w_down):
    f32 = jnp.float32
    bsz, s, d = x.shape
    t = x.reshape(bsz * s, d)
    n_tok = t.shape[0]
    g_prob = jax.nn.softmax((t @ w_group + b_group).astype(f32), axis=-1)
    g_p, g_sel = lax.top_k(g_prob, 1)
    e_logits = (t @ w_expert + b_expert).astype(f32).reshape(n_tok, MOE_GROUPS, MOE_EXPERTS_PER_GROUP)
    idx = jnp.broadcast_to(g_sel[:, :, None], (n_tok, 1, MOE_EXPERTS_PER_GROUP))
    e_prob = jax.nn.softmax(jnp.take_along_axis(e_logits, idx, axis=1)[:, 0], axis=-1)
    top_p, top_i = lax.top_k(e_prob, MOE_TOP_K)
    weights = g_p * top_p / jnp.sum(top_p, axis=-1, keepdims=True)
    expert_id = g_sel * MOE_EXPERTS_PER_GROUP + top_i
    gate = jnp.sum(jax.nn.one_hot(expert_id, MOE_EXPERTS, dtype=f32) * weights[..., None], axis=1)
    y = jnp.zeros((n_tok, d), f32)
    for e in range(MOE_EXPERTS):
        hid = jax.nn.silu(t @ w_gate[e]) * (t @ w_up[e])
        y = y + gate[:, e:e + 1] * (hid @ w_down[e])
    return y.astype(x.dtype).reshape(bsz, s, d)


def setup_inputs(seed: int = 0) -> dict:
    key = jax.random.key(seed)
    ks = iter(jax.random.split(key, 40))
    nrm = lambda shape, scale: jax.random.normal(next(ks), shape, jnp.float32) * scale
    gain = lambda shape: 1.0 + nrm(shape, 0.02)
    dt0 = jnp.exp(jax.random.uniform(next(ks), (N_EVEN, SSD_HEADS), jnp.float32, math.log(1e-3), math.log(1e-1)))
    return {
        'x': nrm((BATCH, SEQ, D_MODEL), 1.0),
        'ev_w_in': nrm((N_EVEN, D_MODEL, EVEN_IN), D_MODEL ** -0.5),
        'ev_conv_w': nrm((N_EVEN, SSD_CONV, SSD_CONV_DIM), SSD_CONV ** -0.5),
        'ev_conv_b': nrm((N_EVEN, SSD_CONV_DIM), 0.02),
        'ev_dt_bias': dt0 + jnp.log(-jnp.expm1(-dt0)),
        'ev_a_log': jnp.log(jax.random.uniform(next(ks), (N_EVEN, SSD_HEADS), jnp.float32, 1.0, 16.0)),
        'ev_d_skip': gain((N_EVEN, SSD_HEADS)),
        'ev_norm_g': gain((N_EVEN, SSD_INNER)),
        'ev_w_out': nrm((N_EVEN, EVEN_OUT, D_MODEL), EVEN_OUT ** -0.5 * DEEPNORM_BETA),
        'od_w_in': nrm((N_ODD, D_MODEL, ODD_IN), D_MODEL ** -0.5),
        'od_q_norm_g': gain((N_ODD, MLA_Q_RANK)),
        'od_w_q_up': nrm((N_ODD, MLA_Q_RANK, MLA_HEADS * (MLA_NOPE + MLA_ROPE)), MLA_Q_RANK ** -0.5),
        'od_kv_norm_g': gain((N_ODD, MLA_KV_RANK)),
        'od_w_kv_up': nrm((N_ODD, MLA_KV_RANK, MLA_HEADS * (MLA_NOPE + MLA_V)), MLA_KV_RANK ** -0.5),
        'od_f_bias': nrm((N_ODD, FOX_HEADS), 0.1),
        'od_w_out': nrm((N_ODD, ODD_OUT, D_MODEL), ODD_OUT ** -0.5 * DEEPNORM_BETA),
        'ln1_g': gain((DEPTH, D_MODEL)),
        'ln1_b': nrm((DEPTH, D_MODEL), 0.02),
        'ln2_g': gain((DEPTH, D_MODEL)),
        'ln2_b': nrm((DEPTH, D_MODEL), 0.02),
        'moe_w_group': nrm((DEPTH, D_MODEL, MOE_GROUPS), D_MODEL ** -0.5),
        'moe_b_group': nrm((DEPTH, MOE_GROUPS), 0.01),
        'moe_w_expert': nrm((DEPTH, D_MODEL, MOE_EXPERTS), D_MODEL ** -0.5),
        'moe_b_expert': nrm((DEPTH, MOE_EXPERTS), 0.01),
        'moe_w_gate': nrm((DEPTH, MOE_EXPERTS, D_MODEL, MOE_FF), D_MODEL ** -0.5),
        'moe_w_up': nrm((DEPTH, MOE_EXPERTS, D_MODEL, MOE_FF), D_MODEL ** -0.5 * DEEPNORM_BETA),
        'moe_w_down': nrm((DEPTH, MOE_EXPERTS, MOE_FF, D_MODEL), MOE_FF ** -0.5 * DEEPNORM_BETA),
    }


def reference(x, ev_w_in, ev_conv_w, ev_conv_b, ev_dt_bias, ev_a_log, ev_d_skip, ev_norm_g, ev_w_out,
              od_w_in, od_q_norm_g, od_w_q_up, od_kv_norm_g, od_w_kv_up, od_f_bias, od_w_out,
              ln1_g, ln1_b, ln2_g, ln2_b, moe_w_group, moe_b_group, moe_w_expert, moe_b_expert,
              moe_w_gate, moe_w_up, moe_w_down):
    s = x.shape[1]
    inv_freq = ROPE_THETA ** (-(jnp.arange(0, MLA_ROPE, 2, dtype=jnp.float32) / MLA_ROPE))
    ang = jnp.arange(s, dtype=jnp.float32)[:, None] * inv_freq[None, :]
    cos, sin = jnp.cos(ang), jnp.sin(ang)
    for i in range(DEPTH):
        j = i // 2
        if i % 2 == 0:
            mix = _even_mixer(x, ev_w_in[j], ev_conv_w[j], ev_conv_b[j], ev_dt_bias[j], ev_a_log[j],
                              ev_d_skip[j], ev_norm_g[j], ev_w_out[j])
        else:
            mix = _odd_mixer(x, od_w_in[j], od_q_norm_g[j], od_w_q_up[j], od_kv_norm_g[j], od_w_kv_up[j],
                             od_f_bias[j], od_w_out[j], cos, sin)
        x = _layer_norm(DEEPNORM_ALPHA * x + mix, ln1_g[i], ln1_b[i])
        ffn = _hier_moe(x, moe_w_group[i], moe_b_group[i], moe_w_expert[i], moe_b_expert[i],
                        moe_w_gate[i], moe_w_up[i], moe_w_down[i])
        x = _layer_norm(DEEPNORM_ALPHA * x + ffn, ln2_g[i], ln2_b[i])
    return x
```

```python
import functools
import math

import jax
import jax.numpy as jnp
from jax import lax
from jax.experimental import pallas as pl
from jax.experimental.pallas import tpu as pltpu

F32 = jnp.float32
BF16 = jnp.bfloat16

SSD_HEADS = 16
SSD_HEAD_DIM = 64
SSD_INNER = SSD_HEADS * SSD_HEAD_DIM
SSD_GROUPS = 2
SSD_STATE = 128
SSD_CONV = 4
SSD_BC = SSD_GROUPS * SSD_STATE
SSD_CONV_DIM = SSD_INNER + 2 * SSD_BC
ATT_HEADS = 8
ATT_HEAD_DIM = 64
ATT_WIDTH = ATT_HEADS * ATT_HEAD_DIM
HEAD_PAIRS = ATT_HEADS // 2
MLA_Q_RANK = 256
MLA_KV_RANK = 128
MLA_NOPE = 64
MLA_ROPE = 32
MLA_V = 64
MLA_CHUNK = 64
ROPE_THETA = 10000.0
MOE_GROUPS = 4
MOE_EPG = 4
MOE_EXPERTS = MOE_GROUPS * MOE_EPG
MOE_FF = 256
DEPTH = 2
DEEPNORM_ALPHA = (2.0 * DEPTH) ** 0.25
LN_EPS = 1e-5
RMS_EPS = 1e-6

LANES = 128
SSD_CHUNK = 128
ATT_TILE = 128
NEG_BIG = -1e30
VMEM_LIMIT = 56 * 1024 * 1024


def _cparams(sem):
    return pltpu.CompilerParams(dimension_semantics=sem, vmem_limit_bytes=VMEM_LIMIT)


def _dot(a, b):
    return jnp.dot(a, b, preferred_element_type=F32)


def _dot_nt(a, b):
    return lax.dot_general(a, b, (((1,), (1,)), ((), ())), preferred_element_type=F32)


def _split3(x):
    hi = x.astype(BF16)
    r1 = x - hi.astype(F32)
    mid = r1.astype(BF16)
    lo = (r1 - mid.astype(F32)).astype(BF16)
    return hi, mid, lo


def _dot01_right(x, m01):
    hi, mid, lo = _split3(x)
    return _dot(hi, m01) + _dot(mid, m01) + _dot(lo, m01)


def _dot01_left(m01, x):
    hi, mid, lo = _split3(x)
    return _dot(m01, hi) + _dot(m01, mid) + _dot(m01, lo)


def _softplus(x):
    return jnp.maximum(x, 0.0) + jnp.log(1.0 + jnp.exp(-jnp.abs(x)))


def _silu(x):
    return x * (1.0 / (1.0 + jnp.exp(-x)))


def _iota(shape, dim):
    return lax.broadcasted_iota(jnp.int32, shape, dim)


def _proj_even_kernel(x_ref, wz_ref, wx_ref, wdt_ref, wdtt_ref, wq_ref, wk_ref, wv_ref,
                      z_ref, xbc_ref, dt_ref, dtt_ref, q_ref, k_ref, v_ref):
    x = x_ref[...].astype(BF16)
    z_ref[...] = _dot(x, wz_ref[...]).astype(z_ref.dtype)
    xbc_ref[...] = _dot(x, wx_ref[...]).astype(xbc_ref.dtype)
    dt_ref[...] = _dot(x, wdt_ref[...])
    dtt_ref[...] = _dot_nt(wdtt_ref[...], x)
    q_ref[...] = _dot(x, wq_ref[...]).astype(q_ref.dtype)
    k_ref[...] = _dot(x, wk_ref[...]).astype(k_ref.dtype)
    v_ref[...] = _dot(x, wv_ref[...]).astype(v_ref.dtype)


def _proj_even(x2, w_in, tm):
    t, d = x2.shape
    o = 0
    parts = []
    for n in (SSD_INNER, SSD_CONV_DIM, SSD_HEADS, ATT_WIDTH, ATT_WIDTH, ATT_WIDTH):
        parts.append(w_in[:, o:o + n])
        o += n
    wz, wx, wdt, wq, wk, wv = parts
    wq = wq * (1.0 / math.sqrt(ATT_HEAD_DIM))
    ws = [wz.astype(BF16), wx.astype(BF16), wdt.astype(BF16), wdt.T.astype(BF16),
          wq.astype(BF16), wk.astype(BF16), wv.astype(BF16)]
    row = lambda n: pl.BlockSpec((tm, n), lambda i: (i, 0))
    full = lambda a: pl.BlockSpec(a.shape, lambda i: (0, 0))
    return pl.pallas_call(
        _proj_even_kernel,
        grid=(t // tm,),
        in_specs=[row(d)] + [full(w) for w in ws],
        out_specs=[row(SSD_INNER), row(SSD_CONV_DIM), row(SSD_HEADS),
                   pl.BlockSpec((SSD_HEADS, tm), lambda i: (0, i)),
                   row(ATT_WIDTH), row(ATT_WIDTH), row(ATT_WIDTH)],
        out_shape=[jax.ShapeDtypeStruct((t, SSD_INNER), BF16),
                   jax.ShapeDtypeStruct((t, SSD_CONV_DIM), BF16),
                   jax.ShapeDtypeStruct((t, SSD_HEADS), F32),
                   jax.ShapeDtypeStruct((SSD_HEADS, t), F32),
                   jax.ShapeDtypeStruct((t, ATT_WIDTH), BF16),
                   jax.ShapeDtypeStruct((t, ATT_WIDTH), BF16),
                   jax.ShapeDtypeStruct((t, ATT_WIDTH), BF16)],
        compiler_params=_cparams(("parallel",)),
        name="proj_even",
    )(x2, *ws)


def _proj_odd_kernel(x_ref, wcq_ref, wckv_ref, wkpe_ref, wkps_ref, wq_ref, wk_ref, wv_ref, wft_ref, fb_ref,
                     cq_ref, ckv_ref, kpe_ref, kps_ref, q_ref, k_ref, v_ref, fh_ref, carry_ref, *, tiles_per_seq):
    i = pl.program_id(0)
    tm = x_ref.shape[0]
    x = x_ref[...].astype(BF16)
    cq_ref[...] = _dot(x, wcq_ref[...])
    ckv_ref[...] = _dot(x, wckv_ref[...])
    kpe_ref[...] = _dot(x, wkpe_ref[...])
    kps_ref[...] = _dot(x, wkps_ref[...])
    q_ref[...] = _dot(x, wq_ref[...]).astype(q_ref.dtype)
    k_ref[...] = _dot(x, wk_ref[...]).astype(k_ref.dtype)
    v_ref[...] = _dot(x, wv_ref[...]).astype(v_ref.dtype)

    @pl.when(i % tiles_per_seq == 0)
    def _():
        carry_ref[...] = jnp.zeros_like(carry_ref)

    f_raw = _dot_nt(wft_ref[...], x) + fb_ref[...]
    log_f = -_softplus(-f_raw)
    upper = (_iota((tm, tm), 0) <= _iota((tm, tm), 1)).astype(BF16)
    cum = _dot01_right(log_f, upper) + carry_ref[...]
    fh_ref[...] = cum
    carry_ref[...] = cum[:, tm - 1:tm]


def _proj_odd(x2, w_in, f_bias, seq, tm):
    t, d = x2.shape
    sizes = (MLA_Q_RANK, MLA_KV_RANK, MLA_ROPE, ATT_WIDTH, ATT_WIDTH, ATT_WIDTH, ATT_HEADS)
    o = 0
    parts = []
    for n in sizes:
        parts.append(w_in[:, o:o + n])
        o += n
    wcq, wckv, wkpe, wq, wk, wv, wf = parts
    half = MLA_ROPE // 2
    pad_l = jnp.zeros((d, MLA_NOPE), F32)
    pad_r = jnp.zeros((d, LANES - MLA_NOPE - MLA_ROPE), F32)
    wkpe_p = jnp.concatenate([pad_l, wkpe, pad_r], axis=1)
    wkps_p = jnp.concatenate([pad_l, -wkpe[:, half:], wkpe[:, :half], pad_r], axis=1)
    wq = wq * (1.0 / math.sqrt(ATT_HEAD_DIM))
    ws = [wcq.astype(BF16), wckv.astype(BF16), wkpe_p.astype(BF16), wkps_p.astype(BF16),
          wq.astype(BF16), wk.astype(BF16), wv.astype(BF16), wf.T.astype(BF16)]
    fb = f_bias.reshape(ATT_HEADS, 1).astype(F32)
    row = lambda n: pl.BlockSpec((tm, n), lambda i: (i, 0))
    full = lambda a: pl.BlockSpec(a.shape, lambda i: (0, 0))
    return pl.pallas_call(
        functools.partial(_proj_odd_kernel, tiles_per_seq=seq // tm),
        grid=(t // tm,),
        in_specs=[row(d)] + [full(w) for w in ws] + [full(fb)],
        out_specs=[row(MLA_Q_RANK), row(MLA_KV_RANK), row(LANES), row(LANES),
                   row(ATT_WIDTH), row(ATT_WIDTH), row(ATT_WIDTH),
                   pl.BlockSpec((ATT_HEADS, tm), lambda i: (0, i))],
        out_shape=[jax.ShapeDtypeStruct((t, MLA_Q_RANK), F32),
                   jax.ShapeDtypeStruct((t, MLA_KV_RANK), F32),
                   jax.ShapeDtypeStruct((t, LANES), F32),
                   jax.ShapeDtypeStruct((t, LANES), F32),
                   jax.ShapeDtypeStruct((t, ATT_WIDTH), BF16),
                   jax.ShapeDtypeStruct((t, ATT_WIDTH), BF16),
                   jax.ShapeDtypeStruct((t, ATT_WIDTH), BF16),
                   jax.ShapeDtypeStruct((ATT_HEADS, t), F32)],
        scratch_shapes=[pltpu.VMEM((ATT_HEADS, 1), F32)],
        compiler_params=_cparams(("arbitrary",)),
        name="proj_odd",
    )(x2, *ws, fb)


def _ssd_kernel(xbc_ref, z_ref, dt_ref, dtt_ref, cw_ref, cb_ref, dtb_ref, dtbt_ref, alog_ref, alogt_ref,
                dskip_ref, ng_ref, o_ref, buf_ref, st_ref, y_ref):
    c = pl.program_id(1)
    L = SSD_CHUNK
    PAD = 8

    @pl.when(c == 0)
    def _():
        buf_ref[0:PAD, :] = jnp.zeros((PAD, SSD_CONV_DIM), F32)
        st_ref[...] = jnp.zeros_like(st_ref)

    buf_ref[PAD:PAD + L, :] = xbc_ref[...].astype(F32)
    acc = jnp.zeros((L, SSD_CONV_DIM), F32) + cb_ref[...]
    for kk in range(SSD_CONV):
        off = PAD - (SSD_CONV - 1) + kk
        acc = acc + buf_ref[off:off + L, :] * cw_ref[kk:kk + 1, :]
    buf_ref[0:PAD, :] = buf_ref[L:L + PAD, :]
    xbc = _silu(acc)
    b_mat = xbc[:, SSD_INNER:SSD_INNER + SSD_BC]
    c_mat = xbc[:, SSD_INNER + SSD_BC:]
    bt = jnp.transpose(b_mat).astype(BF16)
    c_bf = c_mat.astype(BF16)

    dt = _softplus(dt_ref[...] + dtb_ref[...])
    dtt = _softplus(dtt_ref[...] + dtbt_ref[...])
    da = dt * (-jnp.exp(alog_ref[...]))
    dat = dtt * (-jnp.exp(alogt_ref[...]))
    row = _iota((L, L), 0)
    col = _iota((L, L), 1)
    causal = col <= row
    a_cum = _dot01_left(causal.astype(BF16), da)
    a_cumt = _dot01_right(dat, (row <= col).astype(BF16))
    a_end = a_cum[L - 1:L, :]
    low_half = _iota((1, LANES), 1) < SSD_HEAD_DIM

    def pair_cols(m, h0):
        n = m.shape[0]
        return jnp.where(low_half, jnp.broadcast_to(m[:, h0:h0 + 1], (n, LANES)),
                         jnp.broadcast_to(m[:, h0 + 1:h0 + 2], (n, LANES)))

    sumsq = jnp.zeros((L, 1), F32)
    for g in range(SSD_GROUPS):
        bt_g = bt[g * SSD_STATE:(g + 1) * SSD_STATE, :]
        c_g = c_bf[:, g * SSD_STATE:(g + 1) * SSD_STATE]
        cb = _dot(c_g, bt_g)
        for pp in range(SSD_HEADS // SSD_GROUPS // 2):
            p = g * (SSD_HEADS // SSD_GROUPS // 2) + pp
            h0 = 2 * p
            lanes = slice(p * LANES, (p + 1) * LANES)
            xs = xbc[:, lanes]
            xs_bf = xs.astype(BF16)
            yd = []
            for h in (h0, h0 + 1):
                seg = a_cum[:, h:h + 1] - a_cumt[h:h + 1, :]
                dec = jnp.where(causal, jnp.exp(jnp.minimum(seg, 0.0)), 0.0)
                m = cb * dec * dtt[h:h + 1, :]
                yd.append(_dot(m.astype(BF16), xs_bf))
            y = jnp.where(low_half, yd[0], yd[1])
            a_p = pair_cols(a_cum, h0)
            dt_p = pair_cols(dt, h0)
            end_p = pair_cols(a_end, h0)
            st = st_ref[p]
            y = y + _dot(c_g, st.astype(BF16)) * jnp.exp(a_p)
            xw = xs * (jnp.exp(end_p - a_p) * dt_p)
            st_ref[p] = st * jnp.exp(end_p) + _dot(bt_g, xw.astype(BF16))
            y = y + dskip_ref[:, lanes] * xs
            y = y * _silu(z_ref[:, lanes].astype(F32))
            y_ref[:, lanes] = y
            sumsq = sumsq + jnp.sum(y * y, axis=1, keepdims=True)
    inv = lax.rsqrt(sumsq * (1.0 / SSD_INNER) + RMS_EPS)
    o_ref[...] = (y_ref[...] * inv * ng_ref[...]).astype(o_ref.dtype)


def _ssd(xbc, z, dt, dtt, conv_w, conv_b, dt_bias, a_log, d_skip, norm_g, batch, seq):
    t = xbc.shape[0]
    L = SSD_CHUNK
    nc = seq // L
    row = lambda n: pl.BlockSpec((L, n), lambda b, c: (b * nc + c, 0))
    full = lambda a: pl.BlockSpec(a.shape, lambda b, c: (0,) * a.ndim)
    params = [conv_w.astype(F32), conv_b.reshape(1, -1).astype(F32),
              dt_bias.reshape(1, -1).astype(F32), dt_bias.reshape(-1, 1).astype(F32),
              a_log.reshape(1, -1).astype(F32), a_log.reshape(-1, 1).astype(F32),
              jnp.repeat(d_skip.astype(F32), SSD_HEAD_DIM).reshape(1, -1),
              norm_g.reshape(1, -1).astype(F32)]
    return pl.pallas_call(
        _ssd_kernel,
        grid=(batch, nc),
        in_specs=[row(SSD_CONV_DIM), row(SSD_INNER), row(SSD_HEADS),
                  pl.BlockSpec((SSD_HEADS, L), lambda b, c: (0, b * nc + c))] + [full(a) for a in params],
        out_specs=row(SSD_INNER),
        out_shape=jax.ShapeDtypeStruct((t, SSD_INNER), BF16),
        scratch_shapes=[pltpu.VMEM((L + 8, SSD_CONV_DIM), F32),
                        pltpu.VMEM((SSD_HEADS // 2, SSD_STATE, LANES), F32),
                        pltpu.VMEM((L, SSD_INNER), F32)],
        compiler_params=_cparams(("parallel", "arbitrary")),
        name="ssd",
    )(xbc, z, dt, dtt, *params)


def _attention_kernel(*refs, mode):
    if mode == "fox":
        q_ref, k_ref, v_ref, fh_ref, o_ref, acc_ref, m_ref, l_ref = refs
    else:
        q_ref, k_ref, v_ref, o_ref, acc_ref, m_ref, l_ref = refs
        fh_ref = None
    tq = ATT_TILE
    tk = ATT_TILE
    i = pl.program_id(2)
    lane = _iota((1, LANES), 1)
    low_half = lane < ATT_HEAD_DIM
    row = _iota((tq, tk), 0)
    col = _iota((tq, tk), 1)
    if mode == "sb":
        diag_mask = col < row
        u_ext = jnp.concatenate([(_iota((tk, tk), 0) > _iota((tk, tk), 1)).astype(BF16),
                                 jnp.ones((tk, LANES), BF16)], axis=1)
    elif mode == "mla":
        diag_mask = (col // MLA_CHUNK) <= (row // MLA_CHUNK)
    else:
        diag_mask = col <= row

    q = q_ref[...]
    if mode == "mla":
        qs = [q[:, 0:LANES], q[:, LANES:2 * LANES]]
    else:
        zero = jnp.zeros_like(q)
        qs = [jnp.where(low_half, q, zero), jnp.where(low_half, zero, q)]

    acc_ref[...] = jnp.zeros_like(acc_ref)
    if mode == "sb":
        m_ref[...] = jnp.zeros_like(m_ref)
    else:
        m_ref[...] = jnp.full(m_ref.shape, NEG_BIG, F32)
        l_ref[...] = jnp.zeros_like(l_ref)

    if mode == "fox":
        q0 = pl.multiple_of(i * tq, tq)
        fh_q = [fh_ref[hh:hh + 1, pl.ds(q0, LANES)][:, 0:1] for hh in range(2)]

    def tile(j, masked):
        k0 = pl.multiple_of(j * tk, tk)
        k_t = k_ref[pl.ds(k0, tk), :]
        v_t = v_ref[pl.ds(k0, tk), :]
        for hh in range(2):
            if mode == "mla":
                z = _dot_nt(qs[hh], k_t[:, hh * LANES:(hh + 1) * LANES])
            else:
                z = _dot_nt(qs[hh], k_t)
            if mode == "sb":
                sp = _softplus(z)
                l1m = -sp
                if masked:
                    l1m = jnp.where(diag_mask, l1m, 0.0)
                hi = l1m.astype(BF16)
                lo = (l1m - hi.astype(F32)).astype(BF16)
                cs = _dot(hi, u_ext) + _dot(lo, u_ext)
                run = m_ref[hh]
                w = jnp.exp((z - sp) + cs[:, 0:tk] + run)
                if masked:
                    w = jnp.where(diag_mask, w, 0.0)
                acc_ref[hh] = acc_ref[hh] + _dot(w.astype(BF16), v_t)
                m_ref[hh] = run + cs[:, tk:tk + LANES]
            else:
                if mode == "fox":
                    z = z + (fh_q[hh] - fh_ref[hh:hh + 1, pl.ds(k0, tk)])
                if masked:
                    z = jnp.where(diag_mask, z, NEG_BIG)
                m_old = m_ref[hh]
                m_new = jnp.maximum(m_old, jnp.max(z, axis=1, keepdims=True))
                p = jnp.exp(z - m_new)
                alpha = jnp.exp(m_old - m_new)
                l_ref[hh] = alpha * l_ref[hh] + jnp.sum(p, axis=1, keepdims=True)
                acc_ref[hh] = alpha * acc_ref[hh] + _dot(p.astype(BF16), v_t)
                m_ref[hh] = m_new

    if mode == "sb":
        tile(i, True)

        def body(it, carry):
            tile(i - 1 - it, False)
            return carry
        lax.fori_loop(0, i, body, 0)
        out = [acc_ref[0], acc_ref[1]]
    else:
        def body(j, carry):
            tile(j, False)
            return carry
        lax.fori_loop(0, i, body, 0)
        tile(i, True)
        out = [acc_ref[hh] / l_ref[hh] for hh in range(2)]
    o_ref[...] = jnp.where(low_half, out[0], out[1]).astype(o_ref.dtype)


def _attention(mode, q, k, v, batch, seq, fh=None):
    t = v.shape[0]
    tq = ATT_TILE
    nq = seq // tq
    qk_w = 2 * LANES if mode == "mla" else LANES
    in_specs = [pl.BlockSpec((tq, qk_w), lambda b, h, i: (b * nq + i, h)),
                pl.BlockSpec((seq, qk_w), lambda b, h, i: (b, h)),
                pl.BlockSpec((seq, LANES), lambda b, h, i: (b, h))]
    args = [q, k, v]
    if mode == "fox":
        in_specs.append(pl.BlockSpec((None, 2, seq), lambda b, h, i: (h, 0, b)))
        args.append(fh.reshape(HEAD_PAIRS, 2, t))
    return pl.pallas_call(
        functools.partial(_attention_kernel, mode=mode),
        grid=(batch, HEAD_PAIRS, nq),
        in_specs=in_specs,
        out_specs=pl.BlockSpec((tq, LANES), lambda b, h, i: (b * nq + i, h)),
        out_shape=jax.ShapeDtypeStruct((t, ATT_WIDTH), BF16),
        scratch_shapes=[pltpu.VMEM((2, tq, LANES), F32),
                        pltpu.VMEM((2, tq, LANES), F32),
                        pltpu.VMEM((2, tq, LANES), F32)],
        compiler_params=_cparams(("parallel", "parallel", "arbitrary")),
        name="attn_" + mode,
    )(*args)


def _mla_prep_kernel(cq_ref, ckv_ref, kpe_ref, kps_ref, cos_ref, sin_ref, qg_ref, kvg_ref,
                     wqm_ref, wqs_ref, wk_ref, wv_ref, q_ref, k_ref, v_ref):
    def rms(x, g):
        return x * lax.rsqrt(jnp.mean(x * x, axis=1, keepdims=True) + RMS_EPS) * g

    cqn = rms(cq_ref[...], qg_ref[...]).astype(BF16)
    ckvn = rms(ckv_ref[...], kvg_ref[...]).astype(BF16)
    cos1 = cos_ref[...]
    sin1 = sin_ref[...]
    cos8 = jnp.concatenate([cos1] * ATT_HEADS, axis=1)
    sin8 = jnp.concatenate([sin1] * ATT_HEADS, axis=1)
    qm = _dot(cqn, wqm_ref[...])
    qsw = _dot(cqn, wqs_ref[...])
    scale = 1.0 / math.sqrt(MLA_NOPE + MLA_ROPE)
    q_ref[...] = ((qm * cos8 + qsw * sin8) * scale).astype(q_ref.dtype)
    k_rope = kpe_ref[...] * cos1 + kps_ref[...] * sin1
    k_ref[...] = (_dot(ckvn, wk_ref[...]) + jnp.concatenate([k_rope] * ATT_HEADS, axis=1)).astype(k_ref.dtype)
    v_ref[...] = _dot(ckvn, wv_ref[...]).astype(v_ref.dtype)


def _mla_prep(cq, ckv, kpe, kps, q_norm_g, w_q_up, kv_norm_g, w_kv_up, seq, tm):
    t = cq.shape[0]
    half = MLA_ROPE // 2
    inv_freq = ROPE_THETA ** (-(jnp.arange(0, MLA_ROPE, 2, dtype=F32) / MLA_ROPE))
    ang = jnp.arange(seq, dtype=F32)[:, None] * inv_freq[None, :]
    cos, sin = jnp.cos(ang), jnp.sin(ang)
    pad = LANES - MLA_NOPE - MLA_ROPE
    cos1 = jnp.concatenate([jnp.ones((seq, MLA_NOPE), F32), cos, cos, jnp.zeros((seq, pad), F32)], axis=1)
    sin1 = jnp.concatenate([jnp.zeros((seq, MLA_NOPE), F32), sin, sin, jnp.zeros((seq, pad), F32)], axis=1)
    wq = w_q_up.reshape(MLA_Q_RANK, ATT_HEADS, MLA_NOPE + MLA_ROPE)
    zq = lambda n: jnp.zeros((MLA_Q_RANK, ATT_HEADS, n), F32)
    wq_main = jnp.concatenate([wq, zq(pad)], axis=2).reshape(MLA_Q_RANK, ATT_HEADS * LANES)
    wq_swap = jnp.concatenate([zq(MLA_NOPE), -wq[:, :, MLA_NOPE + half:], wq[:, :, MLA_NOPE:MLA_NOPE + half],
                               zq(pad)], axis=2).reshape(MLA_Q_RANK, ATT_HEADS * LANES)
    wkv = w_kv_up.reshape(MLA_KV_RANK, ATT_HEADS, MLA_NOPE + MLA_V)
    wk = jnp.concatenate([wkv[:, :, :MLA_NOPE], jnp.zeros((MLA_KV_RANK, ATT_HEADS, LANES - MLA_NOPE), F32)],
                         axis=2).reshape(MLA_KV_RANK, ATT_HEADS * LANES)
    wv = wkv[:, :, MLA_NOPE:].reshape(MLA_KV_RANK, ATT_HEADS * MLA_V)
    consts = [q_norm_g.reshape(1, -1).astype(F32), kv_norm_g.reshape(1, -1).astype(F32),
              wq_main.astype(BF16), wq_swap.astype(BF16), wk.astype(BF16), wv.astype(BF16)]
    ns = seq // tm
    row = lambda n: pl.BlockSpec((tm, n), lambda i: (i, 0))
    tab = pl.BlockSpec((tm, LANES), lambda i: (i % ns, 0))
    full = lambda a: pl.BlockSpec(a.shape, lambda i: (0, 0))
    return pl.pallas_call(
        _mla_prep_kernel,
        grid=(t // tm,),
        in_specs=[row(MLA_Q_RANK), row(MLA_KV_RANK), row(LANES), row(LANES), tab, tab] + [full(a) for a in consts],
        out_specs=[row(ATT_HEADS * LANES), row(ATT_HEADS * LANES), row(ATT_HEADS * MLA_V)],
        out_shape=[jax.ShapeDtypeStruct((t, ATT_HEADS * LANES), BF16),
                   jax.ShapeDtypeStruct((t, ATT_HEADS * LANES), BF16),
                   jax.ShapeDtypeStruct((t, ATT_HEADS * MLA_V), BF16)],
        compiler_params=_cparams(("parallel",)),
        name="mla_prep",
    )(cq, ckv, kpe, kps, cos1, sin1, *consts)


def _layer_norm_rows(r, g, b):
    mu = jnp.mean(r, axis=1, keepdims=True)
    d = r - mu
    var = jnp.mean(d * d, axis=1, keepdims=True)
    return d * lax.rsqrt(var + LN_EPS) * g + b


def _outproj_ln_kernel(a_ref, b_ref, x_ref, wa_ref, wb_ref, g_ref, beta_ref, o_ref):
    y = _dot(a_ref[...], wa_ref[...]) + _dot(b_ref[...], wb_ref[...])
    r = DEEPNORM_ALPHA * x_ref[...] + y
    o_ref[...] = _layer_norm_rows(r, g_ref[...], beta_ref[...])


def _outproj_ln(a, b, x2, w_out, ln_g, ln_b, tm):
    t, d = x2.shape
    ka, kb = a.shape[1], b.shape[1]
    consts = [w_out[:ka].astype(BF16), w_out[ka:].astype(BF16),
              ln_g.reshape(1, -1).astype(F32), ln_b.reshape(1, -1).astype(F32)]
    row = lambda n: pl.BlockSpec((tm, n), lambda i: (i, 0))
    full = lambda c: pl.BlockSpec(c.shape, lambda i: (0, 0))
    return pl.pallas_call(
        _outproj_ln_kernel,
        grid=(t // tm,),
        in_specs=[row(ka), row(kb), row(d)] + [full(c) for c in consts],
        out_specs=row(d),
        out_shape=jax.ShapeDtypeStruct((t, d), F32),
        compiler_params=_cparams(("parallel",)),
        name="outproj_ln",
    )(a, b, x2, *consts)


def _router_gate(x, wg, bg, we, be):
    hp = lax.Precision.HIGHEST
    tm = x.shape[0]
    gl = jnp.dot(x, wg, precision=hp, preferred_element_type=F32) + bg
    el = jnp.dot(x, we, precision=hp, preferred_element_type=F32) + be
    g_max = jnp.max(gl, axis=1, keepdims=True)
    g_p = 1.0 / jnp.sum(jnp.exp(gl - g_max), axis=1, keepdims=True)
    g_idx = _iota((tm, MOE_GROUPS), 1)
    g_sel = jnp.min(jnp.where(gl == g_max, g_idx, MOE_GROUPS), axis=1, keepdims=True)
    e_idx = _iota((tm, MOE_EXPERTS), 1)
    in_group = (e_idx // MOE_EPG) == g_sel
    masked = jnp.where(in_group, el, -jnp.inf)
    m1 = jnp.max(masked, axis=1, keepdims=True)
    i1 = jnp.min(jnp.where(masked == m1, e_idx, MOE_EXPERTS), axis=1, keepdims=True)
    masked2 = jnp.where(e_idx == i1, -jnp.inf, masked)
    m2 = jnp.max(masked2, axis=1, keepdims=True)
    i2 = jnp.min(jnp.where(masked2 == m2, e_idx, MOE_EXPERTS), axis=1, keepdims=True)
    e2 = jnp.exp(m2 - m1)
    w1 = g_p / (1.0 + e2)
    w2 = w1 * e2
    return jnp.where(e_idx == i1, w1, 0.0) + jnp.where(e_idx == i2, w2, 0.0)


def _moe_dense_kernel(x_ref, wgrp_ref, bgrp_ref, wexp_ref, bexp_ref, wg_ref, wu_ref, wd_ref, g_ref, beta_ref,
                      o_ref, gate_ref, acc_ref, xb_ref):
    e = pl.program_id(1)

    @pl.when(e == 0)
    def _():
        x = x_ref[...]
        gate_ref[...] = _router_gate(x, wgrp_ref[...], bgrp_ref[...], wexp_ref[...], bexp_ref[...])
        xb_ref[...] = x.astype(BF16)
        acc_ref[...] = jnp.zeros_like(acc_ref)

    xb = xb_ref[...]
    gate = gate_ref[...]
    ge = jnp.sum(jnp.where(_iota(gate.shape, 1) == e, gate, 0.0), axis=1, keepdims=True)
    hid = _silu(_dot(xb, wg_ref[...])) * _dot(xb, wu_ref[...])
    acc_ref[...] += _dot((hid * ge).astype(BF16), wd_ref[...])

    @pl.when(e == MOE_EXPERTS - 1)
    def _():
        r = DEEPNORM_ALPHA * x_ref[...] + acc_ref[...]
        o_ref[...] = _layer_norm_rows(r, g_ref[...], beta_ref[...])


def _moe_dense(x2, w_group, b_group, w_expert, b_expert, w_gate, w_up, w_down, ln_g, ln_b, tm):
    t, d = x2.shape
    small = [w_group.astype(F32), b_group.reshape(1, -1).astype(F32),
             w_expert.astype(F32), b_expert.reshape(1, -1).astype(F32)]
    ln = [ln_g.reshape(1, -1).astype(F32), ln_b.reshape(1, -1).astype(F32)]
    row = pl.BlockSpec((tm, d), lambda i, e: (i, 0))
    full = lambda c: pl.BlockSpec(c.shape, lambda i, e: (0, 0))
    return pl.pallas_call(
        _moe_dense_kernel,
        grid=(t // tm, MOE_EXPERTS),
        in_specs=[row] + [full(c) for c in small]
        + [pl.BlockSpec((None, d, MOE_FF), lambda i, e: (e, 0, 0)),
           pl.BlockSpec((None, d, MOE_FF), lambda i, e: (e, 0, 0)),
           pl.BlockSpec((None, MOE_FF, d), lambda i, e: (e, 0, 0))]
        + [full(c) for c in ln],
        out_specs=row,
        out_shape=jax.ShapeDtypeStruct((t, d), F32),
        scratch_shapes=[pltpu.VMEM((tm, MOE_EXPERTS), F32),
                        pltpu.VMEM((tm, d), F32),
                        pltpu.VMEM((tm, d), BF16)],
        compiler_params=_cparams(("parallel", "arbitrary")),
        name="moe_dense",
    )(x2, *small, w_gate.astype(BF16), w_up.astype(BF16), w_down.astype(BF16), *ln)


def _row_tile(n, pref):
    tm = min(pref, n)
    assert n % tm == 0
    return tm


def kernel(x, ev_w_in, ev_conv_w, ev_conv_b, ev_dt_bias, ev_a_log, ev_d_skip, ev_norm_g, ev_w_out, od_w_in, od_q_norm_g, od_w_q_up, od_kv_norm_g, od_w_kv_up, od_f_bias, od_w_out, ln1_g, ln1_b, ln2_g, ln2_b, moe_w_group, moe_b_group, moe_w_expert, moe_b_expert, moe_w_gate, moe_w_up, moe_w_down):
    batch, seq, d = x.shape
    t = batch * seq
    assert seq % SSD_CHUNK == 0 and seq % ATT_TILE == 0
    tm_proj = _row_tile(seq, 512)
    tm_moe = _row_tile(t, 1024)
    x2 = x.reshape(t, d)

    z, xbc, dt, dtt, q, k, v = _proj_even(x2, ev_w_in[0], tm_proj)
    y_ssd = _ssd(xbc, z, dt, dtt, ev_conv_w[0], ev_conv_b[0], ev_dt_bias[0], ev_a_log[0], ev_d_skip[0],
                 ev_norm_g[0], batch, seq)
    y_sb = _attention("sb", q, k, v, batch, seq)
    x2 = _outproj_ln(y_ssd, y_sb, x2, ev_w_out[0], ln1_g[0], ln1_b[0], tm_proj)
    x2 = _moe_dense(x2, moe_w_group[0], moe_b_group[0], moe_w_expert[0], moe_b_expert[0],
                    moe_w_gate[0], moe_w_up[0], moe_w_down[0], ln2_g[0], ln2_b[0], tm_moe)

    cq, ckv, kpe, kps, q, k, v, fh = _proj_odd(x2, od_w_in[0], od_f_bias[0], seq, tm_proj)
    qm, km, vm = _mla_prep(cq, ckv, kpe, kps, od_q_norm_g[0], od_w_q_up[0], od_kv_norm_g[0], od_w_kv_up[0],
                           seq, tm_proj)
    y_mla = _attention("mla", qm, km, vm, batch, seq)
    y_fox = _attention("fox", q, k, v, batch, seq, fh=fh)
    x2 = _outproj_ln(y_mla, y_fox, x2, od_w_out[0], ln1_g[1], ln1_b[1], tm_proj)
    x2 = _moe_dense(x2, moe_w_group[1], moe_b_group[1], moe_w_expert[1], moe_b_expert[1],
                    moe_w_gate[1], moe_w_up[1], moe_w_down[1], ln2_g[1], ln2_b[1], tm_moe)
    return x2.reshape(batch, seq, d)
```

```python
import functools
import math

import jax
import jax.numpy as jnp
from jax import lax
from jax.experimental import pallas as pl
from jax.experimental.pallas import tpu as pltpu

F32 = jnp.float32
BF16 = jnp.bfloat16

SSD_HEADS = 16
SSD_HEAD_DIM = 64
SSD_INNER = SSD_HEADS * SSD_HEAD_DIM
SSD_GROUPS = 2
SSD_STATE = 128
SSD_CONV = 4
SSD_BC = SSD_GROUPS * SSD_STATE
SSD_CONV_DIM = SSD_INNER + 2 * SSD_BC
ATT_HEADS = 8
ATT_HEAD_DIM = 64
ATT_WIDTH = ATT_HEADS * ATT_HEAD_DIM
HEAD_PAIRS = ATT_HEADS // 2
MLA_Q_RANK = 256
MLA_KV_RANK = 128
MLA_NOPE = 64
MLA_ROPE = 32
MLA_V = 64
MLA_CHUNK = 64
ROPE_THETA = 10000.0
MOE_GROUPS = 4
MOE_EPG = 4
MOE_EXPERTS = MOE_GROUPS * MOE_EPG
MOE_FF = 256
DEPTH = 2
DEEPNORM_ALPHA = (2.0 * DEPTH) ** 0.25
LN_EPS = 1e-5
RMS_EPS = 1e-6

LANES = 128
SSD_CHUNK = 128
ATT_TQ = {"sb": 512, "mla": 512, "fox": 512}
ATT_TK = {"sb": 256, "mla": 512, "fox": 512}
NEG_BIG = -1e30
VMEM_LIMIT = 56 * 1024 * 1024


def _cparams(sem):
    return pltpu.CompilerParams(dimension_semantics=sem, vmem_limit_bytes=VMEM_LIMIT)


def _dot(a, b):
    return jnp.dot(a, b, preferred_element_type=F32)


def _dot_nt(a, b):
    return lax.dot_general(a, b, (((1,), (1,)), ((), ())), preferred_element_type=F32)


def _split3(x):
    hi = x.astype(BF16)
    r1 = x - hi.astype(F32)
    mid = r1.astype(BF16)
    lo = (r1 - mid.astype(F32)).astype(BF16)
    return hi, mid, lo


def _dot01_right(x, m01):
    hi, mid, lo = _split3(x)
    return _dot(hi, m01) + _dot(mid, m01) + _dot(lo, m01)


def _dot01_left(m01, x):
    hi, mid, lo = _split3(x)
    return _dot(m01, hi) + _dot(m01, mid) + _dot(m01, lo)


def _softplus(x):
    return jnp.maximum(x, 0.0) + jnp.log(1.0 + jnp.exp(-jnp.abs(x)))


def _silu(x):
    return x * (1.0 / (1.0 + jnp.exp(-x)))


def _iota(shape, dim):
    return lax.broadcasted_iota(jnp.int32, shape, dim)


def _proj_even_kernel(x_ref, wz_ref, wx_ref, wdt_ref, wdtt_ref, wq_ref, wk_ref, wv_ref,
                      z_ref, xbc_ref, dt_ref, dtt_ref, q_ref, k_ref, v_ref):
    x = x_ref[...].astype(BF16)
    z_ref[...] = _dot(x, wz_ref[...]).astype(z_ref.dtype)
    xbc_ref[...] = _dot(x, wx_ref[...]).astype(xbc_ref.dtype)
    dt_ref[...] = _dot(x, wdt_ref[...])
    dtt_ref[...] = _dot_nt(wdtt_ref[...], x)
    q_ref[...] = _dot(x, wq_ref[...]).astype(q_ref.dtype)
    k_ref[...] = _dot(x, wk_ref[...]).astype(k_ref.dtype)
    v_ref[...] = _dot(x, wv_ref[...]).astype(v_ref.dtype)


def _proj_even(x2, w_in, tm):
    t, d = x2.shape
    o = 0
    parts = []
    for n in (SSD_INNER, SSD_CONV_DIM, SSD_HEADS, ATT_WIDTH, ATT_WIDTH, ATT_WIDTH):
        parts.append(w_in[:, o:o + n])
        o += n
    wz, wx, wdt, wq, wk, wv = parts
    wq = wq * (1.0 / math.sqrt(ATT_HEAD_DIM))
    ws = [wz.astype(BF16), wx.astype(BF16), wdt.astype(BF16), wdt.T.astype(BF16),
          wq.astype(BF16), wk.astype(BF16), wv.astype(BF16)]
    row = lambda n: pl.BlockSpec((tm, n), lambda i: (i, 0))
    full = lambda a: pl.BlockSpec(a.shape, lambda i: (0, 0))
    return pl.pallas_call(
        _proj_even_kernel,
        grid=(t // tm,),
        in_specs=[row(d)] + [full(w) for w in ws],
        out_specs=[row(SSD_INNER), row(SSD_CONV_DIM), row(SSD_HEADS),
                   pl.BlockSpec((SSD_HEADS, tm), lambda i: (0, i)),
                   row(ATT_WIDTH), row(ATT_WIDTH), row(ATT_WIDTH)],
        out_shape=[jax.ShapeDtypeStruct((t, SSD_INNER), BF16),
                   jax.ShapeDtypeStruct((t, SSD_CONV_DIM), BF16),
                   jax.ShapeDtypeStruct((t, SSD_HEADS), F32),
                   jax.ShapeDtypeStruct((SSD_HEADS, t), F32),
                   jax.ShapeDtypeStruct((t, ATT_WIDTH), BF16),
                   jax.ShapeDtypeStruct((t, ATT_WIDTH), BF16),
                   jax.ShapeDtypeStruct((t, ATT_WIDTH), BF16)],
        compiler_params=_cparams(("parallel",)),
        name="proj_even",
    )(x2, *ws)


def _proj_odd_kernel(x_ref, wcq_ref, wckv_ref, wkpe_ref, wkps_ref, wq_ref, wk_ref, wv_ref, wft_ref, fb_ref,
                     cq_ref, ckv_ref, kpe_ref, kps_ref, q_ref, k_ref, v_ref, fh_ref, carry_ref, *, tiles_per_seq):
    i = pl.program_id(0)
    tm = x_ref.shape[0]
    x = x_ref[...].astype(BF16)
    cq_ref[...] = _dot(x, wcq_ref[...])
    ckv_ref[...] = _dot(x, wckv_ref[...])
    kpe_ref[...] = _dot(x, wkpe_ref[...])
    kps_ref[...] = _dot(x, wkps_ref[...])
    q_ref[...] = _dot(x, wq_ref[...]).astype(q_ref.dtype)
    k_ref[...] = _dot(x, wk_ref[...]).astype(k_ref.dtype)
    v_ref[...] = _dot(x, wv_ref[...]).astype(v_ref.dtype)

    @pl.when(i % tiles_per_seq == 0)
    def _():
        carry_ref[...] = jnp.zeros_like(carry_ref)

    f_raw = _dot_nt(wft_ref[...], x) + fb_ref[...]
    log_f = -_softplus(-f_raw)
    upper = (_iota((tm, tm), 0) <= _iota((tm, tm), 1)).astype(BF16)
    cum = _dot01_right(log_f, upper) + carry_ref[...]
    fh_ref[...] = cum
    carry_ref[...] = cum[:, tm - 1:tm]


def _proj_odd(x2, w_in, f_bias, seq, tm):
    t, d = x2.shape
    sizes = (MLA_Q_RANK, MLA_KV_RANK, MLA_ROPE, ATT_WIDTH, ATT_WIDTH, ATT_WIDTH, ATT_HEADS)
    o = 0
    parts = []
    for n in sizes:
        parts.append(w_in[:, o:o + n])
        o += n
    wcq, wckv, wkpe, wq, wk, wv, wf = parts
    half = MLA_ROPE // 2
    pad_l = jnp.zeros((d, MLA_NOPE), F32)
    pad_r = jnp.zeros((d, LANES - MLA_NOPE - MLA_ROPE), F32)
    wkpe_p = jnp.concatenate([pad_l, wkpe, pad_r], axis=1)
    wkps_p = jnp.concatenate([pad_l, -wkpe[:, half:], wkpe[:, :half], pad_r], axis=1)
    wq = wq * (1.0 / math.sqrt(ATT_HEAD_DIM))
    ws = [wcq.astype(BF16), wckv.astype(BF16), wkpe_p.astype(BF16), wkps_p.astype(BF16),
          wq.astype(BF16), wk.astype(BF16), wv.astype(BF16), wf.T.astype(BF16)]
    fb = f_bias.reshape(ATT_HEADS, 1).astype(F32)
    row = lambda n: pl.BlockSpec((tm, n), lambda i: (i, 0))
    full = lambda a: pl.BlockSpec(a.shape, lambda i: (0, 0))
    return pl.pallas_call(
        functools.partial(_proj_odd_kernel, tiles_per_seq=seq // tm),
        grid=(t // tm,),
        in_specs=[row(d)] + [full(w) for w in ws] + [full(fb)],
        out_specs=[row(MLA_Q_RANK), row(MLA_KV_RANK), row(LANES), row(LANES),
                   row(ATT_WIDTH), row(ATT_WIDTH), row(ATT_WIDTH),
                   pl.BlockSpec((ATT_HEADS, tm), lambda i: (0, i))],
        out_shape=[jax.ShapeDtypeStruct((t, MLA_Q_RANK), F32),
                   jax.ShapeDtypeStruct((t, MLA_KV_RANK), F32),
                   jax.ShapeDtypeStruct((t, LANES), F32),
                   jax.ShapeDtypeStruct((t, LANES), F32),
                   jax.ShapeDtypeStruct((t, ATT_WIDTH), BF16),
                   jax.ShapeDtypeStruct((t, ATT_WIDTH), BF16),
                   jax.ShapeDtypeStruct((t, ATT_WIDTH), BF16),
                   jax.ShapeDtypeStruct((ATT_HEADS, t), F32)],
        scratch_shapes=[pltpu.VMEM((ATT_HEADS, 1), F32)],
        compiler_params=_cparams(("arbitrary",)),
        name="proj_odd",
    )(x2, *ws, fb)


def _ssd_kernel(xbc_ref, z_ref, dt_ref, dtt_ref, cw_ref, cb_ref, dtb_ref, dtbt_ref, alog_ref, alogt_ref,
                dskip_ref, ng_ref, o_ref, buf_ref, st_ref, y_ref):
    c = pl.program_id(1)
    L = SSD_CHUNK
    PAD = 8

    @pl.when(c == 0)
    def _():
        buf_ref[0:PAD, :] = jnp.zeros((PAD, SSD_CONV_DIM), F32)
        st_ref[...] = jnp.zeros_like(st_ref)

    buf_ref[PAD:PAD + L, :] = xbc_ref[...].astype(F32)
    acc = jnp.zeros((L, SSD_CONV_DIM), F32) + cb_ref[...]
    for kk in range(SSD_CONV):
        off = PAD - (SSD_CONV - 1) + kk
        acc = acc + buf_ref[off:off + L, :] * cw_ref[kk:kk + 1, :]
    buf_ref[0:PAD, :] = buf_ref[L:L + PAD, :]
    xbc = _silu(acc)
    b_mat = xbc[:, SSD_INNER:SSD_INNER + SSD_BC]
    c_mat = xbc[:, SSD_INNER + SSD_BC:]
    bt = jnp.transpose(b_mat).astype(BF16)
    c_bf = c_mat.astype(BF16)

    dt = _softplus(dt_ref[...] + dtb_ref[...])
    dtt = _softplus(dtt_ref[...] + dtbt_ref[...])
    da = dt * (-jnp.exp(alog_ref[...]))
    dat = dtt * (-jnp.exp(alogt_ref[...]))
    row = _iota((L, L), 0)
    col = _iota((L, L), 1)
    causal = col <= row
    a_cum = _dot01_left(causal.astype(BF16), da)
    a_cumt = _dot01_right(dat, (row <= col).astype(BF16))
    a_end = a_cum[L - 1:L, :]
    low_half = _iota((1, LANES), 1) < SSD_HEAD_DIM

    def pair_cols(m, h0):
        n = m.shape[0]
        return jnp.where(low_half, jnp.broadcast_to(m[:, h0:h0 + 1], (n, LANES)),
                         jnp.broadcast_to(m[:, h0 + 1:h0 + 2], (n, LANES)))

    sumsq = jnp.zeros((L, 1), F32)
    for g in range(SSD_GROUPS):
        bt_g = bt[g * SSD_STATE:(g + 1) * SSD_STATE, :]
        c_g = c_bf[:, g * SSD_STATE:(g + 1) * SSD_STATE]
        cb = _dot(c_g, bt_g)
        for pp in range(SSD_HEADS // SSD_GROUPS // 2):
            p = g * (SSD_HEADS // SSD_GROUPS // 2) + pp
            h0 = 2 * p
            lanes = slice(p * LANES, (p + 1) * LANES)
            xs = xbc[:, lanes]
            xs_bf = xs.astype(BF16)
            yd = []
            for h in (h0, h0 + 1):
                seg = a_cum[:, h:h + 1] - a_cumt[h:h + 1, :]
                dec = jnp.where(causal, jnp.exp(jnp.minimum(seg, 0.0)), 0.0)
                m = cb * dec * dtt[h:h + 1, :]
                yd.append(_dot(m.astype(BF16), xs_bf))
            y = jnp.where(low_half, yd[0], yd[1])
            a_p = pair_cols(a_cum, h0)
            dt_p = pair_cols(dt, h0)
            end_p = pair_cols(a_end, h0)
            st = st_ref[p]
            y = y + _dot(c_g, st.astype(BF16)) * jnp.exp(a_p)
            xw = xs * (jnp.exp(end_p - a_p) * dt_p)
            st_ref[p] = st * jnp.exp(end_p) + _dot(bt_g, xw.astype(BF16))
            y = y + dskip_ref[:, lanes] * xs
            y = y * _silu(z_ref[:, lanes].astype(F32))
            y_ref[:, lanes] = y
            sumsq = sumsq + jnp.sum(y * y, axis=1, keepdims=True)
    inv = lax.rsqrt(sumsq * (1.0 / SSD_INNER) + RMS_EPS)
    o_ref[...] = (y_ref[...] * inv * ng_ref[...]).astype(o_ref.dtype)


def _ssd(xbc, z, dt, dtt, conv_w, conv_b, dt_bias, a_log, d_skip, norm_g, batch, seq):
    t = xbc.shape[0]
    L = SSD_CHUNK
    nc = seq // L
    row = lambda n: pl.BlockSpec((L, n), lambda b, c: (b * nc + c, 0))
    full = lambda a: pl.BlockSpec(a.shape, lambda b, c: (0,) * a.ndim)
    params = [conv_w.astype(F32), conv_b.reshape(1, -1).astype(F32),
              dt_bias.reshape(1, -1).astype(F32), dt_bias.reshape(-1, 1).astype(F32),
              a_log.reshape(1, -1).astype(F32), a_log.reshape(-1, 1).astype(F32),
              jnp.repeat(d_skip.astype(F32), SSD_HEAD_DIM).reshape(1, -1),
              norm_g.reshape(1, -1).astype(F32)]
    return pl.pallas_call(
        _ssd_kernel,
        grid=(batch, nc),
        in_specs=[row(SSD_CONV_DIM), row(SSD_INNER), row(SSD_HEADS),
                  pl.BlockSpec((SSD_HEADS, L), lambda b, c: (0, b * nc + c))] + [full(a) for a in params],
        out_specs=row(SSD_INNER),
        out_shape=jax.ShapeDtypeStruct((t, SSD_INNER), BF16),
        scratch_shapes=[pltpu.VMEM((L + 8, SSD_CONV_DIM), F32),
                        pltpu.VMEM((SSD_HEADS // 2, SSD_STATE, LANES), F32),
                        pltpu.VMEM((L, SSD_INNER), F32)],
        compiler_params=_cparams(("parallel", "arbitrary")),
        name="ssd",
    )(xbc, z, dt, dtt, *params)


def _attention_kernel(*refs, mode, tq, tk):
    if mode == "fox":
        q_ref, k_ref, v_ref, fh_ref, o_ref, acc_ref, m_ref, l_ref = refs
    else:
        q_ref, k_ref, v_ref, o_ref, acc_ref, m_ref, l_ref = refs
        fh_ref = None
    i = pl.program_id(2)
    n_diag = tq // tk
    lane = _iota((1, LANES), 1)
    low_half = lane < ATT_HEAD_DIM
    row = _iota((tq, tk), 0)
    col = _iota((tq, tk), 1)

    def diag_mask(dd):
        c = col + dd * tk
        if mode == "sb":
            return c < row
        if mode == "mla":
            return (c // MLA_CHUNK) <= (row // MLA_CHUNK)
        return c <= row

    def rep(x):
        return jnp.concatenate([x] * (tk // LANES), axis=1) if tk > LANES else x

    if mode == "sb":
        u_mat = (_iota((tk, tk), 0) > _iota((tk, tk), 1)).astype(BF16)

    q = q_ref[...]
    if mode == "mla":
        qs = [q[:, 0:LANES], q[:, LANES:2 * LANES]]
    else:
        zero = jnp.zeros_like(q)
        qs = [jnp.where(low_half, q, zero), jnp.where(low_half, zero, q)]

    acc_ref[...] = jnp.zeros_like(acc_ref)
    if mode == "sb":
        m_ref[...] = jnp.zeros_like(m_ref)
    else:
        m_ref[...] = jnp.full(m_ref.shape, NEG_BIG, F32)
        l_ref[...] = jnp.zeros_like(l_ref)

    if mode == "fox":
        q0 = pl.multiple_of(i * tq, tq)
        fh_q = [fh_ref[hh:hh + 1, pl.ds(q0, LANES)][:, 0:1] for hh in range(2)]

    def tile(j, mask):
        k0 = pl.multiple_of(j * tk, tk)
        k_t = k_ref[pl.ds(k0, tk), :]
        v_t = v_ref[pl.ds(k0, tk), :]
        for hh in range(2):
            if mode == "mla":
                z = _dot_nt(qs[hh], k_t[:, hh * LANES:(hh + 1) * LANES])
            else:
                z = _dot_nt(qs[hh], k_t)
            if mode == "sb":
                sp = _softplus(z)
                l1m = -sp
                if mask is not None:
                    l1m = jnp.where(mask, l1m, 0.0)
                hi = l1m.astype(BF16)
                lo = (l1m - hi.astype(F32)).astype(BF16)
                cs = _dot(hi, u_mat) + _dot(lo, u_mat)
                run = m_ref[hh]
                w = jnp.exp((z - sp) + cs + rep(run))
                if mask is not None:
                    w = jnp.where(mask, w, 0.0)
                acc_ref[hh] = acc_ref[hh] + _dot(w.astype(BF16), v_t)
                m_ref[hh] = run + (cs[:, 0:1] + l1m[:, 0:1])
            else:
                if mode == "fox":
                    z = z + (fh_q[hh] - fh_ref[hh:hh + 1, pl.ds(k0, tk)])
                if mask is not None:
                    z = jnp.where(mask, z, NEG_BIG)
                m_old = m_ref[hh]
                m_new = jnp.maximum(m_old, jnp.max(z, axis=1, keepdims=True))
                p = jnp.exp(z - rep(m_new))
                alpha = jnp.exp(m_old - m_new)
                l_ref[hh] = alpha * l_ref[hh] + jnp.sum(p, axis=1, keepdims=True)
                acc_ref[hh] = alpha * acc_ref[hh] + _dot(p.astype(BF16), v_t)
                m_ref[hh] = m_new

    n_off = i * n_diag
    if mode == "sb":
        for dd in reversed(range(n_diag)):
            tile(n_off + dd, diag_mask(dd))

        def body(it, carry):
            tile(n_off - 1 - it, None)
            return carry
        lax.fori_loop(0, n_off, body, 0)
        out = [acc_ref[0], acc_ref[1]]
    else:
        def body(j, carry):
            tile(j, None)
            return carry
        lax.fori_loop(0, n_off, body, 0)
        for dd in range(n_diag):
            tile(n_off + dd, diag_mask(dd))
        out = [acc_ref[hh] / l_ref[hh] for hh in range(2)]
    o_ref[...] = jnp.where(low_half, out[0], out[1]).astype(o_ref.dtype)


def _attention(mode, q, k, v, batch, seq, fh=None):
    t = v.shape[0]
    tq = min(ATT_TQ[mode], seq)
    tk = min(ATT_TK[mode], tq)
    nq = seq // tq
    qk_w = 2 * LANES if mode == "mla" else LANES
    in_specs = [pl.BlockSpec((tq, qk_w), lambda b, h, i: (b * nq + i, h)),
                pl.BlockSpec((seq, qk_w), lambda b, h, i: (b, h)),
                pl.BlockSpec((seq, LANES), lambda b, h, i: (b, h))]
    args = [q, k, v]
    if mode == "fox":
        in_specs.append(pl.BlockSpec((None, 2, seq), lambda b, h, i: (h, 0, b)))
        args.append(fh.reshape(HEAD_PAIRS, 2, t))
    return pl.pallas_call(
        functools.partial(_attention_kernel, mode=mode, tq=tq, tk=tk),
        grid=(batch, HEAD_PAIRS, nq),
        in_specs=in_specs,
        out_specs=pl.BlockSpec((tq, LANES), lambda b, h, i: (b * nq + i, h)),
        out_shape=jax.ShapeDtypeStruct((t, ATT_WIDTH), BF16),
        scratch_shapes=[pltpu.VMEM((2, tq, LANES), F32),
                        pltpu.VMEM((2, tq, LANES), F32),
                        pltpu.VMEM((2, tq, LANES), F32)],
        compiler_params=_cparams(("parallel", "parallel", "arbitrary")),
        name="attn_" + mode,
    )(*args)


def _mla_prep_kernel(cq_ref, ckv_ref, kpe_ref, kps_ref, cos_ref, sin_ref, qg_ref, kvg_ref,
                     wqm_ref, wqs_ref, wk_ref, wv_ref, q_ref, k_ref, v_ref):
    def rms(x, g):
        return x * lax.rsqrt(jnp.mean(x * x, axis=1, keepdims=True) + RMS_EPS) * g

    cqn = rms(cq_ref[...], qg_ref[...]).astype(BF16)
    ckvn = rms(ckv_ref[...], kvg_ref[...]).astype(BF16)
    cos1 = cos_ref[...]
    sin1 = sin_ref[...]
    cos8 = jnp.concatenate([cos1] * ATT_HEADS, axis=1)
    sin8 = jnp.concatenate([sin1] * ATT_HEADS, axis=1)
    qm = _dot(cqn, wqm_ref[...])
    qsw = _dot(cqn, wqs_ref[...])
    scale = 1.0 / math.sqrt(MLA_NOPE + MLA_ROPE)
    q_ref[...] = ((qm * cos8 + qsw * sin8) * scale).astype(q_ref.dtype)
    k_rope = kpe_ref[...] * cos1 + kps_ref[...] * sin1
    k_ref[...] = (_dot(ckvn, wk_ref[...]) + jnp.concatenate([k_rope] * ATT_HEADS, axis=1)).astype(k_ref.dtype)
    v_ref[...] = _dot(ckvn, wv_ref[...]).astype(v_ref.dtype)


def _mla_prep(cq, ckv, kpe, kps, q_norm_g, w_q_up, kv_norm_g, w_kv_up, seq, tm):
    t = cq.shape[0]
    half = MLA_ROPE // 2
    inv_freq = ROPE_THETA ** (-(jnp.arange(0, MLA_ROPE, 2, dtype=F32) / MLA_ROPE))
    ang = jnp.arange(seq, dtype=F32)[:, None] * inv_freq[None, :]
    cos, sin = jnp.cos(ang), jnp.sin(ang)
    pad = LANES - MLA_NOPE - MLA_ROPE
    cos1 = jnp.concatenate([jnp.ones((seq, MLA_NOPE), F32), cos, cos, jnp.zeros((seq, pad), F32)], axis=1)
    sin1 = jnp.concatenate([jnp.zeros((seq, MLA_NOPE), F32), sin, sin, jnp.zeros((seq, pad), F32)], axis=1)
    wq = w_q_up.reshape(MLA_Q_RANK, ATT_HEADS, MLA_NOPE + MLA_ROPE)
    zq = lambda n: jnp.zeros((MLA_Q_RANK, ATT_HEADS, n), F32)
    wq_main = jnp.concatenate([wq, zq(pad)], axis=2).reshape(MLA_Q_RANK, ATT_HEADS * LANES)
    wq_swap = jnp.concatenate([zq(MLA_NOPE), -wq[:, :, MLA_NOPE + half:], wq[:, :, MLA_NOPE:MLA_NOPE + half],
                               zq(pad)], axis=2).reshape(MLA_Q_RANK, ATT_HEADS * LANES)
    wkv = w_kv_up.reshape(MLA_KV_RANK, ATT_HEADS, MLA_NOPE + MLA_V)
    wk = jnp.concatenate([wkv[:, :, :MLA_NOPE], jnp.zeros((MLA_KV_RANK, ATT_HEADS, LANES - MLA_NOPE), F32)],
                         axis=2).reshape(MLA_KV_RANK, ATT_HEADS * LANES)
    wv = wkv[:, :, MLA_NOPE:].reshape(MLA_KV_RANK, ATT_HEADS * MLA_V)
    consts = [q_norm_g.reshape(1, -1).astype(F32), kv_norm_g.reshape(1, -1).astype(F32),
              wq_main.astype(BF16), wq_swap.astype(BF16), wk.astype(BF16), wv.astype(BF16)]
    ns = seq // tm
    row = lambda n: pl.BlockSpec((tm, n), lambda i: (i, 0))
    tab = pl.BlockSpec((tm, LANES), lambda i: (i % ns, 0))
    full = lambda a: pl.BlockSpec(a.shape, lambda i: (0, 0))
    return pl.pallas_call(
        _mla_prep_kernel,
        grid=(t // tm,),
        in_specs=[row(MLA_Q_RANK), row(MLA_KV_RANK), row(LANES), row(LANES), tab, tab] + [full(a) for a in consts],
        out_specs=[row(ATT_HEADS * LANES), row(ATT_HEADS * LANES), row(ATT_HEADS * MLA_V)],
        out_shape=[jax.ShapeDtypeStruct((t, ATT_HEADS * LANES), BF16),
                   jax.ShapeDtypeStruct((t, ATT_HEADS * LANES), BF16),
                   jax.ShapeDtypeStruct((t, ATT_HEADS * MLA_V), BF16)],
        compiler_params=_cparams(("parallel",)),
        name="mla_prep",
    )(cq, ckv, kpe, kps, cos1, sin1, *consts)


def _layer_norm_rows(r, g, b):
    mu = jnp.mean(r, axis=1, keepdims=True)
    d = r - mu
    var = jnp.mean(d * d, axis=1, keepdims=True)
    return d * lax.rsqrt(var + LN_EPS) * g + b


def _outproj_ln_kernel(a_ref, b_ref, x_ref, wa_ref, wb_ref, g_ref, beta_ref, o_ref):
    y = _dot(a_ref[...], wa_ref[...]) + _dot(b_ref[...], wb_ref[...])
    r = DEEPNORM_ALPHA * x_ref[...] + y
    o_ref[...] = _layer_norm_rows(r, g_ref[...], beta_ref[...])


def _outproj_ln(a, b, x2, w_out, ln_g, ln_b, tm):
    t, d = x2.shape
    ka, kb = a.shape[1], b.shape[1]
    consts = [w_out[:ka].astype(BF16), w_out[ka:].astype(BF16),
              ln_g.reshape(1, -1).astype(F32), ln_b.reshape(1, -1).astype(F32)]
    row = lambda n: pl.BlockSpec((tm, n), lambda i: (i, 0))
    full = lambda c: pl.BlockSpec(c.shape, lambda i: (0, 0))
    return pl.pallas_call(
        _outproj_ln_kernel,
        grid=(t // tm,),
        in_specs=[row(ka), row(kb), row(d)] + [full(c) for c in consts],
        out_specs=row(d),
        out_shape=jax.ShapeDtypeStruct((t, d), F32),
        compiler_params=_cparams(("parallel",)),
        name="outproj_ln",
    )(a, b, x2, *consts)


def _router_gate(x, wg, bg, we, be):
    hp = lax.Precision.HIGHEST
    tm = x.shape[0]
    gl = jnp.dot(x, wg, precision=hp, preferred_element_type=F32) + bg
    el = jnp.dot(x, we, precision=hp, preferred_element_type=F32) + be
    g_max = jnp.max(gl, axis=1, keepdims=True)
    g_p = 1.0 / jnp.sum(jnp.exp(gl - g_max), axis=1, keepdims=True)
    g_idx = _iota((tm, MOE_GROUPS), 1)
    g_sel = jnp.min(jnp.where(gl == g_max, g_idx, MOE_GROUPS), axis=1, keepdims=True)
    e_idx = _iota((tm, MOE_EXPERTS), 1)
    in_group = (e_idx // MOE_EPG) == g_sel
    masked = jnp.where(in_group, el, -jnp.inf)
    m1 = jnp.max(masked, axis=1, keepdims=True)
    i1 = jnp.min(jnp.where(masked == m1, e_idx, MOE_EXPERTS), axis=1, keepdims=True)
    masked2 = jnp.where(e_idx == i1, -jnp.inf, masked)
    m2 = jnp.max(masked2, axis=1, keepdims=True)
    i2 = jnp.min(jnp.where(masked2 == m2, e_idx, MOE_EXPERTS), axis=1, keepdims=True)
    e2 = jnp.exp(m2 - m1)
    w1 = g_p / (1.0 + e2)
    w2 = w1 * e2
    return jnp.where(e_idx == i1, w1, 0.0) + jnp.where(e_idx == i2, w2, 0.0)


def _moe_dense_kernel(x_ref, wgrp_ref, bgrp_ref, wexp_ref, bexp_ref, wg_ref, wu_ref, wd_ref, g_ref, beta_ref,
                      o_ref, gate_ref, acc_ref, xb_ref):
    e = pl.program_id(1)

    @pl.when(e == 0)
    def _():
        x = x_ref[...]
        gate_ref[...] = _router_gate(x, wgrp_ref[...], bgrp_ref[...], wexp_ref[...], bexp_ref[...])
        xb_ref[...] = x.astype(BF16)
        acc_ref[...] = jnp.zeros_like(acc_ref)

    xb = xb_ref[...]
    gate = gate_ref[...]
    ge = jnp.sum(jnp.where(_iota(gate.shape, 1) == e, gate, 0.0), axis=1, keepdims=True)
    hid = _silu(_dot(xb, wg_ref[...])) * _dot(xb, wu_ref[...])
    acc_ref[...] += _dot((hid * ge).astype(BF16), wd_ref[...])

    @pl.when(e == MOE_EXPERTS - 1)
    def _():
        r = DEEPNORM_ALPHA * x_ref[...] + acc_ref[...]
        o_ref[...] = _layer_norm_rows(r, g_ref[...], beta_ref[...])


def _moe_dense(x2, w_group, b_group, w_expert, b_expert, w_gate, w_up, w_down, ln_g, ln_b, tm):
    t, d = x2.shape
    small = [w_group.astype(F32), b_group.reshape(1, -1).astype(F32),
             w_expert.astype(F32), b_expert.reshape(1, -1).astype(F32)]
    ln = [ln_g.reshape(1, -1).astype(F32), ln_b.reshape(1, -1).astype(F32)]
    row = pl.BlockSpec((tm, d), lambda i, e: (i, 0))
    full = lambda c: pl.BlockSpec(c.shape, lambda i, e: (0, 0))
    return pl.pallas_call(
        _moe_dense_kernel,
        grid=(t // tm, MOE_EXPERTS),
        in_specs=[row] + [full(c) for c in small]
        + [pl.BlockSpec((None, d, MOE_FF), lambda i, e: (e, 0, 0)),
           pl.BlockSpec((None, d, MOE_FF), lambda i, e: (e, 0, 0)),
           pl.BlockSpec((None, MOE_FF, d), lambda i, e: (e, 0, 0))]
        + [full(c) for c in ln],
        out_specs=row,
        out_shape=jax.ShapeDtypeStruct((t, d), F32),
        scratch_shapes=[pltpu.VMEM((tm, MOE_EXPERTS), F32),
                        pltpu.VMEM((tm, d), F32),
                        pltpu.VMEM((tm, d), BF16)],
        compiler_params=_cparams(("parallel", "arbitrary")),
        name="moe_dense",
    )(x2, *small, w_gate.astype(BF16), w_up.astype(BF16), w_down.astype(BF16), *ln)


def _row_tile(n, pref):
    tm = min(pref, n)
    assert n % tm == 0
    return tm


def kernel(x, ev_w_in, ev_conv_w, ev_conv_b, ev_dt_bias, ev_a_log, ev_d_skip, ev_norm_g, ev_w_out, od_w_in, od_q_norm_g, od_w_q_up, od_kv_norm_g, od_w_kv_up, od_f_bias, od_w_out, ln1_g, ln1_b, ln2_g, ln2_b, moe_w_group, moe_b_group, moe_w_expert, moe_b_expert, moe_w_gate, moe_w_up, moe_w_down):
    batch, seq, d = x.shape
    t = batch * seq
    assert seq % SSD_CHUNK == 0 and seq % LANES == 0
    tm_proj = _row_tile(seq, 512)
    tm_moe = _row_tile(t, 1024)
    x2 = x.reshape(t, d)

    z, xbc, dt, dtt, q, k, v = _proj_even(x2, ev_w_in[0], tm_proj)
    y_ssd = _ssd(xbc, z, dt, dtt, ev_conv_w[0], ev_conv_b[0], ev_dt_bias[0], ev_a_log[0], ev_d_skip[0],
                 ev_norm_g[0], batch, seq)
    y_sb = _attention("sb", q, k, v, batch, seq)
    x2 = _outproj_ln(y_ssd, y_sb, x2, ev_w_out[0], ln1_g[0], ln1_b[0], tm_proj)
    x2 = _moe_dense(x2, moe_w_group[0], moe_b_group[0], moe_w_expert[0], moe_b_expert[0],
                    moe_w_gate[0], moe_w_up[0], moe_w_down[0], ln2_g[0], ln2_b[0], tm_moe)

    cq, ckv, kpe, kps, q, k, v, fh = _proj_odd(x2, od_w_in[0], od_f_bias[0], seq, tm_proj)
    qm, km, vm = _mla_prep(cq, ckv, kpe, kps, od_q_norm_g[0], od_w_q_up[0], od_kv_norm_g[0], od_w_kv_up[0],
                           seq, tm_proj)
    y_mla = _attention("mla", qm, km, vm, batch, seq)
    y_fox = _attention("fox", q, k, v, batch, seq, fh=fh)
    x2 = _outproj_ln(y_mla, y_fox, x2, od_w_out[0], ln1_g[1], ln1_b[1], tm_proj)
    x2 = _moe_dense(x2, moe_w_group[1], moe_b_group[1], moe_w_expert[1], moe_b_expert[1],
                    moe_w_gate[1], moe_w_up[1], moe_w_down[1], ln2_g[1], ln2_b[1], tm_moe)
    return x2.reshape(batch, seq, d)
```

```python
import functools
import math

import jax
import jax.numpy as jnp
from jax import lax
from jax.experimental import pallas as pl
from jax.experimental.pallas import tpu as pltpu

F32 = jnp.float32
BF16 = jnp.bfloat16

SSD_HEADS = 16
SSD_HEAD_DIM = 64
SSD_INNER = SSD_HEADS * SSD_HEAD_DIM
SSD_GROUPS = 2
SSD_STATE = 128
SSD_CONV = 4
SSD_BC = SSD_GROUPS * SSD_STATE
SSD_CONV_DIM = SSD_INNER + 2 * SSD_BC
ATT_HEADS = 8
ATT_HEAD_DIM = 64
ATT_WIDTH = ATT_HEADS * ATT_HEAD_DIM
HEAD_PAIRS = ATT_HEADS // 2
MLA_Q_RANK = 256
MLA_KV_RANK = 128
MLA_NOPE = 64
MLA_ROPE = 32
MLA_V = 64
MLA_CHUNK = 64
ROPE_THETA = 10000.0
MOE_GROUPS = 4
MOE_EPG = 4
MOE_EXPERTS = MOE_GROUPS * MOE_EPG
MOE_FF = 256
DEPTH = 2
DEEPNORM_ALPHA = (2.0 * DEPTH) ** 0.25
LN_EPS = 1e-5
RMS_EPS = 1e-6

LANES = 128
SSD_CHUNK = 128
ATT_TQ = {"sb": 256, "mla": 512, "fox": 256}
ATT_TK = {"sb": 256, "mla": 512, "fox": 256}
NEG_BIG = -1e30
EXP_ZERO = 110.0
MOE_TILE = 256
MOE_PAIRS = MOE_EPG * (MOE_EPG - 1) // 2
MOE_BUCKETS = MOE_GROUPS * MOE_PAIRS
VMEM_LIMIT = 56 * 1024 * 1024


def _cparams(sem):
    return pltpu.CompilerParams(dimension_semantics=sem, vmem_limit_bytes=VMEM_LIMIT)


def _dot(a, b):
    return jnp.dot(a, b, preferred_element_type=F32)


def _dot_nt(a, b):
    return lax.dot_general(a, b, (((1,), (1,)), ((), ())), preferred_element_type=F32)


def _split3(x):
    hi = x.astype(BF16)
    r1 = x - hi.astype(F32)
    mid = r1.astype(BF16)
    lo = (r1 - mid.astype(F32)).astype(BF16)
    return hi, mid, lo


def _dot01_right(x, m01):
    hi, mid, lo = _split3(x)
    return _dot(hi, m01) + _dot(mid, m01) + _dot(lo, m01)


def _dot01_left(m01, x):
    hi, mid, lo = _split3(x)
    return _dot(m01, hi) + _dot(m01, mid) + _dot(m01, lo)


def _softplus(x):
    return jnp.maximum(x, 0.0) + jnp.log(1.0 + jnp.exp(-jnp.abs(x)))


def _silu(x):
    return x * (1.0 / (1.0 + jnp.exp(-x)))


def _iota(shape, dim):
    return lax.broadcasted_iota(jnp.int32, shape, dim)


def _proj_even_kernel(x_ref, wz_ref, wx_ref, wdt_ref, wdtt_ref, wq_ref, wk_ref, wv_ref,
                      z_ref, xbc_ref, dt_ref, dtt_ref, q_ref, k_ref, v_ref):
    x = x_ref[...].astype(BF16)
    z_ref[...] = _dot(x, wz_ref[...]).astype(z_ref.dtype)
    xbc_ref[...] = _dot(x, wx_ref[...]).astype(xbc_ref.dtype)
    dt_ref[...] = _dot(x, wdt_ref[...])
    dtt_ref[...] = _dot_nt(wdtt_ref[...], x)
    q_ref[...] = _dot(x, wq_ref[...]).astype(q_ref.dtype)
    k_ref[...] = _dot(x, wk_ref[...]).astype(k_ref.dtype)
    v_ref[...] = _dot(x, wv_ref[...]).astype(v_ref.dtype)


def _proj_even(x2, w_in, tm):
    t, d = x2.shape
    o = 0
    parts = []
    for n in (SSD_INNER, SSD_CONV_DIM, SSD_HEADS, ATT_WIDTH, ATT_WIDTH, ATT_WIDTH):
        parts.append(w_in[:, o:o + n])
        o += n
    wz, wx, wdt, wq, wk, wv = parts
    wq = wq * (1.0 / math.sqrt(ATT_HEAD_DIM))
    ws = [wz.astype(BF16), wx.astype(BF16), wdt.astype(BF16), wdt.T.astype(BF16),
          wq.astype(BF16), wk.astype(BF16), wv.astype(BF16)]
    row = lambda n: pl.BlockSpec((tm, n), lambda i: (i, 0))
    full = lambda a: pl.BlockSpec(a.shape, lambda i: (0, 0))
    return pl.pallas_call(
        _proj_even_kernel,
        grid=(t // tm,),
        in_specs=[row(d)] + [full(w) for w in ws],
        out_specs=[row(SSD_INNER), row(SSD_CONV_DIM), row(SSD_HEADS),
                   pl.BlockSpec((SSD_HEADS, tm), lambda i: (0, i)),
                   row(ATT_WIDTH), row(ATT_WIDTH), row(ATT_WIDTH)],
        out_shape=[jax.ShapeDtypeStruct((t, SSD_INNER), BF16),
                   jax.ShapeDtypeStruct((t, SSD_CONV_DIM), BF16),
                   jax.ShapeDtypeStruct((t, SSD_HEADS), F32),
                   jax.ShapeDtypeStruct((SSD_HEADS, t), F32),
                   jax.ShapeDtypeStruct((t, ATT_WIDTH), BF16),
                   jax.ShapeDtypeStruct((t, ATT_WIDTH), BF16),
                   jax.ShapeDtypeStruct((t, ATT_WIDTH), BF16)],
        compiler_params=_cparams(("parallel",)),
        name="proj_even",
    )(x2, *ws)


def _proj_odd_kernel(x_ref, wcq_ref, wckv_ref, wkpe_ref, wkps_ref, wq_ref, wk_ref, wv_ref, wft_ref, fb_ref,
                     cq_ref, ckv_ref, kpe_ref, kps_ref, q_ref, k_ref, v_ref, fh_ref, carry_ref, *, tiles_per_seq):
    i = pl.program_id(0)
    tm = x_ref.shape[0]
    x = x_ref[...].astype(BF16)
    cq_ref[...] = _dot(x, wcq_ref[...])
    ckv_ref[...] = _dot(x, wckv_ref[...])
    kpe_ref[...] = _dot(x, wkpe_ref[...])
    kps_ref[...] = _dot(x, wkps_ref[...])
    q_ref[...] = _dot(x, wq_ref[...]).astype(q_ref.dtype)
    k_ref[...] = _dot(x, wk_ref[...]).astype(k_ref.dtype)
    v_ref[...] = _dot(x, wv_ref[...]).astype(v_ref.dtype)

    @pl.when(i % tiles_per_seq == 0)
    def _():
        carry_ref[...] = jnp.zeros_like(carry_ref)

    f_raw = _dot_nt(wft_ref[...], x) + fb_ref[...]
    log_f = -_softplus(-f_raw)
    upper = (_iota((tm, tm), 0) <= _iota((tm, tm), 1)).astype(BF16)
    cum = _dot01_right(log_f, upper) + carry_ref[...]
    fh_ref[...] = cum
    carry_ref[...] = cum[:, tm - 1:tm]


def _proj_odd(x2, w_in, f_bias, seq, tm):
    t, d = x2.shape
    sizes = (MLA_Q_RANK, MLA_KV_RANK, MLA_ROPE, ATT_WIDTH, ATT_WIDTH, ATT_WIDTH, ATT_HEADS)
    o = 0
    parts = []
    for n in sizes:
        parts.append(w_in[:, o:o + n])
        o += n
    wcq, wckv, wkpe, wq, wk, wv, wf = parts
    half = MLA_ROPE // 2
    pad_l = jnp.zeros((d, MLA_NOPE), F32)
    pad_r = jnp.zeros((d, LANES - MLA_NOPE - MLA_ROPE), F32)
    wkpe_p = jnp.concatenate([pad_l, wkpe, pad_r], axis=1)
    wkps_p = jnp.concatenate([pad_l, -wkpe[:, half:], wkpe[:, :half], pad_r], axis=1)
    wq = wq * (1.0 / math.sqrt(ATT_HEAD_DIM))
    ws = [wcq.astype(BF16), wckv.astype(BF16), wkpe_p.astype(BF16), wkps_p.astype(BF16),
          wq.astype(BF16), wk.astype(BF16), wv.astype(BF16), wf.T.astype(BF16)]
    fb = f_bias.reshape(ATT_HEADS, 1).astype(F32)
    row = lambda n: pl.BlockSpec((tm, n), lambda i: (i, 0))
    full = lambda a: pl.BlockSpec(a.shape, lambda i: (0, 0))
    return pl.pallas_call(
        functools.partial(_proj_odd_kernel, tiles_per_seq=seq // tm),
        grid=(t // tm,),
        in_specs=[row(d)] + [full(w) for w in ws] + [full(fb)],
        out_specs=[row(MLA_Q_RANK), row(MLA_KV_RANK), row(LANES), row(LANES),
                   row(ATT_WIDTH), row(ATT_WIDTH), row(ATT_WIDTH),
                   pl.BlockSpec((ATT_HEADS, tm), lambda i: (0, i))],
        out_shape=[jax.ShapeDtypeStruct((t, MLA_Q_RANK), F32),
                   jax.ShapeDtypeStruct((t, MLA_KV_RANK), F32),
                   jax.ShapeDtypeStruct((t, LANES), F32),
                   jax.ShapeDtypeStruct((t, LANES), F32),
                   jax.ShapeDtypeStruct((t, ATT_WIDTH), BF16),
                   jax.ShapeDtypeStruct((t, ATT_WIDTH), BF16),
                   jax.ShapeDtypeStruct((t, ATT_WIDTH), BF16),
                   jax.ShapeDtypeStruct((ATT_HEADS, t), F32)],
        scratch_shapes=[pltpu.VMEM((ATT_HEADS, 1), F32)],
        compiler_params=_cparams(("arbitrary",)),
        name="proj_odd",
    )(x2, *ws, fb)


def _ssd_kernel(xbc_ref, z_ref, dt_ref, dtt_ref, cw_ref, cb_ref, dtb_ref, dtbt_ref, alog_ref, alogt_ref,
                dskip_ref, ng_ref, o_ref, buf_ref, st_ref, y_ref):
    c = pl.program_id(1)
    L = SSD_CHUNK
    PAD = 8

    @pl.when(c == 0)
    def _():
        buf_ref[0:PAD, :] = jnp.zeros((PAD, SSD_CONV_DIM), F32)
        st_ref[...] = jnp.zeros_like(st_ref)

    buf_ref[PAD:PAD + L, :] = xbc_ref[...].astype(F32)
    acc = jnp.zeros((L, SSD_CONV_DIM), F32) + cb_ref[...]
    for kk in range(SSD_CONV):
        off = PAD - (SSD_CONV - 1) + kk
        acc = acc + buf_ref[off:off + L, :] * cw_ref[kk:kk + 1, :]
    buf_ref[0:PAD, :] = buf_ref[L:L + PAD, :]
    xbc = _silu(acc)
    b_mat = xbc[:, SSD_INNER:SSD_INNER + SSD_BC]
    c_mat = xbc[:, SSD_INNER + SSD_BC:]
    bt = jnp.transpose(b_mat).astype(BF16)
    c_bf = c_mat.astype(BF16)

    dt = _softplus(dt_ref[...] + dtb_ref[...])
    dtt = _softplus(dtt_ref[...] + dtbt_ref[...])
    da = dt * (-jnp.exp(alog_ref[...]))
    dat = dtt * (-jnp.exp(alogt_ref[...]))
    row = _iota((L, L), 0)
    col = _iota((L, L), 1)
    causal = col <= row
    a_cum = _dot01_left(causal.astype(BF16), da)
    a_cumt = _dot01_right(dat, (row <= col).astype(BF16))
    a_end = a_cum[L - 1:L, :]
    low_half = _iota((1, LANES), 1) < SSD_HEAD_DIM

    def pair_cols(m, h0):
        n = m.shape[0]
        return jnp.where(low_half, jnp.broadcast_to(m[:, h0:h0 + 1], (n, LANES)),
                         jnp.broadcast_to(m[:, h0 + 1:h0 + 2], (n, LANES)))

    sumsq = jnp.zeros((L, 1), F32)
    for g in range(SSD_GROUPS):
        bt_g = bt[g * SSD_STATE:(g + 1) * SSD_STATE, :]
        c_g = c_bf[:, g * SSD_STATE:(g + 1) * SSD_STATE]
        cb = _dot(c_g, bt_g)
        for pp in range(SSD_HEADS // SSD_GROUPS // 2):
            p = g * (SSD_HEADS // SSD_GROUPS // 2) + pp
            h0 = 2 * p
            lanes = slice(p * LANES, (p + 1) * LANES)
            xs = xbc[:, lanes]
            xs_bf = xs.astype(BF16)
            yd = []
            for h in (h0, h0 + 1):
                seg = a_cum[:, h:h + 1] - a_cumt[h:h + 1, :]
                dec = jnp.where(causal, jnp.exp(jnp.minimum(seg, 0.0)), 0.0)
                m = cb * dec * dtt[h:h + 1, :]
                yd.append(_dot(m.astype(BF16), xs_bf))
            y = jnp.where(low_half, yd[0], yd[1])
            a_p = pair_cols(a_cum, h0)
            dt_p = pair_cols(dt, h0)
            end_p = pair_cols(a_end, h0)
            st = st_ref[p]
            y = y + _dot(c_g, st.astype(BF16)) * jnp.exp(a_p)
            xw = xs * (jnp.exp(end_p - a_p) * dt_p)
            st_ref[p] = st * jnp.exp(end_p) + _dot(bt_g, xw.astype(BF16))
            y = y + dskip_ref[:, lanes] * xs
            y = y * _silu(z_ref[:, lanes].astype(F32))
            y_ref[:, lanes] = y
            sumsq = sumsq + jnp.sum(y * y, axis=1, keepdims=True)
    inv = lax.rsqrt(sumsq * (1.0 / SSD_INNER) + RMS_EPS)
    o_ref[...] = (y_ref[...] * inv * ng_ref[...]).astype(o_ref.dtype)


def _ssd(xbc, z, dt, dtt, conv_w, conv_b, dt_bias, a_log, d_skip, norm_g, batch, seq):
    t = xbc.shape[0]
    L = SSD_CHUNK
    nc = seq // L
    row = lambda n: pl.BlockSpec((L, n), lambda b, c: (b * nc + c, 0))
    full = lambda a: pl.BlockSpec(a.shape, lambda b, c: (0,) * a.ndim)
    params = [conv_w.astype(F32), conv_b.reshape(1, -1).astype(F32),
              dt_bias.reshape(1, -1).astype(F32), dt_bias.reshape(-1, 1).astype(F32),
              a_log.reshape(1, -1).astype(F32), a_log.reshape(-1, 1).astype(F32),
              jnp.repeat(d_skip.astype(F32), SSD_HEAD_DIM).reshape(1, -1),
              norm_g.reshape(1, -1).astype(F32)]
    return pl.pallas_call(
        _ssd_kernel,
        grid=(batch, nc),
        in_specs=[row(SSD_CONV_DIM), row(SSD_INNER), row(SSD_HEADS),
                  pl.BlockSpec((SSD_HEADS, L), lambda b, c: (0, b * nc + c))] + [full(a) for a in params],
        out_specs=row(SSD_INNER),
        out_shape=jax.ShapeDtypeStruct((t, SSD_INNER), BF16),
        scratch_shapes=[pltpu.VMEM((L + 8, SSD_CONV_DIM), F32),
                        pltpu.VMEM((SSD_HEADS // 2, SSD_STATE, LANES), F32),
                        pltpu.VMEM((L, SSD_INNER), F32)],
        compiler_params=_cparams(("parallel", "arbitrary")),
        name="ssd",
    )(xbc, z, dt, dtt, *params)


def _attention_kernel(*refs, mode, tq, tk):
    if mode == "fox":
        q_ref, k_ref, v_ref, fh_ref, o_ref, acc_ref, m_ref, l_ref, kn_ref = refs
    else:
        q_ref, k_ref, v_ref, o_ref, acc_ref, m_ref, l_ref = refs
        fh_ref = None
    i = pl.program_id(2)
    n_diag = tq // tk
    lane = _iota((1, LANES), 1)
    low_half = lane < ATT_HEAD_DIM
    row = _iota((tq, tk), 0)
    col = _iota((tq, tk), 1)

    def diag_mask(dd):
        c = col + dd * tk
        if mode == "sb":
            return c < row
        if mode == "mla":
            return (c // MLA_CHUNK) <= (row // MLA_CHUNK)
        return c <= row

    def rep(x):
        return jnp.concatenate([x] * (tk // LANES), axis=1) if tk > LANES else x

    if mode == "sb":
        u_mat = (_iota((tk, tk), 0) > _iota((tk, tk), 1)).astype(BF16)

    q = q_ref[...]
    if mode == "mla":
        qs = [q[:, 0:LANES], q[:, LANES:2 * LANES]]
    else:
        zero = jnp.zeros_like(q)
        qs = [jnp.where(low_half, q, zero), jnp.where(low_half, zero, q)]

    acc_ref[...] = jnp.zeros_like(acc_ref)
    if mode == "sb":
        m_ref[...] = jnp.zeros_like(m_ref)
    else:
        m_ref[...] = jnp.full(m_ref.shape, NEG_BIG, F32)
        l_ref[...] = jnp.zeros_like(l_ref)

    if mode == "fox":
        q0 = pl.multiple_of(i * tq, tq)
        fh_q = [fh_ref[hh:hh + 1, pl.ds(q0, LANES)][:, 0:1] for hh in range(2)]

    def tile(j, mask):
        k0 = pl.multiple_of(j * tk, tk)
        k_t = k_ref[pl.ds(k0, tk), :]
        v_t = v_ref[pl.ds(k0, tk), :]
        for hh in range(2):
            if mode == "mla":
                z = _dot_nt(qs[hh], k_t[:, hh * LANES:(hh + 1) * LANES])
            else:
                z = _dot_nt(qs[hh], k_t)
            if mode == "sb":
                sp = _softplus(z)
                l1m = -sp
                if mask is not None:
                    l1m = jnp.where(mask, l1m, 0.0)
                hi = l1m.astype(BF16)
                lo = (l1m - hi.astype(F32)).astype(BF16)
                cs = _dot(hi, u_mat) + _dot(lo, u_mat)
                run = m_ref[hh]
                w = jnp.exp((z - sp) + cs + rep(run))
                if mask is not None:
                    w = jnp.where(mask, w, 0.0)
                acc_ref[hh] = acc_ref[hh] + _dot(w.astype(BF16), v_t)
                m_ref[hh] = run + (cs[:, 0:1] + l1m[:, 0:1])
            else:
                if mode == "fox":
                    z = z + (fh_q[hh] - fh_ref[hh:hh + 1, pl.ds(k0, tk)])
                if mask is not None:
                    z = jnp.where(mask, z, NEG_BIG)
                m_old = m_ref[hh]
                m_new = jnp.maximum(m_old, jnp.max(z, axis=1, keepdims=True))
                p = jnp.exp(z - rep(m_new))
                alpha = jnp.exp(m_old - m_new)
                l_ref[hh] = alpha * l_ref[hh] + jnp.sum(p, axis=1, keepdims=True)
                acc_ref[hh] = alpha * acc_ref[hh] + _dot(p.astype(BF16), v_t)
                m_ref[hh] = m_new

    def more(j_done):
        if mode == "sb":
            return (jnp.max(m_ref[...]) > -EXP_ZERO).astype(jnp.int32)
        if mode == "fox":
            kl = pl.multiple_of(jnp.maximum(j_done - 1, 0) * tk, tk)
            best = None
            for hh in range(2):
                fh_last = fh_ref[hh:hh + 1, pl.ds(kl, tk)][:, tk - 1:tk]
                v = jnp.max(qk_bound - m_ref[hh], axis=(0, 1), keepdims=True) + (fh_q[hh] - fh_last)
                best = v if best is None else jnp.maximum(best, v)
            return (jnp.max(best) > -EXP_ZERO).astype(jnp.int32)
        return jnp.int32(1)

    if mode == "fox":
        @pl.when(i == 0)
        def _():
            kf = k_ref[...].astype(F32)
            n2 = jnp.max(jnp.sum(kf * kf, axis=1, keepdims=True), axis=0, keepdims=True)
            kn_ref[...] = jnp.broadcast_to(jnp.sqrt(n2), kn_ref.shape)
        qf = q.astype(F32)
        qk_bound = jnp.sqrt(jnp.sum(qf * qf, axis=1, keepdims=True)) * kn_ref[0:1, :]

    n_off = i * n_diag
    for dd in reversed(range(n_diag)):
        tile(n_off + dd, diag_mask(dd))

    def cond(c):
        return jnp.logical_and(c[0] < n_off, c[1] > 0)

    def body(c):
        j = n_off - 1 - c[0]
        tile(j, None)
        return c[0] + 1, more(j)
    lax.while_loop(cond, body, (jnp.int32(0), more(n_off)))
    if mode == "sb":
        out = [acc_ref[0], acc_ref[1]]
    else:
        out = [acc_ref[hh] / l_ref[hh] for hh in range(2)]
    o_ref[...] = jnp.where(low_half, out[0], out[1]).astype(o_ref.dtype)


def _attention(mode, q, k, v, batch, seq, fh=None):
    t = v.shape[0]
    tq = min(ATT_TQ[mode], seq)
    tk = min(ATT_TK[mode], tq)
    nq = seq // tq
    qk_w = 2 * LANES if mode == "mla" else LANES
    in_specs = [pl.BlockSpec((tq, qk_w), lambda b, h, i: (b * nq + i, h)),
                pl.BlockSpec((seq, qk_w), lambda b, h, i: (b, h)),
                pl.BlockSpec((seq, LANES), lambda b, h, i: (b, h))]
    args = [q, k, v]
    scratch = [pltpu.VMEM((2, tq, LANES), F32), pltpu.VMEM((2, tq, LANES), F32), pltpu.VMEM((2, tq, LANES), F32)]
    if mode == "fox":
        in_specs.append(pl.BlockSpec((None, 2, seq), lambda b, h, i: (h, 0, b)))
        args.append(fh.reshape(HEAD_PAIRS, 2, t))
        scratch.append(pltpu.VMEM((8, LANES), F32))
    return pl.pallas_call(
        functools.partial(_attention_kernel, mode=mode, tq=tq, tk=tk),
        grid=(batch, HEAD_PAIRS, nq),
        in_specs=in_specs,
        out_specs=pl.BlockSpec((tq, LANES), lambda b, h, i: (b * nq + i, h)),
        out_shape=jax.ShapeDtypeStruct((t, ATT_WIDTH), BF16),
        scratch_shapes=scratch,
        compiler_params=_cparams(("parallel", "parallel", "arbitrary")),
        name="attn_" + mode,
    )(*args)


def _mla_prep_kernel(cq_ref, ckv_ref, kpe_ref, kps_ref, cos_ref, sin_ref, qg_ref, kvg_ref,
                     wqm_ref, wqs_ref, wk_ref, wv_ref, q_ref, k_ref, v_ref):
    def rms(x, g):
        return x * lax.rsqrt(jnp.mean(x * x, axis=1, keepdims=True) + RMS_EPS) * g

    cqn = rms(cq_ref[...], qg_ref[...]).astype(BF16)
    ckvn = rms(ckv_ref[...], kvg_ref[...]).astype(BF16)
    cos1 = cos_ref[...]
    sin1 = sin_ref[...]
    cos8 = jnp.concatenate([cos1] * ATT_HEADS, axis=1)
    sin8 = jnp.concatenate([sin1] * ATT_HEADS, axis=1)
    qm = _dot(cqn, wqm_ref[...])
    qsw = _dot(cqn, wqs_ref[...])
    scale = 1.0 / math.sqrt(MLA_NOPE + MLA_ROPE)
    q_ref[...] = ((qm * cos8 + qsw * sin8) * scale).astype(q_ref.dtype)
    k_rope = kpe_ref[...] * cos1 + kps_ref[...] * sin1
    k_ref[...] = (_dot(ckvn, wk_ref[...]) + jnp.concatenate([k_rope] * ATT_HEADS, axis=1)).astype(k_ref.dtype)
    v_ref[...] = _dot(ckvn, wv_ref[...]).astype(v_ref.dtype)


def _mla_prep(cq, ckv, kpe, kps, q_norm_g, w_q_up, kv_norm_g, w_kv_up, seq, tm):
    t = cq.shape[0]
    half = MLA_ROPE // 2
    inv_freq = ROPE_THETA ** (-(jnp.arange(0, MLA_ROPE, 2, dtype=F32) / MLA_ROPE))
    ang = jnp.arange(seq, dtype=F32)[:, None] * inv_freq[None, :]
    cos, sin = jnp.cos(ang), jnp.sin(ang)
    pad = LANES - MLA_NOPE - MLA_ROPE
    cos1 = jnp.concatenate([jnp.ones((seq, MLA_NOPE), F32), cos, cos, jnp.zeros((seq, pad), F32)], axis=1)
    sin1 = jnp.concatenate([jnp.zeros((seq, MLA_NOPE), F32), sin, sin, jnp.zeros((seq, pad), F32)], axis=1)
    wq = w_q_up.reshape(MLA_Q_RANK, ATT_HEADS, MLA_NOPE + MLA_ROPE)
    zq = lambda n: jnp.zeros((MLA_Q_RANK, ATT_HEADS, n), F32)
    wq_main = jnp.concatenate([wq, zq(pad)], axis=2).reshape(MLA_Q_RANK, ATT_HEADS * LANES)
    wq_swap = jnp.concatenate([zq(MLA_NOPE), -wq[:, :, MLA_NOPE + half:], wq[:, :, MLA_NOPE:MLA_NOPE + half],
                               zq(pad)], axis=2).reshape(MLA_Q_RANK, ATT_HEADS * LANES)
    wkv = w_kv_up.reshape(MLA_KV_RANK, ATT_HEADS, MLA_NOPE + MLA_V)
    wk = jnp.concatenate([wkv[:, :, :MLA_NOPE], jnp.zeros((MLA_KV_RANK, ATT_HEADS, LANES - MLA_NOPE), F32)],
                         axis=2).reshape(MLA_KV_RANK, ATT_HEADS * LANES)
    wv = wkv[:, :, MLA_NOPE:].reshape(MLA_KV_RANK, ATT_HEADS * MLA_V)
    consts = [q_norm_g.reshape(1, -1).astype(F32), kv_norm_g.reshape(1, -1).astype(F32),
              wq_main.astype(BF16), wq_swap.astype(BF16), wk.astype(BF16), wv.astype(BF16)]
    ns = seq // tm
    row = lambda n: pl.BlockSpec((tm, n), lambda i: (i, 0))
    tab = pl.BlockSpec((tm, LANES), lambda i: (i % ns, 0))
    full = lambda a: pl.BlockSpec(a.shape, lambda i: (0, 0))
    return pl.pallas_call(
        _mla_prep_kernel,
        grid=(t // tm,),
        in_specs=[row(MLA_Q_RANK), row(MLA_KV_RANK), row(LANES), row(LANES), tab, tab] + [full(a) for a in consts],
        out_specs=[row(ATT_HEADS * LANES), row(ATT_HEADS * LANES), row(ATT_HEADS * MLA_V)],
        out_shape=[jax.ShapeDtypeStruct((t, ATT_HEADS * LANES), BF16),
                   jax.ShapeDtypeStruct((t, ATT_HEADS * LANES), BF16),
                   jax.ShapeDtypeStruct((t, ATT_HEADS * MLA_V), BF16)],
        compiler_params=_cparams(("parallel",)),
        name="mla_prep",
    )(cq, ckv, kpe, kps, cos1, sin1, *consts)


def _layer_norm_rows(r, g, b):
    mu = jnp.mean(r, axis=1, keepdims=True)
    d = r - mu
    var = jnp.mean(d * d, axis=1, keepdims=True)
    return d * lax.rsqrt(var + LN_EPS) * g + b


def _router(x, wg, bg, we, be):
    hp = lax.Precision.HIGHEST
    tm = x.shape[0]
    gl = jnp.dot(x, wg, precision=hp, preferred_element_type=F32) + bg
    el = jnp.dot(x, we, precision=hp, preferred_element_type=F32) + be
    g_max = jnp.max(gl, axis=1, keepdims=True)
    g_p = 1.0 / jnp.sum(jnp.exp(gl - g_max), axis=1, keepdims=True)
    g_idx = _iota((tm, MOE_GROUPS), 1)
    g_sel = jnp.min(jnp.where(gl == g_max, g_idx, MOE_GROUPS), axis=1, keepdims=True)
    e_idx = _iota((tm, LANES), 1)
    in_group = (e_idx // MOE_EPG) == g_sel
    masked = jnp.where(in_group, el, -jnp.inf)
    m1 = jnp.max(masked, axis=1, keepdims=True)
    i1 = jnp.min(jnp.where(masked == m1, e_idx, LANES), axis=1, keepdims=True)
    masked2 = jnp.where(e_idx == i1, -jnp.inf, masked)
    m2 = jnp.max(masked2, axis=1, keepdims=True)
    i2 = jnp.min(jnp.where(masked2 == m2, e_idx, LANES), axis=1, keepdims=True)
    e2 = jnp.exp(m2 - m1)
    w1 = g_p / (1.0 + e2)
    w2 = w1 * e2
    gate = jnp.where(e_idx == i1, w1, 0.0) + jnp.where(e_idx == i2, w2, 0.0)
    lo = jnp.minimum(i1, i2) - g_sel * MOE_EPG
    hi = jnp.maximum(i1, i2) - g_sel * MOE_EPG
    pair = ((lo * (2 * MOE_EPG - 1 - lo)) >> 1) + (hi - lo - 1)
    return gate, g_sel * MOE_PAIRS + pair


def _outproj_ln_router_kernel(a_ref, b_ref, x_ref, wa_ref, wb_ref, g_ref, beta_ref, wgrp_ref, bgrp_ref, wexp_ref,
                              bexp_ref, xg_ref, bucket_ref, rank_ref, count_ref, run_ref):
    i = pl.program_id(0)
    tm, d = x_ref.shape

    @pl.when(i == 0)
    def _():
        run_ref[...] = jnp.zeros_like(run_ref)

    y = _dot(a_ref[...], wa_ref[...]) + _dot(b_ref[...], wb_ref[...])
    x1 = _layer_norm_rows(DEEPNORM_ALPHA * x_ref[...] + y, g_ref[...], beta_ref[...])
    gate, bucket = _router(x1, wgrp_ref[...], bgrp_ref[...], wexp_ref[...], bexp_ref[...])
    xg_ref[:, 0:d] = x1
    xg_ref[:, d:d + LANES] = gate
    bucket_ref[...] = bucket
    onehot = _iota((tm, LANES), 1) == bucket
    tri = (_iota((tm, tm), 1) <= _iota((tm, tm), 0)).astype(BF16)
    prefix = _dot(tri, jnp.where(onehot, 1.0, 0.0).astype(BF16))
    before = run_ref[...]
    rank = jnp.sum(jnp.where(onehot, prefix + before, 0.0), axis=1, keepdims=True) - 1.0
    rank_ref[...] = rank.astype(jnp.int32)
    run_ref[...] = before + prefix[tm - 1:tm, :]
    count_ref[...] = run_ref[...]


def _outproj_ln_router(a, b, x2, w_out, ln_g, ln_b, w_group, b_group, w_expert, b_expert, tm):
    t, d = x2.shape
    ka, kb = a.shape[1], b.shape[1]
    pad = LANES - MOE_EXPERTS
    consts = [w_out[:ka].astype(BF16), w_out[ka:].astype(BF16),
              ln_g.reshape(1, -1).astype(F32), ln_b.reshape(1, -1).astype(F32),
              w_group.astype(F32), b_group.reshape(1, -1).astype(F32),
              jnp.pad(w_expert.astype(F32), ((0, 0), (0, pad))), jnp.pad(b_expert.reshape(1, -1).astype(F32), ((0, 0), (0, pad)))]
    row = lambda n: pl.BlockSpec((tm, n), lambda i: (i, 0))
    full = lambda c: pl.BlockSpec(c.shape, lambda i: (0, 0))
    return pl.pallas_call(
        _outproj_ln_router_kernel,
        grid=(t // tm,),
        in_specs=[row(ka), row(kb), row(d)] + [full(c) for c in consts],
        out_specs=[row(d + LANES), row(1), row(1), pl.BlockSpec((1, LANES), lambda i: (0, 0))],
        out_shape=[jax.ShapeDtypeStruct((t, d + LANES), F32),
                   jax.ShapeDtypeStruct((t, 1), jnp.int32),
                   jax.ShapeDtypeStruct((t, 1), jnp.int32),
                   jax.ShapeDtypeStruct((1, LANES), F32)],
        scratch_shapes=[pltpu.VMEM((1, LANES), F32)],
        compiler_params=_cparams(("arbitrary",)),
        name="outproj_ln_router",
    )(a, b, x2, *consts)


def _moe_plan(bucket, rank, counts, t):
    counts = counts[0, :MOE_BUCKETS].astype(jnp.int32)
    padded = ((counts + MOE_TILE - 1) // MOE_TILE) * MOE_TILE
    ends = jnp.cumsum(padded)
    starts = ends - padded
    pos = jnp.take(starts, bucket[:, 0]) + rank[:, 0]
    n_tiles = t // MOE_TILE + MOE_BUCKETS
    tile_start = jnp.arange(n_tiles, dtype=jnp.int32) * MOE_TILE
    tile_bucket = jnp.minimum(jnp.sum((tile_start[:, None] >= ends[None, :]).astype(jnp.int32), axis=1),
                              MOE_BUCKETS - 1)
    pairs = [(a, b) for a in range(MOE_EPG) for b in range(a + 1, MOE_EPG)]
    first = jnp.asarray([g * MOE_EPG + a for g in range(MOE_GROUPS) for a, _ in pairs], jnp.int32)
    second = jnp.asarray([g * MOE_EPG + b for g in range(MOE_GROUPS) for _, b in pairs], jnp.int32)
    n_used = (ends[MOE_BUCKETS - 1] // MOE_TILE).reshape(1)
    return pos, jnp.take(first, tile_bucket), jnp.take(second, tile_bucket), n_used, n_tiles


def _row_dma_wait(src_hbm, dst, sem, n):
    def body(r, c):
        pltpu.make_async_copy(src_hbm.at[pl.ds(0, 1)], dst.at[pl.ds(0, 1)], sem).wait()
        return c
    lax.fori_loop(0, n, body, 0)


def _moe_scatter_kernel(pos_ref, xg_hbm, init_hbm, xs_hbm, sem):
    del init_hbm
    n = pos_ref.shape[0]
    base = pl.program_id(0) * n

    def body(r, c):
        pltpu.make_async_copy(xg_hbm.at[pl.ds(base + r, 1)], xs_hbm.at[pl.ds(pos_ref[r], 1)], sem).start()
        return c
    lax.fori_loop(0, n, body, 0)
    _row_dma_wait(xg_hbm, xs_hbm, sem, n)


def _moe_scatter(xg, pos, n_rows, chunk):
    t, w = xg.shape
    return pl.pallas_call(
        _moe_scatter_kernel,
        grid=(t // chunk,),
        in_specs=[pl.BlockSpec((chunk,), lambda i: (i,), memory_space=pltpu.SMEM),
                  pl.BlockSpec(memory_space=pl.ANY), pl.BlockSpec(memory_space=pl.ANY)],
        out_specs=pl.BlockSpec(memory_space=pl.ANY),
        out_shape=jax.ShapeDtypeStruct((n_rows, w), xg.dtype),
        scratch_shapes=[pltpu.SemaphoreType.DMA(())],
        input_output_aliases={2: 0},
        compiler_params=_cparams(("arbitrary",)),
        name="moe_scatter",
    )(pos, xg, jnp.zeros((n_rows, w), xg.dtype))


def _moe_ffn_kernel(ea_ref, eb_ref, nu_ref, xs_ref, wga_ref, wua_ref, wda_ref, wgb_ref, wub_ref, wdb_ref, y_ref):
    i = pl.program_id(0)
    d = wga_ref.shape[0]

    @pl.when(i < nu_ref[0])
    def _():
        x = xs_ref[:, 0:d].astype(BF16)
        gate = xs_ref[:, d:d + LANES]
        lane = _iota(gate.shape, 1)
        y = None
        for e_ref, wg_ref, wu_ref, wd_ref in ((ea_ref, wga_ref, wua_ref, wda_ref), (eb_ref, wgb_ref, wub_ref, wdb_ref)):
            w = jnp.sum(jnp.where(lane == e_ref[i], gate, 0.0), axis=1, keepdims=True)
            hid = _silu(_dot(x, wg_ref[...])) * _dot(x, wu_ref[...]) * w
            part = _dot(hid.astype(BF16), wd_ref[...])
            y = part if y is None else y + part
        y_ref[...] = y

    @pl.when(i >= nu_ref[0])
    def _():
        y_ref[...] = jnp.zeros_like(y_ref)


def _moe_ffn(xs, ea, eb, n_used, n_tiles, w_gate, w_up, w_down):
    d = w_gate.shape[1]
    wg, wu, wd = w_gate.astype(BF16), w_up.astype(BF16), w_down.astype(BF16)
    up_a = pl.BlockSpec((None, d, MOE_FF), lambda i, ea, eb, nu: (ea[i], 0, 0))
    dn_a = pl.BlockSpec((None, MOE_FF, d), lambda i, ea, eb, nu: (ea[i], 0, 0))
    up_b = pl.BlockSpec((None, d, MOE_FF), lambda i, ea, eb, nu: (eb[i], 0, 0))
    dn_b = pl.BlockSpec((None, MOE_FF, d), lambda i, ea, eb, nu: (eb[i], 0, 0))
    return pl.pallas_call(
        _moe_ffn_kernel,
        grid_spec=pltpu.PrefetchScalarGridSpec(
            num_scalar_prefetch=3,
            grid=(n_tiles,),
            in_specs=[pl.BlockSpec((MOE_TILE, d + LANES), lambda i, ea, eb, nu: (i, 0)),
                      up_a, up_a, dn_a, up_b, up_b, dn_b],
            out_specs=pl.BlockSpec((MOE_TILE, d), lambda i, ea, eb, nu: (i, 0))),
        out_shape=jax.ShapeDtypeStruct((n_tiles * MOE_TILE, d), F32),
        compiler_params=_cparams(("arbitrary",)),
        name="moe_ffn",
    )(ea, eb, n_used, xs, wg, wu, wd, wg, wu, wd)


def _moe_combine_kernel(pos_ref, y_hbm, x_ref, g_ref, beta_ref, o_ref, buf_ref, sem):
    n = pos_ref.shape[0]

    def body(r, c):
        pltpu.make_async_copy(y_hbm.at[pl.ds(pos_ref[r], 1)], buf_ref.at[pl.ds(r, 1)], sem).start()
        return c
    lax.fori_loop(0, n, body, 0)
    _row_dma_wait(y_hbm, buf_ref, sem, n)
    r = DEEPNORM_ALPHA * x_ref[...] + buf_ref[...]
    o_ref[...] = _layer_norm_rows(r, g_ref[...], beta_ref[...])


def _moe_combine(y_sorted, pos, xg, ln_g, ln_b, tm):
    t = pos.shape[0]
    d = y_sorted.shape[1]
    ln = [ln_g.reshape(1, -1).astype(F32), ln_b.reshape(1, -1).astype(F32)]
    return pl.pallas_call(
        _moe_combine_kernel,
        grid=(t // tm,),
        in_specs=[pl.BlockSpec((tm,), lambda i: (i,), memory_space=pltpu.SMEM),
                  pl.BlockSpec(memory_space=pl.ANY),
                  pl.BlockSpec((tm, d), lambda i: (i, 0))]
        + [pl.BlockSpec(c.shape, lambda i: (0, 0)) for c in ln],
        out_specs=pl.BlockSpec((tm, d), lambda i: (i, 0)),
        out_shape=jax.ShapeDtypeStruct((t, d), F32),
        scratch_shapes=[pltpu.VMEM((tm, d), F32), pltpu.SemaphoreType.DMA(())],
        compiler_params=_cparams(("arbitrary",)),
        name="moe_combine",
    )(pos, y_sorted, xg, *ln)


def _mixer_out_and_moe(a, b, x2, w_out, ln1_g, ln1_b, w_group, b_group, w_expert, b_expert,
                       w_gate, w_up, w_down, ln2_g, ln2_b, tm):
    t = x2.shape[0]
    xg, bucket, rank, counts = _outproj_ln_router(a, b, x2, w_out, ln1_g, ln1_b, w_group, b_group,
                                                  w_expert, b_expert, tm)
    pos, ea, eb, n_used, n_tiles = _moe_plan(bucket, rank, counts, t)
    xs = _moe_scatter(xg, pos, n_tiles * MOE_TILE, _row_tile(t, 1024))
    y_sorted = _moe_ffn(xs, ea, eb, n_used, n_tiles, w_gate, w_up, w_down)
    return _moe_combine(y_sorted, pos, xg, ln2_g, ln2_b, tm)


def _row_tile(n, pref):
    tm = min(pref, n)
    assert n % tm == 0
    return tm


def kernel(x, ev_w_in, ev_conv_w, ev_conv_b, ev_dt_bias, ev_a_log, ev_d_skip, ev_norm_g, ev_w_out, od_w_in, od_q_norm_g, od_w_q_up, od_kv_norm_g, od_w_kv_up, od_f_bias, od_w_out, ln1_g, ln1_b, ln2_g, ln2_b, moe_w_group, moe_b_group, moe_w_expert, moe_b_expert, moe_w_gate, moe_w_up, moe_w_down):
    batch, seq, d = x.shape
    t = batch * seq
    assert seq % SSD_CHUNK == 0 and seq % LANES == 0 and t % MOE_TILE == 0
    tm_proj = _row_tile(seq, 512)
    x2 = x.reshape(t, d)

    z, xbc, dt, dtt, q, k, v = _proj_even(x2, ev_w_in[0], tm_proj)
    y_ssd = _ssd(xbc, z, dt, dtt, ev_conv_w[0], ev_conv_b[0], ev_dt_bias[0], ev_a_log[0], ev_d_skip[0],
                 ev_norm_g[0], batch, seq)
    y_sb = _attention("sb", q, k, v, batch, seq)
    x2 = _mixer_out_and_moe(y_ssd, y_sb, x2, ev_w_out[0], ln1_g[0], ln1_b[0], moe_w_group[0], moe_b_group[0],
                            moe_w_expert[0], moe_b_expert[0], moe_w_gate[0], moe_w_up[0], moe_w_down[0],
                            ln2_g[0], ln2_b[0], tm_proj)

    cq, ckv, kpe, kps, q, k, v, fh = _proj_odd(x2, od_w_in[0], od_f_bias[0], seq, tm_proj)
    qm, km, vm = _mla_prep(cq, ckv, kpe, kps, od_q_norm_g[0], od_w_q_up[0], od_kv_norm_g[0], od_w_kv_up[0],
                           seq, tm_proj)
    y_mla = _attention("mla", qm, km, vm, batch, seq)
    y_fox = _attention("fox", q, k, v, batch, seq, fh=fh)
    x2 = _mixer_out_and_moe(y_mla, y_fox, x2, od_w_out[0], ln1_g[1], ln1_b[1], moe_w_group[1], moe_b_group[1],
                            moe_w_expert[1], moe_b_expert[1], moe_w_gate[1], moe_w_up[1], moe_w_down[1],
                            ln2_g[1], ln2_b[1], tm_proj)
    return x2.reshape(batch, seq, d)
```

```python
import functools
import math

import jax
import jax.numpy as jnp
from jax import lax
from jax.experimental import pallas as pl
from jax.experimental.pallas import tpu as pltpu

F32 = jnp.float32
BF16 = jnp.bfloat16

SSD_HEADS = 16
SSD_HEAD_DIM = 64
SSD_INNER = SSD_HEADS * SSD_HEAD_DIM
SSD_GROUPS = 2
SSD_STATE = 128
SSD_CONV = 4
SSD_BC = SSD_GROUPS * SSD_STATE
SSD_CONV_DIM = SSD_INNER + 2 * SSD_BC
ATT_HEADS = 8
ATT_HEAD_DIM = 64
ATT_WIDTH = ATT_HEADS * ATT_HEAD_DIM
HEAD_PAIRS = ATT_HEADS // 2
MLA_Q_RANK = 256
MLA_KV_RANK = 128
MLA_NOPE = 64
MLA_ROPE = 32
MLA_V = 64
MLA_CHUNK = 64
ROPE_THETA = 10000.0
MOE_GROUPS = 4
MOE_EPG = 4
MOE_EXPERTS = MOE_GROUPS * MOE_EPG
MOE_FF = 256
DEPTH = 2
DEEPNORM_ALPHA = (2.0 * DEPTH) ** 0.25
LN_EPS = 1e-5
RMS_EPS = 1e-6

LANES = 128
SSD_CHUNK = 128
ATT_TQ = {"sb": 256, "mla": 512, "fox": 256}
ATT_TK = {"sb": 256, "mla": 512, "fox": 256}
ATT_ROW_SPLIT = 1
NEG_BIG = -1e30
EXP_ZERO = 110.0
MOE_TILE = 256
DMA_UNROLL = 8
MOE_PAIRS = MOE_EPG * (MOE_EPG - 1) // 2
MOE_BUCKETS = MOE_GROUPS * MOE_PAIRS
VMEM_LIMIT = 56 * 1024 * 1024


def _cparams(sem):
    return pltpu.CompilerParams(dimension_semantics=sem, vmem_limit_bytes=VMEM_LIMIT)


def _dot(a, b):
    return jnp.dot(a, b, preferred_element_type=F32)


def _dot_nt(a, b):
    return lax.dot_general(a, b, (((1,), (1,)), ((), ())), preferred_element_type=F32)


def _split3(x):
    hi = x.astype(BF16)
    r1 = x - hi.astype(F32)
    mid = r1.astype(BF16)
    lo = (r1 - mid.astype(F32)).astype(BF16)
    return hi, mid, lo


def _dot01_right(x, m01):
    hi, mid, lo = _split3(x)
    return _dot(hi, m01) + _dot(mid, m01) + _dot(lo, m01)


def _dot01_left(m01, x):
    hi, mid, lo = _split3(x)
    return _dot(m01, hi) + _dot(m01, mid) + _dot(m01, lo)


def _softplus(x):
    return jnp.maximum(x, 0.0) + jnp.log(1.0 + jnp.exp(-jnp.abs(x)))


def _silu(x):
    return x * (1.0 / (1.0 + jnp.exp(-x)))


def _iota(shape, dim):
    return lax.broadcasted_iota(jnp.int32, shape, dim)


def _proj_even_kernel(x_ref, wz_ref, wx_ref, wdt_ref, wdtt_ref, wq_ref, wk_ref, wv_ref,
                      z_ref, xbc_ref, dt_ref, dtt_ref, q_ref, k_ref, v_ref):
    x = x_ref[...].astype(BF16)
    z_ref[...] = _dot(x, wz_ref[...]).astype(z_ref.dtype)
    xbc_ref[...] = _dot(x, wx_ref[...]).astype(xbc_ref.dtype)
    dt_ref[...] = _dot(x, wdt_ref[...])
    dtt_ref[...] = _dot_nt(wdtt_ref[...], x)
    q_ref[...] = _dot(x, wq_ref[...]).astype(q_ref.dtype)
    k_ref[...] = _dot(x, wk_ref[...]).astype(k_ref.dtype)
    v_ref[...] = _dot(x, wv_ref[...]).astype(v_ref.dtype)


def _proj_even(x2, w_in, tm):
    t, d = x2.shape
    o = 0
    parts = []
    for n in (SSD_INNER, SSD_CONV_DIM, SSD_HEADS, ATT_WIDTH, ATT_WIDTH, ATT_WIDTH):
        parts.append(w_in[:, o:o + n])
        o += n
    wz, wx, wdt, wq, wk, wv = parts
    wq = wq * (1.0 / math.sqrt(ATT_HEAD_DIM))
    ws = [wz.astype(BF16), wx.astype(BF16), wdt.astype(BF16), wdt.T.astype(BF16),
          wq.astype(BF16), wk.astype(BF16), wv.astype(BF16)]
    row = lambda n: pl.BlockSpec((tm, n), lambda i: (i, 0))
    full = lambda a: pl.BlockSpec(a.shape, lambda i: (0, 0))
    return pl.pallas_call(
        _proj_even_kernel,
        grid=(t // tm,),
        in_specs=[row(d)] + [full(w) for w in ws],
        out_specs=[row(SSD_INNER), row(SSD_CONV_DIM), row(SSD_HEADS),
                   pl.BlockSpec((SSD_HEADS, tm), lambda i: (0, i)),
                   row(ATT_WIDTH), row(ATT_WIDTH), row(ATT_WIDTH)],
        out_shape=[jax.ShapeDtypeStruct((t, SSD_INNER), BF16),
                   jax.ShapeDtypeStruct((t, SSD_CONV_DIM), BF16),
                   jax.ShapeDtypeStruct((t, SSD_HEADS), F32),
                   jax.ShapeDtypeStruct((SSD_HEADS, t), F32),
                   jax.ShapeDtypeStruct((t, ATT_WIDTH), BF16),
                   jax.ShapeDtypeStruct((t, ATT_WIDTH), BF16),
                   jax.ShapeDtypeStruct((t, ATT_WIDTH), BF16)],
        compiler_params=_cparams(("parallel",)),
        name="proj_even",
    )(x2, *ws)


def _proj_odd_kernel(x_ref, wcq_ref, wckv_ref, wkpe_ref, wkps_ref, wq_ref, wk_ref, wv_ref, wft_ref, fb_ref,
                     cq_ref, ckv_ref, kpe_ref, kps_ref, q_ref, k_ref, v_ref, fh_ref, carry_ref, *, tiles_per_seq):
    i = pl.program_id(0)
    tm = x_ref.shape[0]
    x = x_ref[...].astype(BF16)
    cq_ref[...] = _dot(x, wcq_ref[...])
    ckv_ref[...] = _dot(x, wckv_ref[...])
    kpe_ref[...] = _dot(x, wkpe_ref[...])
    kps_ref[...] = _dot(x, wkps_ref[...])
    q_ref[...] = _dot(x, wq_ref[...]).astype(q_ref.dtype)
    k_ref[...] = _dot(x, wk_ref[...]).astype(k_ref.dtype)
    v_ref[...] = _dot(x, wv_ref[...]).astype(v_ref.dtype)

    @pl.when(i % tiles_per_seq == 0)
    def _():
        carry_ref[...] = jnp.zeros_like(carry_ref)

    f_raw = _dot_nt(wft_ref[...], x) + fb_ref[...]
    log_f = -_softplus(-f_raw)
    upper = (_iota((tm, tm), 0) <= _iota((tm, tm), 1)).astype(BF16)
    cum = _dot01_right(log_f, upper) + carry_ref[...]
    fh_ref[...] = cum
    carry_ref[...] = cum[:, tm - 1:tm]


def _proj_odd(x2, w_in, f_bias, seq, tm):
    t, d = x2.shape
    sizes = (MLA_Q_RANK, MLA_KV_RANK, MLA_ROPE, ATT_WIDTH, ATT_WIDTH, ATT_WIDTH, ATT_HEADS)
    o = 0
    parts = []
    for n in sizes:
        parts.append(w_in[:, o:o + n])
        o += n
    wcq, wckv, wkpe, wq, wk, wv, wf = parts
    half = MLA_ROPE // 2
    pad_l = jnp.zeros((d, MLA_NOPE), F32)
    pad_r = jnp.zeros((d, LANES - MLA_NOPE - MLA_ROPE), F32)
    wkpe_p = jnp.concatenate([pad_l, wkpe, pad_r], axis=1)
    wkps_p = jnp.concatenate([pad_l, -wkpe[:, half:], wkpe[:, :half], pad_r], axis=1)
    wq = wq * (1.0 / math.sqrt(ATT_HEAD_DIM))
    ws = [wcq.astype(BF16), wckv.astype(BF16), wkpe_p.astype(BF16), wkps_p.astype(BF16),
          wq.astype(BF16), wk.astype(BF16), wv.astype(BF16), wf.T.astype(BF16)]
    fb = f_bias.reshape(ATT_HEADS, 1).astype(F32)
    row = lambda n: pl.BlockSpec((tm, n), lambda i: (i, 0))
    full = lambda a: pl.BlockSpec(a.shape, lambda i: (0, 0))
    return pl.pallas_call(
        functools.partial(_proj_odd_kernel, tiles_per_seq=seq // tm),
        grid=(t // tm,),
        in_specs=[row(d)] + [full(w) for w in ws] + [full(fb)],
        out_specs=[row(MLA_Q_RANK), row(MLA_KV_RANK), row(LANES), row(LANES),
                   row(ATT_WIDTH), row(ATT_WIDTH), row(ATT_WIDTH),
                   pl.BlockSpec((ATT_HEADS, tm), lambda i: (0, i))],
        out_shape=[jax.ShapeDtypeStruct((t, MLA_Q_RANK), F32),
                   jax.ShapeDtypeStruct((t, MLA_KV_RANK), F32),
                   jax.ShapeDtypeStruct((t, LANES), F32),
                   jax.ShapeDtypeStruct((t, LANES), F32),
                   jax.ShapeDtypeStruct((t, ATT_WIDTH), BF16),
                   jax.ShapeDtypeStruct((t, ATT_WIDTH), BF16),
                   jax.ShapeDtypeStruct((t, ATT_WIDTH), BF16),
                   jax.ShapeDtypeStruct((ATT_HEADS, t), F32)],
        scratch_shapes=[pltpu.VMEM((ATT_HEADS, 1), F32)],
        compiler_params=_cparams(("arbitrary",)),
        name="proj_odd",
    )(x2, *ws, fb)


def _ssd_kernel(xbc_ref, z_ref, dt_ref, dtt_ref, cw_ref, cb_ref, dtb_ref, dtbt_ref, alog_ref, alogt_ref,
                dskip_ref, ng_ref, o_ref, buf_ref, st_ref, y_ref):
    c = pl.program_id(1)
    L = SSD_CHUNK
    PAD = 8

    @pl.when(c == 0)
    def _():
        buf_ref[0:PAD, :] = jnp.zeros((PAD, SSD_CONV_DIM), F32)
        st_ref[...] = jnp.zeros_like(st_ref)

    buf_ref[PAD:PAD + L, :] = xbc_ref[...].astype(F32)
    acc = jnp.zeros((L, SSD_CONV_DIM), F32) + cb_ref[...]
    for kk in range(SSD_CONV):
        off = PAD - (SSD_CONV - 1) + kk
        acc = acc + buf_ref[off:off + L, :] * cw_ref[kk:kk + 1, :]
    buf_ref[0:PAD, :] = buf_ref[L:L + PAD, :]
    xbc = _silu(acc)
    b_mat = xbc[:, SSD_INNER:SSD_INNER + SSD_BC]
    c_mat = xbc[:, SSD_INNER + SSD_BC:]
    bt = jnp.transpose(b_mat).astype(BF16)
    c_bf = c_mat.astype(BF16)

    dt = _softplus(dt_ref[...] + dtb_ref[...])
    dtt = _softplus(dtt_ref[...] + dtbt_ref[...])
    da = dt * (-jnp.exp(alog_ref[...]))
    dat = dtt * (-jnp.exp(alogt_ref[...]))
    row = _iota((L, L), 0)
    col = _iota((L, L), 1)
    causal = col <= row
    a_cum = _dot01_left(causal.astype(BF16), da)
    a_cumt = _dot01_right(dat, (row <= col).astype(BF16))
    a_end = a_cum[L - 1:L, :]
    low_half = _iota((1, LANES), 1) < SSD_HEAD_DIM

    def pair_cols(m, h0):
        n = m.shape[0]
        return jnp.where(low_half, jnp.broadcast_to(m[:, h0:h0 + 1], (n, LANES)),
                         jnp.broadcast_to(m[:, h0 + 1:h0 + 2], (n, LANES)))

    sumsq = jnp.zeros((L, 1), F32)
    for g in range(SSD_GROUPS):
        bt_g = bt[g * SSD_STATE:(g + 1) * SSD_STATE, :]
        c_g = c_bf[:, g * SSD_STATE:(g + 1) * SSD_STATE]
        cb = _dot(c_g, bt_g)
        for pp in range(SSD_HEADS // SSD_GROUPS // 2):
            p = g * (SSD_HEADS // SSD_GROUPS // 2) + pp
            h0 = 2 * p
            lanes = slice(p * LANES, (p + 1) * LANES)
            xs = xbc[:, lanes]
            xs_bf = xs.astype(BF16)
            yd = []
            for h in (h0, h0 + 1):
                seg = a_cum[:, h:h + 1] - a_cumt[h:h + 1, :]
                dec = jnp.where(causal, jnp.exp(jnp.minimum(seg, 0.0)), 0.0)
                m = cb * dec * dtt[h:h + 1, :]
                yd.append(_dot(m.astype(BF16), xs_bf))
            y = jnp.where(low_half, yd[0], yd[1])
            a_p = pair_cols(a_cum, h0)
            dt_p = pair_cols(dt, h0)
            end_p = pair_cols(a_end, h0)
            st = st_ref[p]
            y = y + _dot(c_g, st.astype(BF16)) * jnp.exp(a_p)
            xw = xs * (jnp.exp(end_p - a_p) * dt_p)
            st_ref[p] = st * jnp.exp(end_p) + _dot(bt_g, xw.astype(BF16))
            y = y + dskip_ref[:, lanes] * xs
            y = y * _silu(z_ref[:, lanes].astype(F32))
            y_ref[:, lanes] = y
            sumsq = sumsq + jnp.sum(y * y, axis=1, keepdims=True)
    inv = lax.rsqrt(sumsq * (1.0 / SSD_INNER) + RMS_EPS)
    o_ref[...] = (y_ref[...] * inv * ng_ref[...]).astype(o_ref.dtype)


def _ssd(xbc, z, dt, dtt, conv_w, conv_b, dt_bias, a_log, d_skip, norm_g, batch, seq):
    t = xbc.shape[0]
    L = SSD_CHUNK
    nc = seq // L
    row = lambda n: pl.BlockSpec((L, n), lambda b, c: (b * nc + c, 0))
    full = lambda a: pl.BlockSpec(a.shape, lambda b, c: (0,) * a.ndim)
    params = [conv_w.astype(F32), conv_b.reshape(1, -1).astype(F32),
              dt_bias.reshape(1, -1).astype(F32), dt_bias.reshape(-1, 1).astype(F32),
              a_log.reshape(1, -1).astype(F32), a_log.reshape(-1, 1).astype(F32),
              jnp.repeat(d_skip.astype(F32), SSD_HEAD_DIM).reshape(1, -1),
              norm_g.reshape(1, -1).astype(F32)]
    return pl.pallas_call(
        _ssd_kernel,
        grid=(batch, nc),
        in_specs=[row(SSD_CONV_DIM), row(SSD_INNER), row(SSD_HEADS),
                  pl.BlockSpec((SSD_HEADS, L), lambda b, c: (0, b * nc + c))] + [full(a) for a in params],
        out_specs=row(SSD_INNER),
        out_shape=jax.ShapeDtypeStruct((t, SSD_INNER), BF16),
        scratch_shapes=[pltpu.VMEM((L + 8, SSD_CONV_DIM), F32),
                        pltpu.VMEM((SSD_HEADS // 2, SSD_STATE, LANES), F32),
                        pltpu.VMEM((L, SSD_INNER), F32)],
        compiler_params=_cparams(("parallel", "arbitrary")),
        name="ssd",
    )(xbc, z, dt, dtt, *params)


def _attention_kernel(*refs, mode, tq, tk):
    if mode == "fox":
        q_ref, k_ref, v_ref, fh_ref, o_ref, acc_ref, m_ref, l_ref, kn_ref = refs
    else:
        q_ref, k_ref, v_ref, o_ref, acc_ref, m_ref, l_ref = refs
        fh_ref = None
    i = pl.program_id(2)
    n_diag = tq // tk
    lane = _iota((1, LANES), 1)
    low_half = lane < ATT_HEAD_DIM
    row = _iota((tq, tk), 0)
    col = _iota((tq, tk), 1)

    def diag_mask(dd):
        c = col + dd * tk
        if mode == "sb":
            return c < row
        if mode == "mla":
            return (c // MLA_CHUNK) <= (row // MLA_CHUNK)
        return c <= row

    def rep(x):
        return jnp.concatenate([x] * (tk // LANES), axis=1) if tk > LANES else x

    if mode == "sb":
        u_mat = (_iota((tk, tk), 0) > _iota((tk, tk), 1)).astype(BF16)

    q = q_ref[...]
    if mode == "mla":
        qs = [q[:, 0:LANES], q[:, LANES:2 * LANES]]
    else:
        zero = jnp.zeros_like(q)
        qs = [jnp.where(low_half, q, zero), jnp.where(low_half, zero, q)]

    acc_ref[...] = jnp.zeros_like(acc_ref)
    if mode == "sb":
        m_ref[...] = jnp.zeros_like(m_ref)
    else:
        m_ref[...] = jnp.full(m_ref.shape, NEG_BIG, F32)
        l_ref[...] = jnp.zeros_like(l_ref)

    if mode == "fox":
        q0 = pl.multiple_of(i * tq, tq)
        fh_q = [fh_ref[hh:hh + 1, pl.ds(q0, LANES)][:, 0:1] for hh in range(2)]

    rows = tq // ATT_ROW_SPLIT
    streams = [(hh, slice(rb * rows, (rb + 1) * rows)) for hh in range(2) for rb in range(ATT_ROW_SPLIT)]

    def tile(j, mask):
        k0 = pl.multiple_of(j * tk, tk)
        k_t = k_ref[pl.ds(k0, tk), :]
        v_t = v_ref[pl.ds(k0, tk), :]
        k_h = [k_t[:, 0:LANES], k_t[:, LANES:2 * LANES]] if mode == "mla" else [k_t, k_t]
        msk = [None if mask is None else mask[rs] for _, rs in streams]
        zs = [_dot_nt(qs[hh][rs], k_h[hh]) for hh, rs in streams]
        if mode == "sb":
            runs = [m_ref[hh, rs] for hh, rs in streams]
            accs = [acc_ref[hh, rs] for hh, rs in streams]
            sps = [_softplus(z) for z in zs]
            l1ms = [-sp if m is None else jnp.where(m, -sp, 0.0) for sp, m in zip(sps, msk)]
            his = [l1m.astype(BF16) for l1m in l1ms]
            los = [(l1m - hi.astype(F32)).astype(BF16) for l1m, hi in zip(l1ms, his)]
            css = [_dot(hi, u_mat) + _dot(lo, u_mat) for hi, lo in zip(his, los)]
            ws = [jnp.exp((z - sp) + cs + rep(run)) for z, sp, cs, run in zip(zs, sps, css, runs)]
            ws = [w if m is None else jnp.where(m, w, 0.0) for w, m in zip(ws, msk)]
            pvs = [_dot(w.astype(BF16), v_t) for w in ws]
            for (hh, rs), acc, pv, run, cs, l1m in zip(streams, accs, pvs, runs, css, l1ms):
                acc_ref[hh, rs] = acc + pv
                m_ref[hh, rs] = run + (cs[:, 0:1] + l1m[:, 0:1])
        else:
            if mode == "fox":
                bias = [fh_q[hh] - fh_ref[hh:hh + 1, pl.ds(k0, tk)] for hh in range(2)]
                zs = [z + bias[hh] for z, (hh, _) in zip(zs, streams)]
            zs = [z if m is None else jnp.where(m, z, NEG_BIG) for z, m in zip(zs, msk)]
            m_olds = [m_ref[hh, rs] for hh, rs in streams]
            l_olds = [l_ref[hh, rs] for hh, rs in streams]
            accs = [acc_ref[hh, rs] for hh, rs in streams]
            m_news = [jnp.maximum(m_old, jnp.max(z, axis=1, keepdims=True)) for m_old, z in zip(m_olds, zs)]
            ps = [jnp.exp(z - rep(m_new)) for z, m_new in zip(zs, m_news)]
            alphas = [jnp.exp(m_old - m_new) for m_old, m_new in zip(m_olds, m_news)]
            pvs = [_dot(p.astype(BF16), v_t) for p in ps]
            for (hh, rs), m_new, alpha, l_old, acc, p, pv in zip(streams, m_news, alphas, l_olds, accs, ps, pvs):
                l_ref[hh, rs] = alpha * l_old + jnp.sum(p, axis=1, keepdims=True)
                acc_ref[hh, rs] = alpha * acc + pv
                m_ref[hh, rs] = m_new

    def more(j_done):
        if mode == "sb":
            return (jnp.max(m_ref[...]) > -EXP_ZERO).astype(jnp.int32)
        if mode == "fox":
            kl = pl.multiple_of(jnp.maximum(j_done - 1, 0) * tk, tk)
            best = None
            for hh in range(2):
                fh_last = fh_ref[hh:hh + 1, pl.ds(kl, tk)][:, tk - 1:tk]
                v = jnp.max(qk_bound - m_ref[hh], axis=(0, 1), keepdims=True) + (fh_q[hh] - fh_last)
                best = v if best is None else jnp.maximum(best, v)
            return (jnp.max(best) > -EXP_ZERO).astype(jnp.int32)
        return jnp.int32(1)

    if mode == "fox":
        @pl.when(i == 0)
        def _():
            kf = k_ref[...].astype(F32)
            n2 = jnp.max(jnp.sum(kf * kf, axis=1, keepdims=True), axis=0, keepdims=True)
            kn_ref[...] = jnp.broadcast_to(jnp.sqrt(n2), kn_ref.shape)
        qf = q.astype(F32)
        qk_bound = jnp.sqrt(jnp.sum(qf * qf, axis=1, keepdims=True)) * kn_ref[0:1, :]

    n_off = i * n_diag
    for dd in reversed(range(n_diag)):
        tile(n_off + dd, diag_mask(dd))

    def cond(c):
        return jnp.logical_and(c[0] < n_off, c[1] > 0)

    def body(c):
        j = n_off - 1 - c[0]
        tile(j, None)
        return c[0] + 1, more(j)
    lax.while_loop(cond, body, (jnp.int32(0), more(n_off)))
    if mode == "sb":
        out = [acc_ref[0], acc_ref[1]]
    else:
        out = [acc_ref[hh] / l_ref[hh] for hh in range(2)]
    o_ref[...] = jnp.where(low_half, out[0], out[1]).astype(o_ref.dtype)


def _attention(mode, q, k, v, batch, seq, fh=None):
    t = v.shape[0]
    tq = min(ATT_TQ[mode], seq)
    tk = min(ATT_TK[mode], tq)
    nq = seq // tq
    qk_w = 2 * LANES if mode == "mla" else LANES
    in_specs = [pl.BlockSpec((tq, qk_w), lambda b, h, i: (b * nq + i, h)),
                pl.BlockSpec((seq, qk_w), lambda b, h, i: (b, h)),
                pl.BlockSpec((seq, LANES), lambda b, h, i: (b, h))]
    args = [q, k, v]
    scratch = [pltpu.VMEM((2, tq, LANES), F32), pltpu.VMEM((2, tq, LANES), F32), pltpu.VMEM((2, tq, LANES), F32)]
    if mode == "fox":
        in_specs.append(pl.BlockSpec((None, 2, seq), lambda b, h, i: (h, 0, b)))
        args.append(fh.reshape(HEAD_PAIRS, 2, t))
        scratch.append(pltpu.VMEM((8, LANES), F32))
    return pl.pallas_call(
        functools.partial(_attention_kernel, mode=mode, tq=tq, tk=tk),
        grid=(batch, HEAD_PAIRS, nq),
        in_specs=in_specs,
        out_specs=pl.BlockSpec((tq, LANES), lambda b, h, i: (b * nq + i, h)),
        out_shape=jax.ShapeDtypeStruct((t, ATT_WIDTH), BF16),
        scratch_shapes=scratch,
        compiler_params=_cparams(("parallel", "parallel", "arbitrary")),
        name="attn_" + mode,
    )(*args)


def _mla_prep_kernel(cq_ref, ckv_ref, kpe_ref, kps_ref, cos_ref, sin_ref, qg_ref, kvg_ref,
                     wqm_ref, wqs_ref, wk_ref, wv_ref, q_ref, k_ref, v_ref):
    def rms(x, g):
        return x * lax.rsqrt(jnp.mean(x * x, axis=1, keepdims=True) + RMS_EPS) * g

    cqn = rms(cq_ref[...], qg_ref[...]).astype(BF16)
    ckvn = rms(ckv_ref[...], kvg_ref[...]).astype(BF16)
    cos1 = cos_ref[...]
    sin1 = sin_ref[...]
    cos8 = jnp.concatenate([cos1] * ATT_HEADS, axis=1)
    sin8 = jnp.concatenate([sin1] * ATT_HEADS, axis=1)
    qm = _dot(cqn, wqm_ref[...])
    qsw = _dot(cqn, wqs_ref[...])
    scale = 1.0 / math.sqrt(MLA_NOPE + MLA_ROPE)
    q_ref[...] = ((qm * cos8 + qsw * sin8) * scale).astype(q_ref.dtype)
    k_rope = kpe_ref[...] * cos1 + kps_ref[...] * sin1
    k_ref[...] = (_dot(ckvn, wk_ref[...]) + jnp.concatenate([k_rope] * ATT_HEADS, axis=1)).astype(k_ref.dtype)
    v_ref[...] = _dot(ckvn, wv_ref[...]).astype(v_ref.dtype)


def _mla_prep(cq, ckv, kpe, kps, q_norm_g, w_q_up, kv_norm_g, w_kv_up, seq, tm):
    t = cq.shape[0]
    half = MLA_ROPE // 2
    inv_freq = ROPE_THETA ** (-(jnp.arange(0, MLA_ROPE, 2, dtype=F32) / MLA_ROPE))
    ang = jnp.arange(seq, dtype=F32)[:, None] * inv_freq[None, :]
    cos, sin = jnp.cos(ang), jnp.sin(ang)
    pad = LANES - MLA_NOPE - MLA_ROPE
    cos1 = jnp.concatenate([jnp.ones((seq, MLA_NOPE), F32), cos, cos, jnp.zeros((seq, pad), F32)], axis=1)
    sin1 = jnp.concatenate([jnp.zeros((seq, MLA_NOPE), F32), sin, sin, jnp.zeros((seq, pad), F32)], axis=1)
    wq = w_q_up.reshape(MLA_Q_RANK, ATT_HEADS, MLA_NOPE + MLA_ROPE)
    zq = lambda n: jnp.zeros((MLA_Q_RANK, ATT_HEADS, n), F32)
    wq_main = jnp.concatenate([wq, zq(pad)], axis=2).reshape(MLA_Q_RANK, ATT_HEADS * LANES)
    wq_swap = jnp.concatenate([zq(MLA_NOPE), -wq[:, :, MLA_NOPE + half:], wq[:, :, MLA_NOPE:MLA_NOPE + half],
                               zq(pad)], axis=2).reshape(MLA_Q_RANK, ATT_HEADS * LANES)
    wkv = w_kv_up.reshape(MLA_KV_RANK, ATT_HEADS, MLA_NOPE + MLA_V)
    wk = jnp.concatenate([wkv[:, :, :MLA_NOPE], jnp.zeros((MLA_KV_RANK, ATT_HEADS, LANES - MLA_NOPE), F32)],
                         axis=2).reshape(MLA_KV_RANK, ATT_HEADS * LANES)
    wv = wkv[:, :, MLA_NOPE:].reshape(MLA_KV_RANK, ATT_HEADS * MLA_V)
    consts = [q_norm_g.reshape(1, -1).astype(F32), kv_norm_g.reshape(1, -1).astype(F32),
              wq_main.astype(BF16), wq_swap.astype(BF16), wk.astype(BF16), wv.astype(BF16)]
    ns = seq // tm
    row = lambda n: pl.BlockSpec((tm, n), lambda i: (i, 0))
    tab = pl.BlockSpec((tm, LANES), lambda i: (i % ns, 0))
    full = lambda a: pl.BlockSpec(a.shape, lambda i: (0, 0))
    return pl.pallas_call(
        _mla_prep_kernel,
        grid=(t // tm,),
        in_specs=[row(MLA_Q_RANK), row(MLA_KV_RANK), row(LANES), row(LANES), tab, tab] + [full(a) for a in consts],
        out_specs=[row(ATT_HEADS * LANES), row(ATT_HEADS * LANES), row(ATT_HEADS * MLA_V)],
        out_shape=[jax.ShapeDtypeStruct((t, ATT_HEADS * LANES), BF16),
                   jax.ShapeDtypeStruct((t, ATT_HEADS * LANES), BF16),
                   jax.ShapeDtypeStruct((t, ATT_HEADS * MLA_V), BF16)],
        compiler_params=_cparams(("parallel",)),
        name="mla_prep",
    )(cq, ckv, kpe, kps, cos1, sin1, *consts)


def _layer_norm_rows(r, g, b):
    mu = jnp.mean(r, axis=1, keepdims=True)
    d = r - mu
    var = jnp.mean(d * d, axis=1, keepdims=True)
    return d * lax.rsqrt(var + LN_EPS) * g + b


def _router(x, w_hi, w_lo, bias):
    tm = x.shape[0]
    x_hi = x.astype(BF16)
    x_lo = (x - x_hi.astype(F32)).astype(BF16)
    logits = _dot(x_hi, w_hi) + (_dot(x_hi, w_lo) + _dot(x_lo, w_hi)) + bias
    e_idx = _iota((tm, LANES), 1)
    is_group = jnp.logical_and(e_idx >= MOE_EXPERTS, e_idx < MOE_EXPERTS + MOE_GROUPS)
    gl = jnp.where(is_group, logits, -jnp.inf)
    g_max = jnp.max(gl, axis=1, keepdims=True)
    g_p = 1.0 / jnp.sum(jnp.exp(gl - g_max), axis=1, keepdims=True)
    g_sel = jnp.min(jnp.where(gl == g_max, e_idx, LANES), axis=1, keepdims=True) - MOE_EXPERTS
    in_group = (e_idx // MOE_EPG) == g_sel
    masked = jnp.where(in_group, logits, -jnp.inf)
    m1 = jnp.max(masked, axis=1, keepdims=True)
    i1 = jnp.min(jnp.where(masked == m1, e_idx, LANES), axis=1, keepdims=True)
    masked2 = jnp.where(e_idx == i1, -jnp.inf, masked)
    m2 = jnp.max(masked2, axis=1, keepdims=True)
    i2 = jnp.min(jnp.where(masked2 == m2, e_idx, LANES), axis=1, keepdims=True)
    e2 = jnp.exp(m2 - m1)
    w1 = g_p / (1.0 + e2)
    w2 = w1 * e2
    gate = jnp.where(e_idx == i1, w1, 0.0) + jnp.where(e_idx == i2, w2, 0.0)
    lo = jnp.minimum(i1, i2) - g_sel * MOE_EPG
    hi = jnp.maximum(i1, i2) - g_sel * MOE_EPG
    pair = ((lo * (2 * MOE_EPG - 1 - lo)) >> 1) + (hi - lo - 1)
    return gate, g_sel * MOE_PAIRS + pair


def _outproj_ln_router_kernel(a_ref, b_ref, x_ref, wa_ref, wb_ref, g_ref, beta_ref, wr_hi_ref, wr_lo_ref, br_ref,
                              xg_ref, bucket_ref, rank_ref, count_ref, run_ref):
    i = pl.program_id(0)
    tm, d = x_ref.shape

    @pl.when(i == 0)
    def _():
        run_ref[...] = jnp.zeros_like(run_ref)

    y = _dot(a_ref[...], wa_ref[...]) + _dot(b_ref[...], wb_ref[...])
    x1 = _layer_norm_rows(DEEPNORM_ALPHA * x_ref[...] + y, g_ref[...], beta_ref[...])
    gate, bucket = _router(x1, wr_hi_ref[...], wr_lo_ref[...], br_ref[...])
    xg_ref[:, 0:d] = x1
    xg_ref[:, d:d + LANES] = gate
    bucket_ref[...] = bucket
    onehot = _iota((tm, LANES), 1) == bucket
    tri = (_iota((tm, tm), 1) <= _iota((tm, tm), 0)).astype(BF16)
    prefix = _dot(tri, jnp.where(onehot, 1.0, 0.0).astype(BF16))
    before = run_ref[...]
    rank = jnp.sum(jnp.where(onehot, prefix + before, 0.0), axis=1, keepdims=True) - 1.0
    rank_ref[...] = rank.astype(jnp.int32)
    run_ref[...] = before + prefix[tm - 1:tm, :]
    count_ref[...] = run_ref[...]


def _outproj_ln_router(a, b, x2, w_out, ln_g, ln_b, w_group, b_group, w_expert, b_expert, tm):
    t, d = x2.shape
    ka, kb = a.shape[1], b.shape[1]
    pad = LANES - MOE_EXPERTS - MOE_GROUPS
    w_r = jnp.pad(jnp.concatenate([w_expert, w_group], axis=1).astype(F32), ((0, 0), (0, pad)))
    b_r = jnp.pad(jnp.concatenate([b_expert, b_group]).astype(F32), (0, pad)).reshape(1, LANES)
    w_r_hi = w_r.astype(BF16)
    consts = [w_out[:ka].astype(BF16), w_out[ka:].astype(BF16),
              ln_g.reshape(1, -1).astype(F32), ln_b.reshape(1, -1).astype(F32),
              w_r_hi, (w_r - w_r_hi.astype(F32)).astype(BF16), b_r]
    row = lambda n: pl.BlockSpec((tm, n), lambda i: (i, 0))
    full = lambda c: pl.BlockSpec(c.shape, lambda i: (0, 0))
    return pl.pallas_call(
        _outproj_ln_router_kernel,
        grid=(t // tm,),
        in_specs=[row(ka), row(kb), row(d)] + [full(c) for c in consts],
        out_specs=[row(d + LANES), row(1), row(1), pl.BlockSpec((1, LANES), lambda i: (0, 0))],
        out_shape=[jax.ShapeDtypeStruct((t, d + LANES), F32),
                   jax.ShapeDtypeStruct((t, 1), jnp.int32),
                   jax.ShapeDtypeStruct((t, 1), jnp.int32),
                   jax.ShapeDtypeStruct((1, LANES), F32)],
        scratch_shapes=[pltpu.VMEM((1, LANES), F32)],
        compiler_params=_cparams(("arbitrary",)),
        name="outproj_ln_router",
    )(a, b, x2, *consts)


def _moe_plan(bucket, rank, counts, t):
    counts = counts[0, :MOE_BUCKETS].astype(jnp.int32)
    padded = ((counts + MOE_TILE - 1) // MOE_TILE) * MOE_TILE
    ends = jnp.cumsum(padded)
    starts = ends - padded
    pos = jnp.take(starts, bucket[:, 0]) + rank[:, 0]
    n_tiles = t // MOE_TILE + MOE_BUCKETS
    tile_start = jnp.arange(n_tiles, dtype=jnp.int32) * MOE_TILE
    tile_bucket = jnp.minimum(jnp.sum((tile_start[:, None] >= ends[None, :]).astype(jnp.int32), axis=1),
                              MOE_BUCKETS - 1)
    pairs = [(a, b) for a in range(MOE_EPG) for b in range(a + 1, MOE_EPG)]
    first = jnp.asarray([g * MOE_EPG + a for g in range(MOE_GROUPS) for a, _ in pairs], jnp.int32)
    second = jnp.asarray([g * MOE_EPG + b for g in range(MOE_GROUPS) for _, b in pairs], jnp.int32)
    n_used = (ends[MOE_BUCKETS - 1] // MOE_TILE).reshape(1)
    return pos, jnp.take(first, tile_bucket), jnp.take(second, tile_bucket), n_used, n_tiles


def _moe_scatter_kernel(pos_ref, xg_ref, init_hbm, xs_hbm, sem):
    del init_hbm
    n = pos_ref.shape[0]

    def body(r, c):
        pltpu.make_async_copy(xg_ref.at[pl.ds(r, 1)], xs_hbm.at[pl.ds(pos_ref[r], 1)], sem).start()
        return c
    lax.fori_loop(0, n, body, 0, unroll=DMA_UNROLL)
    pltpu.make_async_copy(xg_ref, xs_hbm.at[pl.ds(0, n)], sem).wait()


def _moe_scatter(xg, pos, n_rows, chunk):
    t, w = xg.shape
    return pl.pallas_call(
        _moe_scatter_kernel,
        grid=(t // chunk,),
        in_specs=[pl.BlockSpec((chunk,), lambda i: (i,), memory_space=pltpu.SMEM),
                  pl.BlockSpec((chunk, w), lambda i: (i, 0)), pl.BlockSpec(memory_space=pl.ANY)],
        out_specs=pl.BlockSpec(memory_space=pl.ANY),
        out_shape=jax.ShapeDtypeStruct((n_rows, w), xg.dtype),
        scratch_shapes=[pltpu.SemaphoreType.DMA(())],
        input_output_aliases={2: 0},
        compiler_params=_cparams(("arbitrary",)),
        name="moe_scatter",
    )(pos, xg, jnp.zeros((n_rows, w), xg.dtype))


def _moe_ffn_kernel(ea_ref, eb_ref, nu_ref, xs_ref, wga_ref, wua_ref, wda_ref, wgb_ref, wub_ref, wdb_ref, y_ref):
    i = pl.program_id(0)
    d = wga_ref.shape[0]

    @pl.when(i < nu_ref[0])
    def _():
        x = xs_ref[:, 0:d].astype(BF16)
        gate = xs_ref[:, d:d + LANES]
        lane = _iota(gate.shape, 1)
        y = None
        for e_ref, wg_ref, wu_ref, wd_ref in ((ea_ref, wga_ref, wua_ref, wda_ref), (eb_ref, wgb_ref, wub_ref, wdb_ref)):
            w = jnp.sum(jnp.where(lane == e_ref[i], gate, 0.0), axis=1, keepdims=True)
            hid = _silu(_dot(x, wg_ref[...])) * _dot(x, wu_ref[...]) * w
            part = _dot(hid.astype(BF16), wd_ref[...])
            y = part if y is None else y + part
        y_ref[...] = y

    @pl.when(i >= nu_ref[0])
    def _():
        y_ref[...] = jnp.zeros_like(y_ref)


def _moe_ffn(xs, ea, eb, n_used, n_tiles, w_gate, w_up, w_down):
    d = w_gate.shape[1]
    wg, wu, wd = w_gate.astype(BF16), w_up.astype(BF16), w_down.astype(BF16)
    up_a = pl.BlockSpec((None, d, MOE_FF), lambda i, ea, eb, nu: (ea[i], 0, 0))
    dn_a = pl.BlockSpec((None, MOE_FF, d), lambda i, ea, eb, nu: (ea[i], 0, 0))
    up_b = pl.BlockSpec((None, d, MOE_FF), lambda i, ea, eb, nu: (eb[i], 0, 0))
    dn_b = pl.BlockSpec((None, MOE_FF, d), lambda i, ea, eb, nu: (eb[i], 0, 0))
    return pl.pallas_call(
        _moe_ffn_kernel,
        grid_spec=pltpu.PrefetchScalarGridSpec(
            num_scalar_prefetch=3,
            grid=(n_tiles,),
            in_specs=[pl.BlockSpec((MOE_TILE, d + LANES), lambda i, ea, eb, nu: (i, 0)),
                      up_a, up_a, dn_a, up_b, up_b, dn_b],
            out_specs=pl.BlockSpec((MOE_TILE, d), lambda i, ea, eb, nu: (i, 0))),
        out_shape=jax.ShapeDtypeStruct((n_tiles * MOE_TILE, d), F32),
        compiler_params=_cparams(("arbitrary",)),
        name="moe_ffn",
    )(ea, eb, n_used, xs, wg, wu, wd, wg, wu, wd)


def _moe_combine_kernel(pos_ref, pos_next_ref, y_hbm, x_ref, g_ref, beta_ref, o_ref, buf_ref, sem):
    i = pl.program_id(0)
    n = pos_ref.shape[0]
    slot = i % 2

    def gather(p_ref, s):
        def body(r, c):
            pltpu.make_async_copy(y_hbm.at[pl.ds(p_ref[r], 1)], buf_ref.at[s, pl.ds(r, 1)], sem.at[s]).start()
            return c
        lax.fori_loop(0, n, body, 0, unroll=DMA_UNROLL)

    @pl.when(i == 0)
    def _():
        gather(pos_ref, slot)

    @pl.when(i + 1 < pl.num_programs(0))
    def _():
        gather(pos_next_ref, 1 - slot)

    pltpu.make_async_copy(y_hbm.at[pl.ds(0, n)], buf_ref.at[slot], sem.at[slot]).wait()
    r = DEEPNORM_ALPHA * x_ref[...] + buf_ref[slot]
    o_ref[...] = _layer_norm_rows(r, g_ref[...], beta_ref[...])


def _moe_combine(y_sorted, pos, xg, ln_g, ln_b, tm):
    t = pos.shape[0]
    d = y_sorted.shape[1]
    ln = [ln_g.reshape(1, -1).astype(F32), ln_b.reshape(1, -1).astype(F32)]
    last = t // tm - 1
    return pl.pallas_call(
        _moe_combine_kernel,
        grid=(t // tm,),
        in_specs=[pl.BlockSpec((tm,), lambda i: (i,), memory_space=pltpu.SMEM),
                  pl.BlockSpec((tm,), lambda i: (jnp.minimum(i + 1, last),), memory_space=pltpu.SMEM),
                  pl.BlockSpec(memory_space=pl.ANY),
                  pl.BlockSpec((tm, d), lambda i: (i, 0))]
        + [pl.BlockSpec(c.shape, lambda i: (0, 0)) for c in ln],
        out_specs=pl.BlockSpec((tm, d), lambda i: (i, 0)),
        out_shape=jax.ShapeDtypeStruct((t, d), F32),
        scratch_shapes=[pltpu.VMEM((2, tm, d), F32), pltpu.SemaphoreType.DMA((2,))],
        compiler_params=_cparams(("arbitrary",)),
        name="moe_combine",
    )(pos, pos, y_sorted, xg, *ln)


def _mixer_out_and_moe(a, b, x2, w_out, ln1_g, ln1_b, w_group, b_group, w_expert, b_expert,
                       w_gate, w_up, w_down, ln2_g, ln2_b, tm):
    t = x2.shape[0]
    xg, bucket, rank, counts = _outproj_ln_router(a, b, x2, w_out, ln1_g, ln1_b, w_group, b_group,
                                                  w_expert, b_expert, tm)
    pos, ea, eb, n_used, n_tiles = _moe_plan(bucket, rank, counts, t)
    xs = _moe_scatter(xg, pos, n_tiles * MOE_TILE, _row_tile(t, 1024))
    y_sorted = _moe_ffn(xs, ea, eb, n_used, n_tiles, w_gate, w_up, w_down)
    return _moe_combine(y_sorted, pos, xg, ln2_g, ln2_b, tm)


def _row_tile(n, pref):
    tm = min(pref, n)
    assert n % tm == 0
    return tm


def kernel(x, ev_w_in, ev_conv_w, ev_conv_b, ev_dt_bias, ev_a_log, ev_d_skip, ev_norm_g, ev_w_out, od_w_in, od_q_norm_g, od_w_q_up, od_kv_norm_g, od_w_kv_up, od_f_bias, od_w_out, ln1_g, ln1_b, ln2_g, ln2_b, moe_w_group, moe_b_group, moe_w_expert, moe_b_expert, moe_w_gate, moe_w_up, moe_w_down):
    batch, seq, d = x.shape
    t = batch * seq
    assert seq % SSD_CHUNK == 0 and seq % LANES == 0 and t % MOE_TILE == 0
    tm_proj = _row_tile(seq, 512)
    x2 = x.reshape(t, d)

    z, xbc, dt, dtt, q, k, v = _proj_even(x2, ev_w_in[0], tm_proj)
    y_ssd = _ssd(xbc, z, dt, dtt, ev_conv_w[0], ev_conv_b[0], ev_dt_bias[0], ev_a_log[0], ev_d_skip[0],
                 ev_norm_g[0], batch, seq)
    y_sb = _attention("sb", q, k, v, batch, seq)
    x2 = _mixer_out_and_moe(y_ssd, y_sb, x2, ev_w_out[0], ln1_g[0], ln1_b[0], moe_w_group[0], moe_b_group[0],
                            moe_w_expert[0], moe_b_expert[0], moe_w_gate[0], moe_w_up[0], moe_w_down[0],
                            ln2_g[0], ln2_b[0], tm_proj)

    cq, ckv, kpe, kps, q, k, v, fh = _proj_odd(x2, od_w_in[0], od_f_bias[0], seq, tm_proj)
    qm, km, vm = _mla_prep(cq, ckv, kpe, kps, od_q_norm_g[0], od_w_q_up[0], od_kv_norm_g[0], od_w_kv_up[0],
                           seq, tm_proj)
    y_mla = _attention("mla", qm, km, vm, batch, seq)
    y_fox = _attention("fox", q, k, v, batch, seq, fh=fh)
    x2 = _mixer_out_and_moe(y_mla, y_fox, x2, od_w_out[0], ln1_g[1], ln1_b[1], moe_w_group[1], moe_b_group[1],
                            moe_w_expert[1], moe_b_expert[1], moe_w_gate[1], moe_w_up[1], moe_w_down[1],
                            ln2_g[1], ln2_b[1], tm_proj)
    return x2.reshape(batch, seq, d)
```

```python
import functools
import math

import jax
import jax.numpy as jnp
from jax import lax
from jax.experimental import pallas as pl
from jax.experimental.pallas import tpu as pltpu

F32 = jnp.float32
BF16 = jnp.bfloat16

SSD_HEADS = 16
SSD_HEAD_DIM = 64
SSD_INNER = SSD_HEADS * SSD_HEAD_DIM
SSD_GROUPS = 2
SSD_STATE = 128
SSD_CONV = 4
SSD_BC = SSD_GROUPS * SSD_STATE
SSD_CONV_DIM = SSD_INNER + 2 * SSD_BC
ATT_HEADS = 8
ATT_HEAD_DIM = 64
ATT_WIDTH = ATT_HEADS * ATT_HEAD_DIM
HEAD_PAIRS = ATT_HEADS // 2
MLA_Q_RANK = 256
MLA_KV_RANK = 128
MLA_NOPE = 64
MLA_ROPE = 32
MLA_V = 64
MLA_CHUNK = 64
ROPE_THETA = 10000.0
MOE_GROUPS = 4
MOE_EPG = 4
MOE_EXPERTS = MOE_GROUPS * MOE_EPG
MOE_FF = 256
DEPTH = 2
DEEPNORM_ALPHA = (2.0 * DEPTH) ** 0.25
LN_EPS = 1e-5
RMS_EPS = 1e-6

LANES = 128
SSD_CHUNK = 128
ATT_TQ = {"sb": 256, "mla": 512, "fox": 256}
ATT_TK = {"sb": 256, "mla": 512, "fox": 256}
ATT_DIAG_ROWS = 128
NEG_BIG = -1e30
EXP_ZERO = 110.0
MOE_TILE = 256
DMA_UNROLL = 8
MOE_PAIRS = MOE_EPG * (MOE_EPG - 1) // 2
MOE_BUCKETS = MOE_GROUPS * MOE_PAIRS
VMEM_LIMIT = 56 * 1024 * 1024


def _cparams(sem):
    return pltpu.CompilerParams(dimension_semantics=sem, vmem_limit_bytes=VMEM_LIMIT)


def _dot(a, b):
    return jnp.dot(a, b, preferred_element_type=F32)


def _dot_nt(a, b):
    return lax.dot_general(a, b, (((1,), (1,)), ((), ())), preferred_element_type=F32)


def _split3(x):
    hi = x.astype(BF16)
    r1 = x - hi.astype(F32)
    mid = r1.astype(BF16)
    lo = (r1 - mid.astype(F32)).astype(BF16)
    return hi, mid, lo


def _dot01_right(x, m01):
    hi, mid, lo = _split3(x)
    return _dot(hi, m01) + _dot(mid, m01) + _dot(lo, m01)


def _dot01_left(m01, x):
    hi, mid, lo = _split3(x)
    return _dot(m01, hi) + _dot(m01, mid) + _dot(m01, lo)


def _softplus(x):
    return jnp.maximum(x, 0.0) + jnp.log(1.0 + jnp.exp(-jnp.abs(x)))


def _silu(x):
    return x * (1.0 / (1.0 + jnp.exp(-x)))


def _iota(shape, dim):
    return lax.broadcasted_iota(jnp.int32, shape, dim)


def _proj_even_kernel(x_ref, wz_ref, wx_ref, wdt_ref, wdtt_ref, wq_ref, wk_ref, wv_ref,
                      z_ref, xbc_ref, dt_ref, dtt_ref, q_ref, k_ref, v_ref):
    x = x_ref[...].astype(BF16)
    z_ref[...] = _dot(x, wz_ref[...]).astype(z_ref.dtype)
    xbc_ref[...] = _dot(x, wx_ref[...]).astype(xbc_ref.dtype)
    dt_ref[...] = _dot(x, wdt_ref[...])
    dtt_ref[...] = _dot_nt(wdtt_ref[...], x)
    q_ref[...] = _dot(x, wq_ref[...]).astype(q_ref.dtype)
    k_ref[...] = _dot(x, wk_ref[...]).astype(k_ref.dtype)
    v_ref[...] = _dot(x, wv_ref[...]).astype(v_ref.dtype)


def _proj_even(x2, w_in, tm):
    t, d = x2.shape
    o = 0
    parts = []
    for n in (SSD_INNER, SSD_CONV_DIM, SSD_HEADS, ATT_WIDTH, ATT_WIDTH, ATT_WIDTH):
        parts.append(w_in[:, o:o + n])
        o += n
    wz, wx, wdt, wq, wk, wv = parts
    wq = wq * (1.0 / math.sqrt(ATT_HEAD_DIM))
    ws = [wz.astype(BF16), wx.astype(BF16), wdt.astype(BF16), wdt.T.astype(BF16),
          wq.astype(BF16), wk.astype(BF16), wv.astype(BF16)]
    row = lambda n: pl.BlockSpec((tm, n), lambda i: (i, 0))
    full = lambda a: pl.BlockSpec(a.shape, lambda i: (0, 0))
    return pl.pallas_call(
        _proj_even_kernel,
        grid=(t // tm,),
        in_specs=[row(d)] + [full(w) for w in ws],
        out_specs=[row(SSD_INNER), row(SSD_CONV_DIM), row(SSD_HEADS),
                   pl.BlockSpec((SSD_HEADS, tm), lambda i: (0, i)),
                   row(ATT_WIDTH), row(ATT_WIDTH), row(ATT_WIDTH)],
        out_shape=[jax.ShapeDtypeStruct((t, SSD_INNER), BF16),
                   jax.ShapeDtypeStruct((t, SSD_CONV_DIM), BF16),
                   jax.ShapeDtypeStruct((t, SSD_HEADS), F32),
                   jax.ShapeDtypeStruct((SSD_HEADS, t), F32),
                   jax.ShapeDtypeStruct((t, ATT_WIDTH), BF16),
                   jax.ShapeDtypeStruct((t, ATT_WIDTH), BF16),
                   jax.ShapeDtypeStruct((t, ATT_WIDTH), BF16)],
        compiler_params=_cparams(("parallel",)),
        name="proj_even",
    )(x2, *ws)


def _proj_odd_kernel(x_ref, wcq_ref, wckv_ref, wkpe_ref, wkps_ref, wq_ref, wk_ref, wv_ref, wft_ref, fb_ref,
                     cq_ref, ckv_ref, kpe_ref, kps_ref, q_ref, k_ref, v_ref, fh_ref, carry_ref, *, tiles_per_seq):
    i = pl.program_id(0)
    tm = x_ref.shape[0]
    x = x_ref[...].astype(BF16)
    cq_ref[...] = _dot(x, wcq_ref[...])
    ckv_ref[...] = _dot(x, wckv_ref[...])
    kpe_ref[...] = _dot(x, wkpe_ref[...])
    kps_ref[...] = _dot(x, wkps_ref[...])
    q_ref[...] = _dot(x, wq_ref[...]).astype(q_ref.dtype)
    k_ref[...] = _dot(x, wk_ref[...]).astype(k_ref.dtype)
    v_ref[...] = _dot(x, wv_ref[...]).astype(v_ref.dtype)

    @pl.when(i % tiles_per_seq == 0)
    def _():
        carry_ref[...] = jnp.zeros_like(carry_ref)

    f_raw = _dot_nt(wft_ref[...], x) + fb_ref[...]
    log_f = -_softplus(-f_raw)
    upper = (_iota((tm, tm), 0) <= _iota((tm, tm), 1)).astype(BF16)
    cum = _dot01_right(log_f, upper) + carry_ref[...]
    fh_ref[...] = cum
    carry_ref[...] = cum[:, tm - 1:tm]


def _proj_odd(x2, w_in, f_bias, seq, tm):
    t, d = x2.shape
    sizes = (MLA_Q_RANK, MLA_KV_RANK, MLA_ROPE, ATT_WIDTH, ATT_WIDTH, ATT_WIDTH, ATT_HEADS)
    o = 0
    parts = []
    for n in sizes:
        parts.append(w_in[:, o:o + n])
        o += n
    wcq, wckv, wkpe, wq, wk, wv, wf = parts
    half = MLA_ROPE // 2
    pad_l = jnp.zeros((d, MLA_NOPE), F32)
    pad_r = jnp.zeros((d, LANES - MLA_NOPE - MLA_ROPE), F32)
    wkpe_p = jnp.concatenate([pad_l, wkpe, pad_r], axis=1)
    wkps_p = jnp.concatenate([pad_l, -wkpe[:, half:], wkpe[:, :half], pad_r], axis=1)
    wq = wq * (1.0 / math.sqrt(ATT_HEAD_DIM))
    ws = [wcq.astype(BF16), wckv.astype(BF16), wkpe_p.astype(BF16), wkps_p.astype(BF16),
          wq.astype(BF16), wk.astype(BF16), wv.astype(BF16), wf.T.astype(BF16)]
    fb = f_bias.reshape(ATT_HEADS, 1).astype(F32)
    row = lambda n: pl.BlockSpec((tm, n), lambda i: (i, 0))
    full = lambda a: pl.BlockSpec(a.shape, lambda i: (0, 0))
    return pl.pallas_call(
        functools.partial(_proj_odd_kernel, tiles_per_seq=seq // tm),
        grid=(t // tm,),
        in_specs=[row(d)] + [full(w) for w in ws] + [full(fb)],
        out_specs=[row(MLA_Q_RANK), row(MLA_KV_RANK), row(LANES), row(LANES),
                   row(ATT_WIDTH), row(ATT_WIDTH), row(ATT_WIDTH),
                   pl.BlockSpec((ATT_HEADS, tm), lambda i: (0, i))],
        out_shape=[jax.ShapeDtypeStruct((t, MLA_Q_RANK), F32),
                   jax.ShapeDtypeStruct((t, MLA_KV_RANK), F32),
                   jax.ShapeDtypeStruct((t, LANES), F32),
                   jax.ShapeDtypeStruct((t, LANES), F32),
                   jax.ShapeDtypeStruct((t, ATT_WIDTH), BF16),
                   jax.ShapeDtypeStruct((t, ATT_WIDTH), BF16),
                   jax.ShapeDtypeStruct((t, ATT_WIDTH), BF16),
                   jax.ShapeDtypeStruct((ATT_HEADS, t), F32)],
        scratch_shapes=[pltpu.VMEM((ATT_HEADS, 1), F32)],
        compiler_params=_cparams(("arbitrary",)),
        name="proj_odd",
    )(x2, *ws, fb)


def _ssd_kernel(xbc_ref, z_ref, dt_ref, dtt_ref, cw_ref, cb_ref, dtb_ref, dtbt_ref, alog_ref, alogt_ref,
                dskip_ref, ng_ref, o_ref, buf_ref, st_ref, y_ref):
    c = pl.program_id(1)
    L = SSD_CHUNK

    @pl.when(c == 0)
    def _():
        buf_ref[...] = jnp.zeros_like(buf_ref)
        st_ref[...] = jnp.zeros_like(st_ref)

    cur = xbc_ref[...]
    both = jnp.concatenate([buf_ref[(c + 1) % 2], cur], axis=0)
    shifts = range(1, SSD_CONV)
    sel = jnp.concatenate([(_iota((L, 2 * L), 1) == _iota((L, 2 * L), 0) + (L - k)) for k in shifts],
                          axis=0).astype(BF16)
    shifted = _dot(sel, both)
    acc = cb_ref[...] + cur.astype(F32) * cw_ref[SSD_CONV - 1:SSD_CONV, :]
    for n, k in enumerate(shifts):
        acc = acc + shifted[n * L:(n + 1) * L] * cw_ref[SSD_CONV - 1 - k:SSD_CONV - k, :]
    buf_ref[c % 2] = cur
    xbc = _silu(acc)
    b_mat = xbc[:, SSD_INNER:SSD_INNER + SSD_BC]
    c_mat = xbc[:, SSD_INNER + SSD_BC:]
    bt = jnp.transpose(b_mat).astype(BF16)
    c_bf = c_mat.astype(BF16)

    dt = _softplus(dt_ref[...] + dtb_ref[...])
    dtt = _softplus(dtt_ref[...] + dtbt_ref[...])
    da = dt * (-jnp.exp(alog_ref[...]))
    dat = dtt * (-jnp.exp(alogt_ref[...]))
    row = _iota((L, L), 0)
    col = _iota((L, L), 1)
    causal = col <= row
    a_cum = _dot01_left(causal.astype(BF16), da)
    a_cumt = _dot01_right(dat, (row <= col).astype(BF16))
    a_end = a_cum[L - 1:L, :]
    low_half = _iota((1, LANES), 1) < SSD_HEAD_DIM

    def pair_cols(m, h0):
        n = m.shape[0]
        return jnp.where(low_half, jnp.broadcast_to(m[:, h0:h0 + 1], (n, LANES)),
                         jnp.broadcast_to(m[:, h0 + 1:h0 + 2], (n, LANES)))

    sumsq = jnp.zeros((L, 1), F32)
    for g in range(SSD_GROUPS):
        bt_g = bt[g * SSD_STATE:(g + 1) * SSD_STATE, :]
        c_g = c_bf[:, g * SSD_STATE:(g + 1) * SSD_STATE]
        cb = _dot(c_g, bt_g)
        for pp in range(SSD_HEADS // SSD_GROUPS // 2):
            p = g * (SSD_HEADS // SSD_GROUPS // 2) + pp
            h0 = 2 * p
            lanes = slice(p * LANES, (p + 1) * LANES)
            xs = xbc[:, lanes]
            xs_bf = xs.astype(BF16)
            yd = []
            for h in (h0, h0 + 1):
                seg = a_cum[:, h:h + 1] - a_cumt[h:h + 1, :]
                dec = jnp.where(causal, jnp.exp(jnp.minimum(seg, 0.0)), 0.0)
                m = cb * dec * dtt[h:h + 1, :]
                yd.append(_dot(m.astype(BF16), xs_bf))
            y = jnp.where(low_half, yd[0], yd[1])
            a_p = pair_cols(a_cum, h0)
            dt_p = pair_cols(dt, h0)
            end_p = pair_cols(a_end, h0)
            st = st_ref[p]
            y = y + _dot(c_g, st.astype(BF16)) * jnp.exp(a_p)
            xw = xs * (jnp.exp(end_p - a_p) * dt_p)
            st_ref[p] = st * jnp.exp(end_p) + _dot(bt_g, xw.astype(BF16))
            y = y + dskip_ref[:, lanes] * xs
            y = y * _silu(z_ref[:, lanes].astype(F32))
            y_ref[:, lanes] = y
            sumsq = sumsq + jnp.sum(y * y, axis=1, keepdims=True)
    inv = lax.rsqrt(sumsq * (1.0 / SSD_INNER) + RMS_EPS)
    o_ref[...] = (y_ref[...] * inv * ng_ref[...]).astype(o_ref.dtype)


def _ssd(xbc, z, dt, dtt, conv_w, conv_b, dt_bias, a_log, d_skip, norm_g, batch, seq):
    t = xbc.shape[0]
    L = SSD_CHUNK
    nc = seq // L
    row = lambda n: pl.BlockSpec((L, n), lambda b, c: (b * nc + c, 0))
    full = lambda a: pl.BlockSpec(a.shape, lambda b, c: (0,) * a.ndim)
    params = [conv_w.astype(F32), conv_b.reshape(1, -1).astype(F32),
              dt_bias.reshape(1, -1).astype(F32), dt_bias.reshape(-1, 1).astype(F32),
              a_log.reshape(1, -1).astype(F32), a_log.reshape(-1, 1).astype(F32),
              jnp.repeat(d_skip.astype(F32), SSD_HEAD_DIM).reshape(1, -1),
              norm_g.reshape(1, -1).astype(F32)]
    return pl.pallas_call(
        _ssd_kernel,
        grid=(batch, nc),
        in_specs=[row(SSD_CONV_DIM), row(SSD_INNER), row(SSD_HEADS),
                  pl.BlockSpec((SSD_HEADS, L), lambda b, c: (0, b * nc + c))] + [full(a) for a in params],
        out_specs=row(SSD_INNER),
        out_shape=jax.ShapeDtypeStruct((t, SSD_INNER), BF16),
        scratch_shapes=[pltpu.VMEM((2, L, SSD_CONV_DIM), BF16),
                        pltpu.VMEM((SSD_HEADS // 2, SSD_STATE, LANES), F32),
                        pltpu.VMEM((L, SSD_INNER), F32)],
        compiler_params=_cparams(("parallel", "arbitrary")),
        name="ssd",
    )(xbc, z, dt, dtt, *params)


def _attention_kernel(*refs, mode, tq, tk, diag_split):
    if mode == "fox":
        q_ref, k_ref, v_ref, fh_ref, o_ref, acc_ref, m_ref, l_ref, kn_ref = refs
    else:
        q_ref, k_ref, v_ref, o_ref, acc_ref, m_ref, l_ref = refs
        fh_ref = None
    i = pl.program_id(2)
    n_diag = tq // tk
    lane = _iota((1, LANES), 1)
    low_half = lane < ATT_HEAD_DIM
    row = _iota((tq, tk), 0)
    col = _iota((tq, tk), 1)

    def diag_mask(dd):
        c = col + dd * tk
        if mode == "sb":
            return c < row
        if mode == "mla":
            return (c // MLA_CHUNK) <= (row // MLA_CHUNK)
        return c <= row

    if mode == "sb":
        u_mat = (_iota((tk, tk), 0) > _iota((tk, tk), 1)).astype(BF16)

    q = q_ref[...]
    if mode == "mla":
        qs = [q[:, 0:LANES], q[:, LANES:2 * LANES]]
    else:
        zero = jnp.zeros_like(q)
        qs = [jnp.where(low_half, q, zero), jnp.where(low_half, zero, q)]

    acc_ref[...] = jnp.zeros_like(acc_ref)
    if mode == "sb":
        m_ref[...] = jnp.zeros_like(m_ref)
    else:
        m_ref[...] = jnp.full(m_ref.shape, NEG_BIG, F32)
        l_ref[...] = jnp.zeros_like(l_ref)

    if mode == "fox":
        q0 = pl.multiple_of(i * tq, tq)
        fh_q = [fh_ref[hh:hh + 1, pl.ds(q0, LANES)][:, 0:1] for hh in range(2)]

    def rep(x, width):
        return jnp.concatenate([x] * (width // LANES), axis=1) if width > LANES else x

    def tile(chunks, split):
        rows = tq // split
        groups = [(hh, slice(rb * rows, (rb + 1) * rows), rb) for hh in range(2) for rb in range(split)]
        loaded = []
        for j, dd in chunks:
            k0 = pl.multiple_of(j * tk, tk)
            k_t = k_ref[pl.ds(k0, tk), :]
            k_h = [k_t[:, 0:LANES], k_t[:, LANES:2 * LANES]] if mode == "mla" else [k_t, k_t]
            bias = None
            if mode == "fox":
                bias = [fh_q[hh] - fh_ref[hh:hh + 1, pl.ds(k0, tk)] for hh in range(2)]
            loaded.append((k_h, v_ref[pl.ds(k0, tk), :], bias, None if dd is None else diag_mask(dd)))
        parts = []
        for hh, rs, rb in groups:
            ps_ = []
            for c, (j, dd) in enumerate(chunks):
                nk = tk if dd is None else min(tk, (rb + 1) * rows - dd * tk)
                if nk > 0:
                    ps_.append((c, nk, None if dd is None else loaded[c][3][rs, 0:nk]))
            parts.append(ps_)
        flat = [(g, c, nk, m) for g, ps_ in enumerate(parts) for c, nk, m in ps_]
        zs = [_dot_nt(qs[groups[g][0]][groups[g][1]], loaded[c][0][groups[g][0]][0:nk]) for g, c, nk, m in flat]
        if mode == "sb":
            runs = [m_ref[hh, rs] for hh, rs, _ in groups]
            accs = [acc_ref[hh, rs] for hh, rs, _ in groups]
            sps = [_softplus(z) for z in zs]
            l1ms = [-sp if m is None else jnp.where(m, -sp, 0.0) for sp, (_, _, _, m) in zip(sps, flat)]
            his = [l1m.astype(BF16) for l1m in l1ms]
            los = [(l1m - hi.astype(F32)).astype(BF16) for l1m, hi in zip(l1ms, his)]
            css = [_dot(hi, u_mat[0:nk, 0:nk]) + _dot(lo, u_mat[0:nk, 0:nk])
                   for hi, lo, (_, _, nk, _) in zip(his, los, flat)]
            tots = [cs[:, 0:1] + l1m[:, 0:1] for cs, l1m in zip(css, l1ms)]
            seen = []
            cur = list(runs)
            for n, (g, c, nk, m) in enumerate(flat):
                seen.append(cur[g])
                cur[g] = cur[g] + tots[n]
            ws = [jnp.exp((z - sp) + cs + rep(run, nk))
                  for z, sp, cs, run, (_, _, nk, _) in zip(zs, sps, css, seen, flat)]
            ws = [w if m is None else jnp.where(m, w, 0.0) for w, (_, _, _, m) in zip(ws, flat)]
            pvs = [_dot(w.astype(BF16), loaded[c][1][0:nk]) for w, (_, c, nk, _) in zip(ws, flat)]
            for n, (g, c, nk, m) in enumerate(flat):
                accs[g] = accs[g] + pvs[n]
            for (hh, rs, _), acc, run in zip(groups, accs, cur):
                acc_ref[hh, rs] = acc
                m_ref[hh, rs] = run
        else:
            if mode == "fox":
                zs = [z + loaded[c][2][groups[g][0]][:, 0:nk] for z, (g, c, nk, m) in zip(zs, flat)]
            zs = [z if m is None else jnp.where(m, z, NEG_BIG) for z, (_, _, _, m) in zip(zs, flat)]
            m_olds = [m_ref[hh, rs] for hh, rs, _ in groups]
            l_olds = [l_ref[hh, rs] for hh, rs, _ in groups]
            accs = [acc_ref[hh, rs] for hh, rs, _ in groups]
            m_news = list(m_olds)
            for z, (g, _, _, _) in zip(zs, flat):
                m_news[g] = jnp.maximum(m_news[g], jnp.max(z, axis=1, keepdims=True))
            ps = [jnp.exp(z - rep(m_news[g], nk)) for z, (g, _, nk, _) in zip(zs, flat)]
            alphas = [jnp.exp(m_old - m_new) for m_old, m_new in zip(m_olds, m_news)]
            pvs = [_dot(p.astype(BF16), loaded[c][1][0:nk]) for p, (_, c, nk, _) in zip(ps, flat)]
            l_news = [alpha * l_old for alpha, l_old in zip(alphas, l_olds)]
            acc_news = [alpha * acc for alpha, acc in zip(alphas, accs)]
            for p, pv, (g, _, _, _) in zip(ps, pvs, flat):
                l_news[g] = l_news[g] + jnp.sum(p, axis=1, keepdims=True)
                acc_news[g] = acc_news[g] + pv
            for (hh, rs, _), m_new, l_new, acc in zip(groups, m_news, l_news, acc_news):
                l_ref[hh, rs] = l_new
                acc_ref[hh, rs] = acc
                m_ref[hh, rs] = m_new

    def more(j_done):
        if mode == "sb":
            return (jnp.max(m_ref[...]) > -EXP_ZERO).astype(jnp.int32)
        if mode == "fox":
            kl = pl.multiple_of(jnp.maximum(j_done - 1, 0) * tk, tk)
            best = None
            for hh in range(2):
                fh_last = fh_ref[hh:hh + 1, pl.ds(kl, tk)][:, tk - 1:tk]
                v = jnp.max(qk_bound - m_ref[hh], axis=(0, 1), keepdims=True) + (fh_q[hh] - fh_last)
                best = v if best is None else jnp.maximum(best, v)
            return (jnp.max(best) > -EXP_ZERO).astype(jnp.int32)
        return jnp.int32(1)

    if mode == "fox":
        @pl.when(i == 0)
        def _():
            kf = k_ref[...].astype(F32)
            n2 = jnp.max(jnp.sum(kf * kf, axis=1, keepdims=True), axis=0, keepdims=True)
            kn_ref[...] = jnp.broadcast_to(jnp.sqrt(n2), kn_ref.shape)
        qf = q.astype(F32)
        qk_bound = jnp.sqrt(jnp.sum(qf * qf, axis=1, keepdims=True)) * kn_ref[0:1, :]

    n_off = i * n_diag
    diag_chunks = [(n_off + dd, dd) for dd in reversed(range(n_diag))]

    @pl.when(i == 0)
    def _():
        tile(diag_chunks, diag_split)

    @pl.when(i > 0)
    def _():
        tile(diag_chunks + [(n_off - 1, None)], diag_split)

    def cond(c):
        return jnp.logical_and(c[0] < n_off, c[1] > 0)

    def body(c):
        j = n_off - 1 - c[0]
        tile([(j, None)], 1)
        return c[0] + 1, more(j)
    lax.while_loop(cond, body, (jnp.int32(1), more(jnp.maximum(n_off - 1, 0))))
    if mode == "sb":
        out = [acc_ref[0], acc_ref[1]]
    else:
        out = [acc_ref[hh] / l_ref[hh] for hh in range(2)]
    o_ref[...] = jnp.where(low_half, out[0], out[1]).astype(o_ref.dtype)


def _attention(mode, q, k, v, batch, seq, fh=None):
    t = v.shape[0]
    tq = min(ATT_TQ[mode], seq)
    tk = min(ATT_TK[mode], tq)
    nq = seq // tq
    qk_w = 2 * LANES if mode == "mla" else LANES
    in_specs = [pl.BlockSpec((tq, qk_w), lambda b, h, i: (b * nq + i, h)),
                pl.BlockSpec((seq, qk_w), lambda b, h, i: (b, h)),
                pl.BlockSpec((seq, LANES), lambda b, h, i: (b, h))]
    args = [q, k, v]
    scratch = [pltpu.VMEM((2, tq, LANES), F32), pltpu.VMEM((2, tq, LANES), F32), pltpu.VMEM((2, tq, LANES), F32)]
    if mode == "fox":
        in_specs.append(pl.BlockSpec((None, 2, seq), lambda b, h, i: (h, 0, b)))
        args.append(fh.reshape(HEAD_PAIRS, 2, t))
        scratch.append(pltpu.VMEM((8, LANES), F32))
    return pl.pallas_call(
        functools.partial(_attention_kernel, mode=mode, tq=tq, tk=tk, diag_split=max(1, tq // ATT_DIAG_ROWS)),
        grid=(batch, HEAD_PAIRS, nq),
        in_specs=in_specs,
        out_specs=pl.BlockSpec((tq, LANES), lambda b, h, i: (b * nq + i, h)),
        out_shape=jax.ShapeDtypeStruct((t, ATT_WIDTH), BF16),
        scratch_shapes=scratch,
        compiler_params=_cparams(("parallel", "parallel", "arbitrary")),
        name="attn_" + mode,
    )(*args)


def _mla_prep_kernel(cq_ref, ckv_ref, kpe_ref, kps_ref, cos_ref, sin_ref, qg_ref, kvg_ref,
                     wqm_ref, wqs_ref, wk_ref, wv_ref, q_ref, k_ref, v_ref):
    def rms(x, g):
        return x * lax.rsqrt(jnp.mean(x * x, axis=1, keepdims=True) + RMS_EPS) * g

    cqn = rms(cq_ref[...], qg_ref[...]).astype(BF16)
    ckvn = rms(ckv_ref[...], kvg_ref[...]).astype(BF16)
    cos1 = cos_ref[...]
    sin1 = sin_ref[...]
    cos8 = jnp.concatenate([cos1] * ATT_HEADS, axis=1)
    sin8 = jnp.concatenate([sin1] * ATT_HEADS, axis=1)
    qm = _dot(cqn, wqm_ref[...])
    qsw = _dot(cqn, wqs_ref[...])
    scale = 1.0 / math.sqrt(MLA_NOPE + MLA_ROPE)
    q_ref[...] = ((qm * cos8 + qsw * sin8) * scale).astype(q_ref.dtype)
    k_rope = kpe_ref[...] * cos1 + kps_ref[...] * sin1
    k_ref[...] = (_dot(ckvn, wk_ref[...]) + jnp.concatenate([k_rope] * ATT_HEADS, axis=1)).astype(k_ref.dtype)
    v_ref[...] = _dot(ckvn, wv_ref[...]).astype(v_ref.dtype)


def _mla_prep(cq, ckv, kpe, kps, q_norm_g, w_q_up, kv_norm_g, w_kv_up, seq, tm):
    t = cq.shape[0]
    half = MLA_ROPE // 2
    inv_freq = ROPE_THETA ** (-(jnp.arange(0, MLA_ROPE, 2, dtype=F32) / MLA_ROPE))
    ang = jnp.arange(seq, dtype=F32)[:, None] * inv_freq[None, :]
    cos, sin = jnp.cos(ang), jnp.sin(ang)
    pad = LANES - MLA_NOPE - MLA_ROPE
    cos1 = jnp.concatenate([jnp.ones((seq, MLA_NOPE), F32), cos, cos, jnp.zeros((seq, pad), F32)], axis=1)
    sin1 = jnp.concatenate([jnp.zeros((seq, MLA_NOPE), F32), sin, sin, jnp.zeros((seq, pad), F32)], axis=1)
    wq = w_q_up.reshape(MLA_Q_RANK, ATT_HEADS, MLA_NOPE + MLA_ROPE)
    zq = lambda n: jnp.zeros((MLA_Q_RANK, ATT_HEADS, n), F32)
    wq_main = jnp.concatenate([wq, zq(pad)], axis=2).reshape(MLA_Q_RANK, ATT_HEADS * LANES)
    wq_swap = jnp.concatenate([zq(MLA_NOPE), -wq[:, :, MLA_NOPE + half:], wq[:, :, MLA_NOPE:MLA_NOPE + half],
                               zq(pad)], axis=2).reshape(MLA_Q_RANK, ATT_HEADS * LANES)
    wkv = w_kv_up.reshape(MLA_KV_RANK, ATT_HEADS, MLA_NOPE + MLA_V)
    wk = jnp.concatenate([wkv[:, :, :MLA_NOPE], jnp.zeros((MLA_KV_RANK, ATT_HEADS, LANES - MLA_NOPE), F32)],
                         axis=2).reshape(MLA_KV_RANK, ATT_HEADS * LANES)
    wv = wkv[:, :, MLA_NOPE:].reshape(MLA_KV_RANK, ATT_HEADS * MLA_V)
    consts = [q_norm_g.reshape(1, -1).astype(F32), kv_norm_g.reshape(1, -1).astype(F32),
              wq_main.astype(BF16), wq_swap.astype(BF16), wk.astype(BF16), wv.astype(BF16)]
    ns = seq // tm
    row = lambda n: pl.BlockSpec((tm, n), lambda i: (i, 0))
    tab = pl.BlockSpec((tm, LANES), lambda i: (i % ns, 0))
    full = lambda a: pl.BlockSpec(a.shape, lambda i: (0, 0))
    return pl.pallas_call(
        _mla_prep_kernel,
        grid=(t // tm,),
        in_specs=[row(MLA_Q_RANK), row(MLA_KV_RANK), row(LANES), row(LANES), tab, tab] + [full(a) for a in consts],
        out_specs=[row(ATT_HEADS * LANES), row(ATT_HEADS * LANES), row(ATT_HEADS * MLA_V)],
        out_shape=[jax.ShapeDtypeStruct((t, ATT_HEADS * LANES), BF16),
                   jax.ShapeDtypeStruct((t, ATT_HEADS * LANES), BF16),
                   jax.ShapeDtypeStruct((t, ATT_HEADS * MLA_V), BF16)],
        compiler_params=_cparams(("parallel",)),
        name="mla_prep",
    )(cq, ckv, kpe, kps, cos1, sin1, *consts)


def _layer_norm_rows(r, g, b):
    mu = jnp.mean(r, axis=1, keepdims=True)
    d = r - mu
    var = jnp.mean(d * d, axis=1, keepdims=True)
    return d * lax.rsqrt(var + LN_EPS) * g + b


def _router(x, w_hi, w_lo, bias):
    tm = x.shape[0]
    x_hi = x.astype(BF16)
    x_lo = (x - x_hi.astype(F32)).astype(BF16)
    logits = _dot(x_hi, w_hi) + (_dot(x_hi, w_lo) + _dot(x_lo, w_hi)) + bias
    e_idx = _iota((tm, LANES), 1)
    is_group = jnp.logical_and(e_idx >= MOE_EXPERTS, e_idx < MOE_EXPERTS + MOE_GROUPS)
    gl = jnp.where(is_group, logits, -jnp.inf)
    g_max = jnp.max(gl, axis=1, keepdims=True)
    g_p = 1.0 / jnp.sum(jnp.exp(gl - g_max), axis=1, keepdims=True)
    g_sel = jnp.min(jnp.where(gl == g_max, e_idx, LANES), axis=1, keepdims=True) - MOE_EXPERTS
    in_group = (e_idx // MOE_EPG) == g_sel
    masked = jnp.where(in_group, logits, -jnp.inf)
    m1 = jnp.max(masked, axis=1, keepdims=True)
    i1 = jnp.min(jnp.where(masked == m1, e_idx, LANES), axis=1, keepdims=True)
    masked2 = jnp.where(e_idx == i1, -jnp.inf, masked)
    m2 = jnp.max(masked2, axis=1, keepdims=True)
    i2 = jnp.min(jnp.where(masked2 == m2, e_idx, LANES), axis=1, keepdims=True)
    e2 = jnp.exp(m2 - m1)
    w1 = g_p / (1.0 + e2)
    w2 = w1 * e2
    gate = jnp.where(e_idx == i1, w1, 0.0) + jnp.where(e_idx == i2, w2, 0.0)
    lo = jnp.minimum(i1, i2) - g_sel * MOE_EPG
    hi = jnp.maximum(i1, i2) - g_sel * MOE_EPG
    pair = ((lo * (2 * MOE_EPG - 1 - lo)) >> 1) + (hi - lo - 1)
    return gate, g_sel * MOE_PAIRS + pair


def _outproj_ln_router_kernel(a_ref, b_ref, x_ref, wa_ref, wb_ref, g_ref, beta_ref, wr_hi_ref, wr_lo_ref, br_ref,
                              xg_ref, bucket_ref, rank_ref, count_ref, run_ref):
    i = pl.program_id(0)
    tm, d = x_ref.shape

    @pl.when(i == 0)
    def _():
        run_ref[...] = jnp.zeros_like(run_ref)

    y = _dot(a_ref[...], wa_ref[...]) + _dot(b_ref[...], wb_ref[...])
    x1 = _layer_norm_rows(DEEPNORM_ALPHA * x_ref[...] + y, g_ref[...], beta_ref[...])
    gate, bucket = _router(x1, wr_hi_ref[...], wr_lo_ref[...], br_ref[...])
    xg_ref[:, 0:d] = x1
    xg_ref[:, d:d + LANES] = gate
    bucket_ref[...] = bucket
    onehot = _iota((tm, LANES), 1) == bucket
    tri = (_iota((tm, tm), 1) <= _iota((tm, tm), 0)).astype(BF16)
    prefix = _dot(tri, jnp.where(onehot, 1.0, 0.0).astype(BF16))
    before = run_ref[...]
    rank = jnp.sum(jnp.where(onehot, prefix + before, 0.0), axis=1, keepdims=True) - 1.0
    rank_ref[...] = rank.astype(jnp.int32)
    run_ref[...] = before + prefix[tm - 1:tm, :]
    count_ref[...] = run_ref[...]


def _outproj_ln_router(a, b, x2, w_out, ln_g, ln_b, w_group, b_group, w_expert, b_expert, tm):
    t, d = x2.shape
    ka, kb = a.shape[1], b.shape[1]
    pad = LANES - MOE_EXPERTS - MOE_GROUPS
    w_r = jnp.pad(jnp.concatenate([w_expert, w_group], axis=1).astype(F32), ((0, 0), (0, pad)))
    b_r = jnp.pad(jnp.concatenate([b_expert, b_group]).astype(F32), (0, pad)).reshape(1, LANES)
    w_r_hi = w_r.astype(BF16)
    consts = [w_out[:ka].astype(BF16), w_out[ka:].astype(BF16),
              ln_g.reshape(1, -1).astype(F32), ln_b.reshape(1, -1).astype(F32),
              w_r_hi, (w_r - w_r_hi.astype(F32)).astype(BF16), b_r]
    row = lambda n: pl.BlockSpec((tm, n), lambda i: (i, 0))
    full = lambda c: pl.BlockSpec(c.shape, lambda i: (0, 0))
    return pl.pallas_call(
        _outproj_ln_router_kernel,
        grid=(t // tm,),
        in_specs=[row(ka), row(kb), row(d)] + [full(c) for c in consts],
        out_specs=[row(d + LANES), row(1), row(1), pl.BlockSpec((1, LANES), lambda i: (0, 0))],
        out_shape=[jax.ShapeDtypeStruct((t, d + LANES), F32),
                   jax.ShapeDtypeStruct((t, 1), jnp.int32),
                   jax.ShapeDtypeStruct((t, 1), jnp.int32),
                   jax.ShapeDtypeStruct((1, LANES), F32)],
        scratch_shapes=[pltpu.VMEM((1, LANES), F32)],
        compiler_params=_cparams(("arbitrary",)),
        name="outproj_ln_router",
    )(a, b, x2, *consts)


def _moe_plan(bucket, rank, counts, t):
    counts = counts[0, :MOE_BUCKETS].astype(jnp.int32)
    padded = ((counts + MOE_TILE - 1) // MOE_TILE) * MOE_TILE
    ends = jnp.cumsum(padded)
    starts = ends - padded
    pos = jnp.take(starts, bucket[:, 0]) + rank[:, 0]
    n_tiles = t // MOE_TILE + MOE_BUCKETS
    tile_start = jnp.arange(n_tiles, dtype=jnp.int32) * MOE_TILE
    tile_bucket = jnp.minimum(jnp.sum((tile_start[:, None] >= ends[None, :]).astype(jnp.int32), axis=1),
                              MOE_BUCKETS - 1)
    pairs = [(a, b) for a in range(MOE_EPG) for b in range(a + 1, MOE_EPG)]
    first = jnp.asarray([g * MOE_EPG + a for g in range(MOE_GROUPS) for a, _ in pairs], jnp.int32)
    second = jnp.asarray([g * MOE_EPG + b for g in range(MOE_GROUPS) for _, b in pairs], jnp.int32)
    n_used = (ends[MOE_BUCKETS - 1] // MOE_TILE).reshape(1)
    return pos, jnp.take(first, tile_bucket), jnp.take(second, tile_bucket), n_used, n_tiles


def _moe_scatter_kernel(pos_ref, xg_ref, init_hbm, xs_hbm, sem):
    del init_hbm
    n = pos_ref.shape[0]

    def body(r, c):
        pltpu.make_async_copy(xg_ref.at[pl.ds(r, 1)], xs_hbm.at[pl.ds(pos_ref[r], 1)], sem).start()
        return c
    lax.fori_loop(0, n, body, 0, unroll=DMA_UNROLL)
    pltpu.make_async_copy(xg_ref, xs_hbm.at[pl.ds(0, n)], sem).wait()


def _moe_scatter(xg, pos, n_rows, chunk):
    t, w = xg.shape
    return pl.pallas_call(
        _moe_scatter_kernel,
        grid=(t // chunk,),
        in_specs=[pl.BlockSpec((chunk,), lambda i: (i,), memory_space=pltpu.SMEM),
                  pl.BlockSpec((chunk, w), lambda i: (i, 0)), pl.BlockSpec(memory_space=pl.ANY)],
        out_specs=pl.BlockSpec(memory_space=pl.ANY),
        out_shape=jax.ShapeDtypeStruct((n_rows, w), xg.dtype),
        scratch_shapes=[pltpu.SemaphoreType.DMA(())],
        input_output_aliases={2: 0},
        compiler_params=_cparams(("arbitrary",)),
        name="moe_scatter",
    )(pos, xg, jnp.zeros((n_rows, w), xg.dtype))


def _moe_ffn_kernel(ea_ref, eb_ref, nu_ref, xs_ref, wga_ref, wua_ref, wda_ref, wgb_ref, wub_ref, wdb_ref, y_ref):
    i = pl.program_id(0)
    d = wga_ref.shape[0]

    @pl.when(i < nu_ref[0])
    def _():
        x = xs_ref[:, 0:d].astype(BF16)
        gate = xs_ref[:, d:d + LANES]
        lane = _iota(gate.shape, 1)
        y = None
        for e_ref, wg_ref, wu_ref, wd_ref in ((ea_ref, wga_ref, wua_ref, wda_ref), (eb_ref, wgb_ref, wub_ref, wdb_ref)):
            w = jnp.sum(jnp.where(lane == e_ref[i], gate, 0.0), axis=1, keepdims=True)
            hid = _silu(_dot(x, wg_ref[...])) * _dot(x, wu_ref[...]) * w
            part = _dot(hid.astype(BF16), wd_ref[...])
            y = part if y is None else y + part
        y_ref[...] = y

    @pl.when(i >= nu_ref[0])
    def _():
        y_ref[...] = jnp.zeros_like(y_ref)


def _moe_ffn(xs, ea, eb, n_used, n_tiles, w_gate, w_up, w_down):
    d = w_gate.shape[1]
    wg, wu, wd = w_gate.astype(BF16), w_up.astype(BF16), w_down.astype(BF16)
    up_a = pl.BlockSpec((None, d, MOE_FF), lambda i, ea, eb, nu: (ea[i], 0, 0))
    dn_a = pl.BlockSpec((None, MOE_FF, d), lambda i, ea, eb, nu: (ea[i], 0, 0))
    up_b = pl.BlockSpec((None, d, MOE_FF), lambda i, ea, eb, nu: (eb[i], 0, 0))
    dn_b = pl.BlockSpec((None, MOE_FF, d), lambda i, ea, eb, nu: (eb[i], 0, 0))
    return pl.pallas_call(
        _moe_ffn_kernel,
        grid_spec=pltpu.PrefetchScalarGridSpec(
            num_scalar_prefetch=3,
            grid=(n_tiles,),
            in_specs=[pl.BlockSpec((MOE_TILE, d + LANES), lambda i, ea, eb, nu: (i, 0)),
                      up_a, up_a, dn_a, up_b, up_b, dn_b],
            out_specs=pl.BlockSpec((MOE_TILE, d), lambda i, ea, eb, nu: (i, 0))),
        out_shape=jax.ShapeDtypeStruct((n_tiles * MOE_TILE, d), F32),
        compiler_params=_cparams(("arbitrary",)),
        name="moe_ffn",
    )(ea, eb, n_used, xs, wg, wu, wd, wg, wu, wd)


def _moe_combine_kernel(pos_ref, pos_next_ref, y_hbm, x_ref, g_ref, beta_ref, o_ref, buf_ref, sem):
    i = pl.program_id(0)
    n = pos_ref.shape[0]
    slot = i % 2

    def gather(p_ref, s):
        def body(r, c):
            pltpu.make_async_copy(y_hbm.at[pl.ds(p_ref[r], 1)], buf_ref.at[s, pl.ds(r, 1)], sem.at[s]).start()
            return c
        lax.fori_loop(0, n, body, 0, unroll=DMA_UNROLL)

    @pl.when(i == 0)
    def _():
        gather(pos_ref, slot)

    @pl.when(i + 1 < pl.num_programs(0))
    def _():
        gather(pos_next_ref, 1 - slot)

    pltpu.make_async_copy(y_hbm.at[pl.ds(0, n)], buf_ref.at[slot], sem.at[slot]).wait()
    r = DEEPNORM_ALPHA * x_ref[...] + buf_ref[slot]
    o_ref[...] = _layer_norm_rows(r, g_ref[...], beta_ref[...])


def _moe_combine(y_sorted, pos, xg, ln_g, ln_b, tm):
    t = pos.shape[0]
    d = y_sorted.shape[1]
    ln = [ln_g.reshape(1, -1).astype(F32), ln_b.reshape(1, -1).astype(F32)]
    last = t // tm - 1
    return pl.pallas_call(
        _moe_combine_kernel,
        grid=(t // tm,),
        in_specs=[pl.BlockSpec((tm,), lambda i: (i,), memory_space=pltpu.SMEM),
                  pl.BlockSpec((tm,), lambda i: (jnp.minimum(i + 1, last),), memory_space=pltpu.SMEM),
                  pl.BlockSpec(memory_space=pl.ANY),
                  pl.BlockSpec((tm, d), lambda i: (i, 0))]
        + [pl.BlockSpec(c.shape, lambda i: (0, 0)) for c in ln],
        out_specs=pl.BlockSpec((tm, d), lambda i: (i, 0)),
        out_shape=jax.ShapeDtypeStruct((t, d), F32),
        scratch_shapes=[pltpu.VMEM((2, tm, d), F32), pltpu.SemaphoreType.DMA((2,))],
        compiler_params=_cparams(("arbitrary",)),
        name="moe_combine",
    )(pos, pos, y_sorted, xg, *ln)


def _mixer_out_and_moe(a, b, x2, w_out, ln1_g, ln1_b, w_group, b_group, w_expert, b_expert,
                       w_gate, w_up, w_down, ln2_g, ln2_b, tm):
    t = x2.shape[0]
    xg, bucket, rank, counts = _outproj_ln_router(a, b, x2, w_out, ln1_g, ln1_b, w_group, b_group,
                                                  w_expert, b_expert, tm)
    pos, ea, eb, n_used, n_tiles = _moe_plan(bucket, rank, counts, t)
    xs = _moe_scatter(xg, pos, n_tiles * MOE_TILE, _row_tile(t, 1024))
    y_sorted = _moe_ffn(xs, ea, eb, n_used, n_tiles, w_gate, w_up, w_down)
    return _moe_combine(y_sorted, pos, xg, ln2_g, ln2_b, tm)


def _row_tile(n, pref):
    tm = min(pref, n)
    assert n % tm == 0
    return tm


def kernel(x, ev_w_in, ev_conv_w, ev_conv_b, ev_dt_bias, ev_a_log, ev_d_skip, ev_norm_g, ev_w_out, od_w_in, od_q_norm_g, od_w_q_up, od_kv_norm_g, od_w_kv_up, od_f_bias, od_w_out, ln1_g, ln1_b, ln2_g, ln2_b, moe_w_group, moe_b_group, moe_w_expert, moe_b_expert, moe_w_gate, moe_w_up, moe_w_down):
    batch, seq, d = x.shape
    t = batch * seq
    assert seq % SSD_CHUNK == 0 and seq % LANES == 0 and t % MOE_TILE == 0
    tm_proj = _row_tile(seq, 512)
    x2 = x.reshape(t, d)

    z, xbc, dt, dtt, q, k, v = _proj_even(x2, ev_w_in[0], tm_proj)
    y_ssd = _ssd(xbc, z, dt, dtt, ev_conv_w[0], ev_conv_b[0], ev_dt_bias[0], ev_a_log[0], ev_d_skip[0],
                 ev_norm_g[0], batch, seq)
    y_sb = _attention("sb", q, k, v, batch, seq)
    x2 = _mixer_out_and_moe(y_ssd, y_sb, x2, ev_w_out[0], ln1_g[0], ln1_b[0], moe_w_group[0], moe_b_group[0],
                            moe_w_expert[0], moe_b_expert[0], moe_w_gate[0], moe_w_up[0], moe_w_down[0],
                            ln2_g[0], ln2_b[0], tm_proj)

    cq, ckv, kpe, kps, q, k, v, fh = _proj_odd(x2, od_w_in[0], od_f_bias[0], seq, tm_proj)
    qm, km, vm = _mla_prep(cq, ckv, kpe, kps, od_q_norm_g[0], od_w_q_up[0], od_kv_norm_g[0], od_w_kv_up[0],
                           seq, tm_proj)
    y_mla = _attention("mla", qm, km, vm, batch, seq)
    y_fox = _attention("fox", q, k, v, batch, seq, fh=fh)
    x2 = _mixer_out_and_moe(y_mla, y_fox, x2, od_w_out[0], ln1_g[1], ln1_b[1], moe_w_group[1], moe_b_group[1],
                            moe_w_expert[1], moe_b_expert[1], moe_w_gate[1], moe_w_up[1], moe_w_down[1],
                            ln2_g[1], ln2_b[1], tm_proj)
    return x2.reshape(batch, seq, d)
```

```python
import functools
import math

import jax
import jax.numpy as jnp
from jax import lax
from jax.experimental import pallas as pl
from jax.experimental.pallas import tpu as pltpu

F32 = jnp.float32
BF16 = jnp.bfloat16

SSD_HEADS = 16
SSD_HEAD_DIM = 64
SSD_INNER = SSD_HEADS * SSD_HEAD_DIM
SSD_GROUPS = 2
SSD_STATE = 128
SSD_CONV = 4
SSD_BC = SSD_GROUPS * SSD_STATE
SSD_CONV_DIM = SSD_INNER + 2 * SSD_BC
ATT_HEADS = 8
ATT_HEAD_DIM = 64
ATT_WIDTH = ATT_HEADS * ATT_HEAD_DIM
HEAD_PAIRS = ATT_HEADS // 2
MLA_Q_RANK = 256
MLA_KV_RANK = 128
MLA_NOPE = 64
MLA_ROPE = 32
MLA_V = 64
MLA_CHUNK = 64
ROPE_THETA = 10000.0
MOE_GROUPS = 4
MOE_EPG = 4
MOE_EXPERTS = MOE_GROUPS * MOE_EPG
MOE_FF = 256
DEPTH = 2
DEEPNORM_ALPHA = (2.0 * DEPTH) ** 0.25
LN_EPS = 1e-5
RMS_EPS = 1e-6

LANES = 128
SSD_CHUNK = 128
ATT_TQ = {"sb": 256, "mla": 512, "fox": 256}
ATT_TK = {"sb": 256, "mla": 512, "fox": 256}
ATT_DIAG_ROWS = 128
NEG_BIG = -1e30
EXP_ZERO = 110.0
MOE_TILE = 256
DMA_UNROLL = 8
MOE_PAIRS = MOE_EPG * (MOE_EPG - 1) // 2
MOE_BUCKETS = MOE_GROUPS * MOE_PAIRS
VMEM_LIMIT = 56 * 1024 * 1024


def _cparams(sem):
    return pltpu.CompilerParams(dimension_semantics=sem, vmem_limit_bytes=VMEM_LIMIT)


def _dot(a, b):
    return jnp.dot(a, b, preferred_element_type=F32)


def _dot_nt(a, b):
    return lax.dot_general(a, b, (((1,), (1,)), ((), ())), preferred_element_type=F32)


def _split3(x):
    hi = x.astype(BF16)
    r1 = x - hi.astype(F32)
    mid = r1.astype(BF16)
    lo = (r1 - mid.astype(F32)).astype(BF16)
    return hi, mid, lo


def _dot01_right(x, m01):
    hi, mid, lo = _split3(x)
    return _dot(hi, m01) + _dot(mid, m01) + _dot(lo, m01)


def _dot01_left(m01, x):
    hi, mid, lo = _split3(x)
    return _dot(m01, hi) + _dot(m01, mid) + _dot(m01, lo)


def _softplus(x):
    return jnp.maximum(x, 0.0) + jnp.log(1.0 + jnp.exp(-jnp.abs(x)))


def _silu(x):
    return x * (1.0 / (1.0 + jnp.exp(-x)))


def _iota(shape, dim):
    return lax.broadcasted_iota(jnp.int32, shape, dim)


def _proj_even_kernel(x_ref, wz_ref, wx_ref, wdt_ref, wdtt_ref, wq_ref, wk_ref, wv_ref,
                      z_ref, xbc_ref, dt_ref, dtt_ref, q_ref, k_ref, v_ref):
    x = x_ref[...].astype(BF16)
    z_ref[...] = _dot(x, wz_ref[...]).astype(z_ref.dtype)
    xbc_ref[...] = _dot(x, wx_ref[...]).astype(xbc_ref.dtype)
    dt_ref[...] = _dot(x, wdt_ref[...])
    dtt_ref[...] = _dot_nt(wdtt_ref[...], x)
    q_ref[...] = _dot(x, wq_ref[...]).astype(q_ref.dtype)
    k_ref[...] = _dot(x, wk_ref[...]).astype(k_ref.dtype)
    v_ref[...] = _dot(x, wv_ref[...]).astype(v_ref.dtype)


def _proj_even(x2, w_in, tm):
    t, d = x2.shape
    o = 0
    parts = []
    for n in (SSD_INNER, SSD_CONV_DIM, SSD_HEADS, ATT_WIDTH, ATT_WIDTH, ATT_WIDTH):
        parts.append(w_in[:, o:o + n])
        o += n
    wz, wx, wdt, wq, wk, wv = parts
    wq = wq * (1.0 / math.sqrt(ATT_HEAD_DIM))
    ws = [wz.astype(BF16), wx.astype(BF16), wdt.astype(BF16), wdt.T.astype(BF16),
          wq.astype(BF16), wk.astype(BF16), wv.astype(BF16)]
    row = lambda n: pl.BlockSpec((tm, n), lambda i: (i, 0))
    full = lambda a: pl.BlockSpec(a.shape, lambda i: (0, 0))
    return pl.pallas_call(
        _proj_even_kernel,
        grid=(t // tm,),
        in_specs=[row(d)] + [full(w) for w in ws],
        out_specs=[row(SSD_INNER), row(SSD_CONV_DIM), row(SSD_HEADS),
                   pl.BlockSpec((SSD_HEADS, tm), lambda i: (0, i)),
                   row(ATT_WIDTH), row(ATT_WIDTH), row(ATT_WIDTH)],
        out_shape=[jax.ShapeDtypeStruct((t, SSD_INNER), BF16),
                   jax.ShapeDtypeStruct((t, SSD_CONV_DIM), BF16),
                   jax.ShapeDtypeStruct((t, SSD_HEADS), F32),
                   jax.ShapeDtypeStruct((SSD_HEADS, t), F32),
                   jax.ShapeDtypeStruct((t, ATT_WIDTH), BF16),
                   jax.ShapeDtypeStruct((t, ATT_WIDTH), BF16),
                   jax.ShapeDtypeStruct((t, ATT_WIDTH), BF16)],
        compiler_params=_cparams(("parallel",)),
        name="proj_even",
    )(x2, *ws)


def _mla_rows(cq, ckv, kpe, kps, cos1, sin1, qg, kvg, wqm, wqs, wk, wv):
    def rms(x, g):
        return x * lax.rsqrt(jnp.mean(x * x, axis=1, keepdims=True) + RMS_EPS) * g

    cqn = rms(cq, qg).astype(BF16)
    ckvn = rms(ckv, kvg).astype(BF16)
    cos8 = jnp.concatenate([cos1] * ATT_HEADS, axis=1)
    sin8 = jnp.concatenate([sin1] * ATT_HEADS, axis=1)
    scale = 1.0 / math.sqrt(MLA_NOPE + MLA_ROPE)
    q = (_dot(cqn, wqm) * cos8 + _dot(cqn, wqs) * sin8) * scale
    k_rope = kpe * cos1 + kps * sin1
    k = _dot(ckvn, wk) + jnp.concatenate([k_rope] * ATT_HEADS, axis=1)
    return q, k, _dot(ckvn, wv)


def _proj_odd_kernel(x_ref, wcq_ref, wckv_ref, wkpe_ref, wkps_ref, wq_ref, wk_ref, wv_ref, wft_ref, fb_ref,
                     cos_ref, sin_ref, qg_ref, kvg_ref, wqm_ref, wqs_ref, wkup_ref, wvup_ref,
                     q_ref, k_ref, v_ref, fh_ref, qm_ref, km_ref, vm_ref, carry_ref, *, tiles_per_seq):
    i = pl.program_id(0)
    tm = x_ref.shape[0]
    x = x_ref[...].astype(BF16)
    q_ref[...] = _dot(x, wq_ref[...]).astype(q_ref.dtype)
    k_ref[...] = _dot(x, wk_ref[...]).astype(k_ref.dtype)
    v_ref[...] = _dot(x, wv_ref[...]).astype(v_ref.dtype)
    qm, km, vm = _mla_rows(_dot(x, wcq_ref[...]), _dot(x, wckv_ref[...]), _dot(x, wkpe_ref[...]),
                           _dot(x, wkps_ref[...]), cos_ref[...], sin_ref[...], qg_ref[...], kvg_ref[...],
                           wqm_ref[...], wqs_ref[...], wkup_ref[...], wvup_ref[...])
    qm_ref[...] = qm.astype(qm_ref.dtype)
    km_ref[...] = km.astype(km_ref.dtype)
    vm_ref[...] = vm.astype(vm_ref.dtype)

    @pl.when(i % tiles_per_seq == 0)
    def _():
        carry_ref[...] = jnp.zeros_like(carry_ref)

    f_raw = _dot_nt(wft_ref[...], x) + fb_ref[...]
    log_f = -_softplus(-f_raw)
    upper = (_iota((tm, tm), 0) <= _iota((tm, tm), 1)).astype(BF16)
    cum = _dot01_right(log_f, upper) + carry_ref[...]
    fh_ref[...] = cum
    carry_ref[...] = cum[:, tm - 1:tm]


def _proj_odd(x2, w_in, f_bias, q_norm_g, w_q_up, kv_norm_g, w_kv_up, seq, tm):
    t, d = x2.shape
    sizes = (MLA_Q_RANK, MLA_KV_RANK, MLA_ROPE, ATT_WIDTH, ATT_WIDTH, ATT_WIDTH, ATT_HEADS)
    o = 0
    parts = []
    for n in sizes:
        parts.append(w_in[:, o:o + n])
        o += n
    wcq, wckv, wkpe, wq, wk, wv, wf = parts
    half = MLA_ROPE // 2
    pad = LANES - MLA_NOPE - MLA_ROPE
    pad_l = jnp.zeros((d, MLA_NOPE), F32)
    pad_r = jnp.zeros((d, pad), F32)
    wkpe_p = jnp.concatenate([pad_l, wkpe, pad_r], axis=1)
    wkps_p = jnp.concatenate([pad_l, -wkpe[:, half:], wkpe[:, :half], pad_r], axis=1)
    wq = wq * (1.0 / math.sqrt(ATT_HEAD_DIM))
    ws = [wcq.astype(BF16), wckv.astype(BF16), wkpe_p.astype(BF16), wkps_p.astype(BF16),
          wq.astype(BF16), wk.astype(BF16), wv.astype(BF16), wf.T.astype(BF16)]
    fb = f_bias.reshape(ATT_HEADS, 1).astype(F32)
    inv_freq = ROPE_THETA ** (-(jnp.arange(0, MLA_ROPE, 2, dtype=F32) / MLA_ROPE))
    ang = jnp.arange(seq, dtype=F32)[:, None] * inv_freq[None, :]
    cos, sin = jnp.cos(ang), jnp.sin(ang)
    cos1 = jnp.concatenate([jnp.ones((seq, MLA_NOPE), F32), cos, cos, jnp.zeros((seq, pad), F32)], axis=1)
    sin1 = jnp.concatenate([jnp.zeros((seq, MLA_NOPE), F32), sin, sin, jnp.zeros((seq, pad), F32)], axis=1)
    wqu = w_q_up.reshape(MLA_Q_RANK, ATT_HEADS, MLA_NOPE + MLA_ROPE)
    zq = lambda n: jnp.zeros((MLA_Q_RANK, ATT_HEADS, n), F32)
    wq_main = jnp.concatenate([wqu, zq(pad)], axis=2).reshape(MLA_Q_RANK, ATT_HEADS * LANES)
    wq_swap = jnp.concatenate([zq(MLA_NOPE), -wqu[:, :, MLA_NOPE + half:], wqu[:, :, MLA_NOPE:MLA_NOPE + half],
                               zq(pad)], axis=2).reshape(MLA_Q_RANK, ATT_HEADS * LANES)
    wkv = w_kv_up.reshape(MLA_KV_RANK, ATT_HEADS, MLA_NOPE + MLA_V)
    wk_up = jnp.concatenate([wkv[:, :, :MLA_NOPE], jnp.zeros((MLA_KV_RANK, ATT_HEADS, LANES - MLA_NOPE), F32)],
                            axis=2).reshape(MLA_KV_RANK, ATT_HEADS * LANES)
    wv_up = wkv[:, :, MLA_NOPE:].reshape(MLA_KV_RANK, ATT_HEADS * MLA_V)
    mla = [q_norm_g.reshape(1, -1).astype(F32), kv_norm_g.reshape(1, -1).astype(F32),
           wq_main.astype(BF16), wq_swap.astype(BF16), wk_up.astype(BF16), wv_up.astype(BF16)]
    ns = seq // tm
    row = lambda n: pl.BlockSpec((tm, n), lambda i: (i, 0))
    tab = pl.BlockSpec((tm, LANES), lambda i: (i % ns, 0))
    full = lambda a: pl.BlockSpec(a.shape, lambda i: (0, 0))
    return pl.pallas_call(
        functools.partial(_proj_odd_kernel, tiles_per_seq=ns),
        grid=(t // tm,),
        in_specs=[row(d)] + [full(w) for w in ws] + [full(fb), tab, tab] + [full(a) for a in mla],
        out_specs=[row(ATT_WIDTH), row(ATT_WIDTH), row(ATT_WIDTH),
                   pl.BlockSpec((ATT_HEADS, tm), lambda i: (0, i)),
                   row(ATT_HEADS * LANES), row(ATT_HEADS * LANES), row(ATT_HEADS * MLA_V)],
        out_shape=[jax.ShapeDtypeStruct((t, ATT_WIDTH), BF16),
                   jax.ShapeDtypeStruct((t, ATT_WIDTH), BF16),
                   jax.ShapeDtypeStruct((t, ATT_WIDTH), BF16),
                   jax.ShapeDtypeStruct((ATT_HEADS, t), F32),
                   jax.ShapeDtypeStruct((t, ATT_HEADS * LANES), BF16),
                   jax.ShapeDtypeStruct((t, ATT_HEADS * LANES), BF16),
                   jax.ShapeDtypeStruct((t, ATT_HEADS * MLA_V), BF16)],
        scratch_shapes=[pltpu.VMEM((ATT_HEADS, 1), F32)],
        compiler_params=_cparams(("arbitrary",)),
        name="proj_odd",
    )(x2, *ws, fb, cos1, sin1, *mla)


def _ssd_kernel(xbc_ref, z_ref, dt_ref, dtt_ref, cw_ref, cb_ref, dtb_ref, dtbt_ref, alog_ref, alogt_ref,
                dskip_ref, ng_ref, o_ref, buf_ref, st_ref, y_ref):
    c = pl.program_id(1)
    L = SSD_CHUNK

    @pl.when(c == 0)
    def _():
        buf_ref[...] = jnp.zeros_like(buf_ref)
        st_ref[...] = jnp.zeros_like(st_ref)

    cur = xbc_ref[...]
    both = jnp.concatenate([buf_ref[(c + 1) % 2], cur], axis=0)
    shifts = range(1, SSD_CONV)
    sel = jnp.concatenate([(_iota((L, 2 * L), 1) == _iota((L, 2 * L), 0) + (L - k)) for k in shifts],
                          axis=0).astype(BF16)
    shifted = _dot(sel, both)
    acc = cb_ref[...] + cur.astype(F32) * cw_ref[SSD_CONV - 1:SSD_CONV, :]
    for n, k in enumerate(shifts):
        acc = acc + shifted[n * L:(n + 1) * L] * cw_ref[SSD_CONV - 1 - k:SSD_CONV - k, :]
    buf_ref[c % 2] = cur
    xbc = _silu(acc)
    b_mat = xbc[:, SSD_INNER:SSD_INNER + SSD_BC]
    c_mat = xbc[:, SSD_INNER + SSD_BC:]
    bt = jnp.transpose(b_mat).astype(BF16)
    c_bf = c_mat.astype(BF16)

    dt = _softplus(dt_ref[...] + dtb_ref[...])
    dtt = _softplus(dtt_ref[...] + dtbt_ref[...])
    da = dt * (-jnp.exp(alog_ref[...]))
    dat = dtt * (-jnp.exp(alogt_ref[...]))
    row = _iota((L, L), 0)
    col = _iota((L, L), 1)
    causal = col <= row
    a_cum = _dot01_left(causal.astype(BF16), da)
    a_cumt = _dot01_right(dat, (row <= col).astype(BF16))
    a_end = a_cum[L - 1:L, :]
    low_half = _iota((1, LANES), 1) < SSD_HEAD_DIM

    def pair_cols(m, h0):
        n = m.shape[0]
        return jnp.where(low_half, jnp.broadcast_to(m[:, h0:h0 + 1], (n, LANES)),
                         jnp.broadcast_to(m[:, h0 + 1:h0 + 2], (n, LANES)))

    sumsq = jnp.zeros((L, 1), F32)
    for g in range(SSD_GROUPS):
        bt_g = bt[g * SSD_STATE:(g + 1) * SSD_STATE, :]
        c_g = c_bf[:, g * SSD_STATE:(g + 1) * SSD_STATE]
        cb = _dot(c_g, bt_g)
        for pp in range(SSD_HEADS // SSD_GROUPS // 2):
            p = g * (SSD_HEADS // SSD_GROUPS // 2) + pp
            h0 = 2 * p
            lanes = slice(p * LANES, (p + 1) * LANES)
            xs = xbc[:, lanes]
            xs_bf = xs.astype(BF16)
            yd = []
            for h in (h0, h0 + 1):
                seg = a_cum[:, h:h + 1] - a_cumt[h:h + 1, :]
                dec = jnp.where(causal, jnp.exp(jnp.minimum(seg, 0.0)), 0.0)
                m = cb * dec * dtt[h:h + 1, :]
                yd.append(_dot(m.astype(BF16), xs_bf))
            y = jnp.where(low_half, yd[0], yd[1])
            a_p = pair_cols(a_cum, h0)
            dt_p = pair_cols(dt, h0)
            end_p = pair_cols(a_end, h0)
            st = st_ref[p]
            y = y + _dot(c_g, st.astype(BF16)) * jnp.exp(a_p)
            xw = xs * (jnp.exp(end_p - a_p) * dt_p)
            st_ref[p] = st * jnp.exp(end_p) + _dot(bt_g, xw.astype(BF16))
            y = y + dskip_ref[:, lanes] * xs
            y = y * _silu(z_ref[:, lanes].astype(F32))
            y_ref[:, lanes] = y
            sumsq = sumsq + jnp.sum(y * y, axis=1, keepdims=True)
    inv = lax.rsqrt(sumsq * (1.0 / SSD_INNER) + RMS_EPS)
    o_ref[...] = (y_ref[...] * inv * ng_ref[...]).astype(o_ref.dtype)


def _ssd(xbc, z, dt, dtt, conv_w, conv_b, dt_bias, a_log, d_skip, norm_g, batch, seq):
    t = xbc.shape[0]
    L = SSD_CHUNK
    nc = seq // L
    row = lambda n: pl.BlockSpec((L, n), lambda b, c: (b * nc + c, 0))
    full = lambda a: pl.BlockSpec(a.shape, lambda b, c: (0,) * a.ndim)
    params = [conv_w.astype(F32), conv_b.reshape(1, -1).astype(F32),
              dt_bias.reshape(1, -1).astype(F32), dt_bias.reshape(-1, 1).astype(F32),
              a_log.reshape(1, -1).astype(F32), a_log.reshape(-1, 1).astype(F32),
              jnp.repeat(d_skip.astype(F32), SSD_HEAD_DIM).reshape(1, -1),
              norm_g.reshape(1, -1).astype(F32)]
    return pl.pallas_call(
        _ssd_kernel,
        grid=(batch, nc),
        in_specs=[row(SSD_CONV_DIM), row(SSD_INNER), row(SSD_HEADS),
                  pl.BlockSpec((SSD_HEADS, L), lambda b, c: (0, b * nc + c))] + [full(a) for a in params],
        out_specs=row(SSD_INNER),
        out_shape=jax.ShapeDtypeStruct((t, SSD_INNER), BF16),
        scratch_shapes=[pltpu.VMEM((2, L, SSD_CONV_DIM), BF16),
                        pltpu.VMEM((SSD_HEADS // 2, SSD_STATE, LANES), F32),
                        pltpu.VMEM((L, SSD_INNER), F32)],
        compiler_params=_cparams(("parallel", "arbitrary")),
        name="ssd",
    )(xbc, z, dt, dtt, *params)


def _attention_kernel(*refs, mode, tq, tk, diag_split):
    if mode == "fox":
        q_ref, k_ref, v_ref, fh_ref, o_ref, acc_ref, m_ref, l_ref, kn_ref = refs
    else:
        q_ref, k_ref, v_ref, o_ref, acc_ref, m_ref, l_ref = refs
        fh_ref = None
    i = pl.program_id(2)
    n_diag = tq // tk
    lane = _iota((1, LANES), 1)
    low_half = lane < ATT_HEAD_DIM
    row = _iota((tq, tk), 0)
    col = _iota((tq, tk), 1)

    def diag_mask(dd):
        c = col + dd * tk
        if mode == "sb":
            return c < row
        if mode == "mla":
            return (c // MLA_CHUNK) <= (row // MLA_CHUNK)
        return c <= row

    if mode == "sb":
        u_mat = (_iota((tk, tk), 0) > _iota((tk, tk), 1)).astype(BF16)

    q = q_ref[...]
    if mode == "mla":
        qs = [q[:, 0:LANES], q[:, LANES:2 * LANES]]
    else:
        zero = jnp.zeros_like(q)
        qs = [jnp.where(low_half, q, zero), jnp.where(low_half, zero, q)]

    if mode == "fox":
        q0 = pl.multiple_of(i * tq, tq)
        fh_q = [fh_ref[hh:hh + 1, pl.ds(q0, LANES)][:, 0:1] for hh in range(2)]

    def rep(x, width):
        return jnp.concatenate([x] * (width // LANES), axis=1) if width > LANES else x

    def tile(chunks, split, first):
        rows = tq // split
        groups = [(hh, slice(rb * rows, (rb + 1) * rows), rb) for hh in range(2) for rb in range(split)]
        loaded = []
        for j, dd in chunks:
            k0 = pl.multiple_of(j * tk, tk)
            k_t = k_ref[pl.ds(k0, tk), :]
            k_h = [k_t[:, 0:LANES], k_t[:, LANES:2 * LANES]] if mode == "mla" else [k_t, k_t]
            bias = None
            if mode == "fox":
                bias = [fh_q[hh] - fh_ref[hh:hh + 1, pl.ds(k0, tk)] for hh in range(2)]
            loaded.append((k_h, v_ref[pl.ds(k0, tk), :], bias, None if dd is None else diag_mask(dd)))
        parts = []
        for hh, rs, rb in groups:
            ps_ = []
            for c, (j, dd) in enumerate(chunks):
                nk = tk if dd is None else min(tk, (rb + 1) * rows - dd * tk)
                if nk > 0:
                    ps_.append((c, nk, None if dd is None else loaded[c][3][rs, 0:nk]))
            parts.append(ps_)
        flat = [(g, c, nk, m) for g, ps_ in enumerate(parts) for c, nk, m in ps_]
        zs = [_dot_nt(qs[groups[g][0]][groups[g][1]], loaded[c][0][groups[g][0]][0:nk]) for g, c, nk, m in flat]
        if mode == "sb":
            if first:
                runs = [jnp.zeros((rows, LANES), F32) for _ in groups]
                accs = [None for _ in groups]
            else:
                runs = [m_ref[hh, rs] for hh, rs, _ in groups]
                accs = [acc_ref[hh, rs] for hh, rs, _ in groups]
            sps = [_softplus(z) for z in zs]
            l1ms = [-sp if m is None else jnp.where(m, -sp, 0.0) for sp, (_, _, _, m) in zip(sps, flat)]
            his = [l1m.astype(BF16) for l1m in l1ms]
            los = [(l1m - hi.astype(F32)).astype(BF16) for l1m, hi in zip(l1ms, his)]
            css = [_dot(hi, u_mat[0:nk, 0:nk]) + _dot(lo, u_mat[0:nk, 0:nk])
                   for hi, lo, (_, _, nk, _) in zip(his, los, flat)]
            tots = [cs[:, 0:1] + l1m[:, 0:1] for cs, l1m in zip(css, l1ms)]
            seen = []
            cur = list(runs)
            for n, (g, c, nk, m) in enumerate(flat):
                seen.append(cur[g])
                cur[g] = cur[g] + tots[n]
            ws = [jnp.exp((z - sp) + cs + rep(run, nk))
                  for z, sp, cs, run, (_, _, nk, _) in zip(zs, sps, css, seen, flat)]
            ws = [w if m is None else jnp.where(m, w, 0.0) for w, (_, _, _, m) in zip(ws, flat)]
            pvs = [_dot(w.astype(BF16), loaded[c][1][0:nk]) for w, (_, c, nk, _) in zip(ws, flat)]
            for n, (g, c, nk, m) in enumerate(flat):
                accs[g] = pvs[n] if accs[g] is None else accs[g] + pvs[n]
            for (hh, rs, _), acc, run in zip(groups, accs, cur):
                acc_ref[hh, rs] = acc
                m_ref[hh, rs] = run
        else:
            if mode == "fox":
                zs = [z + loaded[c][2][groups[g][0]][:, 0:nk] for z, (g, c, nk, m) in zip(zs, flat)]
            zs = [z if m is None else jnp.where(m, z, NEG_BIG) for z, (_, _, _, m) in zip(zs, flat)]
            if first:
                m_news = [jnp.full((rows, LANES), NEG_BIG, F32) for _ in groups]
            else:
                m_olds = [m_ref[hh, rs] for hh, rs, _ in groups]
                m_news = list(m_olds)
            for z, (g, _, _, _) in zip(zs, flat):
                m_news[g] = jnp.maximum(m_news[g], jnp.max(z, axis=1, keepdims=True))
            ps = [jnp.exp(z - rep(m_news[g], nk)) for z, (g, _, nk, _) in zip(zs, flat)]
            pvs = [_dot(p.astype(BF16), loaded[c][1][0:nk]) for p, (_, c, nk, _) in zip(ps, flat)]
            if first:
                l_news = [None for _ in groups]
                acc_news = [None for _ in groups]
            else:
                alphas = [jnp.exp(m_old - m_new) for m_old, m_new in zip(m_olds, m_news)]
                l_news = [alpha * l_ref[hh, rs] for alpha, (hh, rs, _) in zip(alphas, groups)]
                acc_news = [alpha * acc_ref[hh, rs] for alpha, (hh, rs, _) in zip(alphas, groups)]
            for p, pv, (g, _, _, _) in zip(ps, pvs, flat):
                ls = jnp.sum(p, axis=1, keepdims=True)
                l_news[g] = jnp.broadcast_to(ls, (rows, LANES)) if l_news[g] is None else l_news[g] + ls
                acc_news[g] = pv if acc_news[g] is None else acc_news[g] + pv
            for (hh, rs, _), m_new, l_new, acc in zip(groups, m_news, l_news, acc_news):
                l_ref[hh, rs] = l_new
                acc_ref[hh, rs] = acc
                m_ref[hh, rs] = m_new

    def more(j_done):
        if mode == "sb":
            return (jnp.max(m_ref[...]) > -EXP_ZERO).astype(jnp.int32)
        if mode == "fox":
            kl = pl.multiple_of(jnp.maximum(j_done - 1, 0) * tk, tk)
            best = None
            for hh in range(2):
                fh_last = fh_ref[hh:hh + 1, pl.ds(kl, tk)][:, tk - 1:tk]
                v = jnp.max(qk_bound - m_ref[hh], axis=(0, 1), keepdims=True) + (fh_q[hh] - fh_last)
                best = v if best is None else jnp.maximum(best, v)
            return (jnp.max(best) > -EXP_ZERO).astype(jnp.int32)
        return jnp.int32(1)

    if mode == "fox":
        @pl.when(i == 0)
        def _():
            kf = k_ref[...].astype(F32)
            n2 = jnp.max(jnp.sum(kf * kf, axis=1, keepdims=True), axis=0, keepdims=True)
            kn_ref[...] = jnp.broadcast_to(jnp.sqrt(n2), kn_ref.shape)
        qf = q.astype(F32)
        qk_bound = jnp.sqrt(jnp.sum(qf * qf, axis=1, keepdims=True)) * kn_ref[0:1, :]

    n_off = i * n_diag
    diag_chunks = [(n_off + dd, dd) for dd in reversed(range(n_diag))]

    @pl.when(i == 0)
    def _():
        tile(diag_chunks, diag_split, True)

    @pl.when(i > 0)
    def _():
        tile(diag_chunks + [(n_off - 1, None)], diag_split, True)

    def cond(c):
        return jnp.logical_and(c[0] < n_off, c[1] > 0)

    def body(c):
        j = n_off - 1 - c[0]
        tile([(j, None)], 1, False)
        return c[0] + 1, more(j)
    lax.while_loop(cond, body, (jnp.int32(1), more(jnp.maximum(n_off - 1, 0))))
    if mode == "sb":
        out = [acc_ref[0], acc_ref[1]]
    else:
        out = [acc_ref[hh] / l_ref[hh] for hh in range(2)]
    o_ref[...] = jnp.where(low_half, out[0], out[1]).astype(o_ref.dtype)


def _attention(mode, q, k, v, batch, seq, fh=None):
    t = v.shape[0]
    tq = min(ATT_TQ[mode], seq)
    tk = min(ATT_TK[mode], tq)
    nq = seq // tq
    qk_w = 2 * LANES if mode == "mla" else LANES
    in_specs = [pl.BlockSpec((tq, qk_w), lambda b, h, i: (b * nq + i, h)),
                pl.BlockSpec((seq, qk_w), lambda b, h, i: (b, h)),
                pl.BlockSpec((seq, LANES), lambda b, h, i: (b, h))]
    args = [q, k, v]
    scratch = [pltpu.VMEM((2, tq, LANES), F32), pltpu.VMEM((2, tq, LANES), F32), pltpu.VMEM((2, tq, LANES), F32)]
    if mode == "fox":
        in_specs.append(pl.BlockSpec((None, 2, seq), lambda b, h, i: (h, 0, b)))
        args.append(fh.reshape(HEAD_PAIRS, 2, t))
        scratch.append(pltpu.VMEM((8, LANES), F32))
    return pl.pallas_call(
        functools.partial(_attention_kernel, mode=mode, tq=tq, tk=tk, diag_split=max(1, tq // ATT_DIAG_ROWS)),
        grid=(batch, HEAD_PAIRS, nq),
        in_specs=in_specs,
        out_specs=pl.BlockSpec((tq, LANES), lambda b, h, i: (b * nq + i, h)),
        out_shape=jax.ShapeDtypeStruct((t, ATT_WIDTH), BF16),
        scratch_shapes=scratch,
        compiler_params=_cparams(("parallel", "parallel", "arbitrary")),
        name="attn_" + mode,
    )(*args)


def _layer_norm_rows(r, g, b):
    mu = jnp.mean(r, axis=1, keepdims=True)
    d = r - mu
    var = jnp.mean(d * d, axis=1, keepdims=True)
    return d * lax.rsqrt(var + LN_EPS) * g + b


def _router(x, w_hi, w_lo, bias):
    tm = x.shape[0]
    x_hi = x.astype(BF16)
    x_lo = (x - x_hi.astype(F32)).astype(BF16)
    logits = _dot(x_hi, w_hi) + (_dot(x_hi, w_lo) + _dot(x_lo, w_hi)) + bias
    e_idx = _iota((tm, LANES), 1)
    is_group = jnp.logical_and(e_idx >= MOE_EXPERTS, e_idx < MOE_EXPERTS + MOE_GROUPS)
    gl = jnp.where(is_group, logits, -jnp.inf)
    g_max = jnp.max(gl, axis=1, keepdims=True)
    g_p = 1.0 / jnp.sum(jnp.exp(gl - g_max), axis=1, keepdims=True)
    g_sel = jnp.min(jnp.where(gl == g_max, e_idx, LANES), axis=1, keepdims=True) - MOE_EXPERTS
    in_group = (e_idx // MOE_EPG) == g_sel
    masked = jnp.where(in_group, logits, -jnp.inf)
    m1 = jnp.max(masked, axis=1, keepdims=True)
    i1 = jnp.min(jnp.where(masked == m1, e_idx, LANES), axis=1, keepdims=True)
    masked2 = jnp.where(e_idx == i1, -jnp.inf, masked)
    m2 = jnp.max(masked2, axis=1, keepdims=True)
    i2 = jnp.min(jnp.where(masked2 == m2, e_idx, LANES), axis=1, keepdims=True)
    e2 = jnp.exp(m2 - m1)
    w1 = g_p / (1.0 + e2)
    w2 = w1 * e2
    gate = jnp.where(e_idx == i1, w1, 0.0) + jnp.where(e_idx == i2, w2, 0.0)
    lo = jnp.minimum(i1, i2) - g_sel * MOE_EPG
    hi = jnp.maximum(i1, i2) - g_sel * MOE_EPG
    pair = ((lo * (2 * MOE_EPG - 1 - lo)) >> 1) + (hi - lo - 1)
    return gate, g_sel * MOE_PAIRS + pair


def _outproj_ln_router_kernel(a_ref, b_ref, x_ref, wa_ref, wb_ref, g_ref, beta_ref, wr_hi_ref, wr_lo_ref, br_ref,
                              xg_ref, bucket_ref, rank_ref, count_ref, run_ref):
    i = pl.program_id(0)
    tm, d = x_ref.shape

    @pl.when(i == 0)
    def _():
        run_ref[...] = jnp.zeros_like(run_ref)

    y = _dot(a_ref[...], wa_ref[...]) + _dot(b_ref[...], wb_ref[...])
    x1 = _layer_norm_rows(DEEPNORM_ALPHA * x_ref[...] + y, g_ref[...], beta_ref[...])
    gate, bucket = _router(x1, wr_hi_ref[...], wr_lo_ref[...], br_ref[...])
    xg_ref[:, 0:d] = x1
    xg_ref[:, d:d + LANES] = gate
    bucket_ref[...] = bucket
    onehot = _iota((tm, LANES), 1) == bucket
    tri = (_iota((tm, tm), 1) <= _iota((tm, tm), 0)).astype(BF16)
    prefix = _dot(tri, jnp.where(onehot, 1.0, 0.0).astype(BF16))
    before = run_ref[...]
    rank = jnp.sum(jnp.where(onehot, prefix + before, 0.0), axis=1, keepdims=True) - 1.0
    rank_ref[...] = rank.astype(jnp.int32)
    run_ref[...] = before + prefix[tm - 1:tm, :]
    count_ref[...] = run_ref[...]


def _outproj_ln_router(a, b, x2, w_out, ln_g, ln_b, w_group, b_group, w_expert, b_expert, tm):
    t, d = x2.shape
    ka, kb = a.shape[1], b.shape[1]
    pad = LANES - MOE_EXPERTS - MOE_GROUPS
    w_r = jnp.pad(jnp.concatenate([w_expert, w_group], axis=1).astype(F32), ((0, 0), (0, pad)))
    b_r = jnp.pad(jnp.concatenate([b_expert, b_group]).astype(F32), (0, pad)).reshape(1, LANES)
    w_r_hi = w_r.astype(BF16)
    consts = [w_out[:ka].astype(BF16), w_out[ka:].astype(BF16),
              ln_g.reshape(1, -1).astype(F32), ln_b.reshape(1, -1).astype(F32),
              w_r_hi, (w_r - w_r_hi.astype(F32)).astype(BF16), b_r]
    row = lambda n: pl.BlockSpec((tm, n), lambda i: (i, 0))
    full = lambda c: pl.BlockSpec(c.shape, lambda i: (0, 0))
    return pl.pallas_call(
        _outproj_ln_router_kernel,
        grid=(t // tm,),
        in_specs=[row(ka), row(kb), row(d)] + [full(c) for c in consts],
        out_specs=[row(d + LANES), row(1), row(1), pl.BlockSpec((1, LANES), lambda i: (0, 0))],
        out_shape=[jax.ShapeDtypeStruct((t, d + LANES), F32),
                   jax.ShapeDtypeStruct((t, 1), jnp.int32),
                   jax.ShapeDtypeStruct((t, 1), jnp.int32),
                   jax.ShapeDtypeStruct((1, LANES), F32)],
        scratch_shapes=[pltpu.VMEM((1, LANES), F32)],
        compiler_params=_cparams(("arbitrary",)),
        name="outproj_ln_router",
    )(a, b, x2, *consts)


def _moe_plan(bucket, rank, counts, t):
    counts = counts[0, :MOE_BUCKETS].astype(jnp.int32)
    padded = ((counts + MOE_TILE - 1) // MOE_TILE) * MOE_TILE
    ends = jnp.cumsum(padded)
    starts = ends - padded
    pos = jnp.take(starts, bucket[:, 0]) + rank[:, 0]
    n_tiles = t // MOE_TILE + MOE_BUCKETS
    tile_start = jnp.arange(n_tiles, dtype=jnp.int32) * MOE_TILE
    tile_bucket = jnp.minimum(jnp.sum((tile_start[:, None] >= ends[None, :]).astype(jnp.int32), axis=1),
                              MOE_BUCKETS - 1)
    pairs = [(a, b) for a in range(MOE_EPG) for b in range(a + 1, MOE_EPG)]
    first = jnp.asarray([g * MOE_EPG + a for g in range(MOE_GROUPS) for a, _ in pairs], jnp.int32)
    second = jnp.asarray([g * MOE_EPG + b for g in range(MOE_GROUPS) for _, b in pairs], jnp.int32)
    n_used = (ends[MOE_BUCKETS - 1] // MOE_TILE).reshape(1)
    return pos, ends, jnp.take(first, tile_bucket), jnp.take(second, tile_bucket), n_used, n_tiles


def _moe_scatter_kernel(ends_ref, pos_ref, xg_ref, xs_hbm, zero_ref, sem, zero_sem):
    n = pos_ref.shape[0]

    @pl.when(pl.program_id(0) == 0)
    def _():
        zero_ref[...] = jnp.zeros_like(zero_ref)
        total = ends_ref[MOE_BUCKETS - 1]

        def zero_copy(row0):
            return pltpu.make_async_copy(zero_ref, xs_hbm.at[pl.ds(pl.multiple_of(row0, MOE_TILE), MOE_TILE)],
                                         zero_sem)

        jobs = [(ends_ref[b] >= MOE_TILE, ends_ref[b] - MOE_TILE) for b in range(MOE_BUCKETS)]
        jobs += [(total + u * MOE_TILE < xs_hbm.shape[0], total + u * MOE_TILE) for u in range(MOE_BUCKETS)]
        for wanted, row0 in jobs:
            @pl.when(wanted)
            def _():
                zero_copy(row0).start()
        for wanted, row0 in jobs:
            @pl.when(wanted)
            def _():
                zero_copy(row0).wait()

    def body(r, c):
        pltpu.make_async_copy(xg_ref.at[pl.ds(r, 1)], xs_hbm.at[pl.ds(pos_ref[r], 1)], sem).start()
        return c
    lax.fori_loop(0, n, body, 0, unroll=DMA_UNROLL)
    pltpu.make_async_copy(xg_ref, xs_hbm.at[pl.ds(0, n)], sem).wait()


def _moe_scatter(xg, pos, ends, n_rows, chunk):
    t, w = xg.shape
    return pl.pallas_call(
        _moe_scatter_kernel,
        grid=(t // chunk,),
        in_specs=[pl.BlockSpec(memory_space=pltpu.SMEM),
                  pl.BlockSpec((chunk,), lambda i: (i,), memory_space=pltpu.SMEM),
                  pl.BlockSpec((chunk, w), lambda i: (i, 0))],
        out_specs=pl.BlockSpec(memory_space=pl.ANY),
        out_shape=jax.ShapeDtypeStruct((n_rows, w), xg.dtype),
        scratch_shapes=[pltpu.VMEM((MOE_TILE, w), xg.dtype), pltpu.SemaphoreType.DMA(()),
                        pltpu.SemaphoreType.DMA(())],
        compiler_params=_cparams(("arbitrary",)),
        name="moe_scatter",
    )(ends, pos, xg)


def _moe_ffn_kernel(ea_ref, eb_ref, nu_ref, xs_ref, wga_ref, wua_ref, wda_ref, wgb_ref, wub_ref, wdb_ref, y_ref):
    i = pl.program_id(0)
    d = wga_ref.shape[0]

    @pl.when(i < nu_ref[0])
    def _():
        x = xs_ref[:, 0:d].astype(BF16)
        gate = xs_ref[:, d:d + LANES]
        lane = _iota(gate.shape, 1)
        y = None
        for e_ref, wg_ref, wu_ref, wd_ref in ((ea_ref, wga_ref, wua_ref, wda_ref), (eb_ref, wgb_ref, wub_ref, wdb_ref)):
            w = jnp.sum(jnp.where(lane == e_ref[i], gate, 0.0), axis=1, keepdims=True)
            hid = _silu(_dot(x, wg_ref[...])) * _dot(x, wu_ref[...]) * w
            part = _dot(hid.astype(BF16), wd_ref[...])
            y = part if y is None else y + part
        y_ref[...] = y

    @pl.when(i >= nu_ref[0])
    def _():
        y_ref[...] = jnp.zeros_like(y_ref)


def _moe_ffn(xs, ea, eb, n_used, n_tiles, w_gate, w_up, w_down):
    d = w_gate.shape[1]
    wg, wu, wd = w_gate.astype(BF16), w_up.astype(BF16), w_down.astype(BF16)
    up_a = pl.BlockSpec((None, d, MOE_FF), lambda i, ea, eb, nu: (ea[i], 0, 0))
    dn_a = pl.BlockSpec((None, MOE_FF, d), lambda i, ea, eb, nu: (ea[i], 0, 0))
    up_b = pl.BlockSpec((None, d, MOE_FF), lambda i, ea, eb, nu: (eb[i], 0, 0))
    dn_b = pl.BlockSpec((None, MOE_FF, d), lambda i, ea, eb, nu: (eb[i], 0, 0))
    return pl.pallas_call(
        _moe_ffn_kernel,
        grid_spec=pltpu.PrefetchScalarGridSpec(
            num_scalar_prefetch=3,
            grid=(n_tiles,),
            in_specs=[pl.BlockSpec((MOE_TILE, d + LANES), lambda i, ea, eb, nu: (jnp.minimum(i, nu[0] - 1), 0)),
                      up_a, up_a, dn_a, up_b, up_b, dn_b],
            out_specs=pl.BlockSpec((MOE_TILE, d), lambda i, ea, eb, nu: (i, 0))),
        out_shape=jax.ShapeDtypeStruct((n_tiles * MOE_TILE, d), F32),
        compiler_params=_cparams(("arbitrary",)),
        name="moe_ffn",
    )(ea, eb, n_used, xs, wg, wu, wd, wg, wu, wd)


def _moe_combine_kernel(pos_ref, pos_next_ref, y_hbm, x_ref, g_ref, beta_ref, o_ref, buf_ref, sem):
    i = pl.program_id(0)
    n = pos_ref.shape[0]
    slot = i % 2

    def gather(p_ref, s):
        def body(r, c):
            pltpu.make_async_copy(y_hbm.at[pl.ds(p_ref[r], 1)], buf_ref.at[s, pl.ds(r, 1)], sem.at[s]).start()
            return c
        lax.fori_loop(0, n, body, 0, unroll=DMA_UNROLL)

    @pl.when(i == 0)
    def _():
        gather(pos_ref, slot)

    @pl.when(i + 1 < pl.num_programs(0))
    def _():
        gather(pos_next_ref, 1 - slot)

    pltpu.make_async_copy(y_hbm.at[pl.ds(0, n)], buf_ref.at[slot], sem.at[slot]).wait()
    r = DEEPNORM_ALPHA * x_ref[...] + buf_ref[slot]
    o_ref[...] = _layer_norm_rows(r, g_ref[...], beta_ref[...])


def _moe_combine(y_sorted, pos, xg, ln_g, ln_b, tm):
    t = pos.shape[0]
    d = y_sorted.shape[1]
    ln = [ln_g.reshape(1, -1).astype(F32), ln_b.reshape(1, -1).astype(F32)]
    last = t // tm - 1
    return pl.pallas_call(
        _moe_combine_kernel,
        grid=(t // tm,),
        in_specs=[pl.BlockSpec((tm,), lambda i: (i,), memory_space=pltpu.SMEM),
                  pl.BlockSpec((tm,), lambda i: (jnp.minimum(i + 1, last),), memory_space=pltpu.SMEM),
                  pl.BlockSpec(memory_space=pl.ANY),
                  pl.BlockSpec((tm, d), lambda i: (i, 0))]
        + [pl.BlockSpec(c.shape, lambda i: (0, 0)) for c in ln],
        out_specs=pl.BlockSpec((tm, d), lambda i: (i, 0)),
        out_shape=jax.ShapeDtypeStruct((t, d), F32),
        scratch_shapes=[pltpu.VMEM((2, tm, d), F32), pltpu.SemaphoreType.DMA((2,))],
        compiler_params=_cparams(("arbitrary",)),
        name="moe_combine",
    )(pos, pos, y_sorted, xg, *ln)


def _mixer_out_and_moe(a, b, x2, w_out, ln1_g, ln1_b, w_group, b_group, w_expert, b_expert,
                       w_gate, w_up, w_down, ln2_g, ln2_b, tm):
    t = x2.shape[0]
    xg, bucket, rank, counts = _outproj_ln_router(a, b, x2, w_out, ln1_g, ln1_b, w_group, b_group,
                                                  w_expert, b_expert, tm)
    pos, ends, ea, eb, n_used, n_tiles = _moe_plan(bucket, rank, counts, t)
    xs = _moe_scatter(xg, pos, ends, n_tiles * MOE_TILE, _row_tile(t, 1024))
    y_sorted = _moe_ffn(xs, ea, eb, n_used, n_tiles, w_gate, w_up, w_down)
    return _moe_combine(y_sorted, pos, xg, ln2_g, ln2_b, tm)


def _row_tile(n, pref):
    tm = min(pref, n)
    assert n % tm == 0
    return tm


def kernel(x, ev_w_in, ev_conv_w, ev_conv_b, ev_dt_bias, ev_a_log, ev_d_skip, ev_norm_g, ev_w_out, od_w_in, od_q_norm_g, od_w_q_up, od_kv_norm_g, od_w_kv_up, od_f_bias, od_w_out, ln1_g, ln1_b, ln2_g, ln2_b, moe_w_group, moe_b_group, moe_w_expert, moe_b_expert, moe_w_gate, moe_w_up, moe_w_down):
    batch, seq, d = x.shape
    t = batch * seq
    assert seq % SSD_CHUNK == 0 and seq % LANES == 0 and t % MOE_TILE == 0
    tm_proj = _row_tile(seq, 512)
    x2 = x.reshape(t, d)

    z, xbc, dt, dtt, q, k, v = _proj_even(x2, ev_w_in[0], tm_proj)
    y_ssd = _ssd(xbc, z, dt, dtt, ev_conv_w[0], ev_conv_b[0], ev_dt_bias[0], ev_a_log[0], ev_d_skip[0],
                 ev_norm_g[0], batch, seq)
    y_sb = _attention("sb", q, k, v, batch, seq)
    x2 = _mixer_out_and_moe(y_ssd, y_sb, x2, ev_w_out[0], ln1_g[0], ln1_b[0], moe_w_group[0], moe_b_group[0],
                            moe_w_expert[0], moe_b_expert[0], moe_w_gate[0], moe_w_up[0], moe_w_down[0],
                            ln2_g[0], ln2_b[0], tm_proj)

    q, k, v, fh, qm, km, vm = _proj_odd(x2, od_w_in[0], od_f_bias[0], od_q_norm_g[0], od_w_q_up[0],
                                        od_kv_norm_g[0], od_w_kv_up[0], seq, tm_proj)
    y_mla = _attention("mla", qm, km, vm, batch, seq)
    y_fox = _attention("fox", q, k, v, batch, seq, fh=fh)
    x2 = _mixer_out_and_moe(y_mla, y_fox, x2, od_w_out[0], ln1_g[1], ln1_b[1], moe_w_group[1], moe_b_group[1],
                            moe_w_expert[1], moe_b_expert[1], moe_w_gate[1], moe_w_up[1], moe_w_down[1],
                            ln2_g[1], ln2_b[1], tm_proj)
    return x2.reshape(batch, seq, d)
```

```python
import functools
import math

import jax
import jax.numpy as jnp
from jax import lax
from jax.experimental import pallas as pl
from jax.experimental.pallas import tpu as pltpu

F32 = jnp.float32
BF16 = jnp.bfloat16

SSD_HEADS = 16
SSD_HEAD_DIM = 64
SSD_INNER = SSD_HEADS * SSD_HEAD_DIM
SSD_GROUPS = 2
SSD_STATE = 128
SSD_CONV = 4
SSD_BC = SSD_GROUPS * SSD_STATE
SSD_CONV_DIM = SSD_INNER + 2 * SSD_BC
ATT_HEADS = 8
ATT_HEAD_DIM = 64
ATT_WIDTH = ATT_HEADS * ATT_HEAD_DIM
HEAD_PAIRS = ATT_HEADS // 2
MLA_Q_RANK = 256
MLA_KV_RANK = 128
MLA_NOPE = 64
MLA_ROPE = 32
MLA_V = 64
MLA_CHUNK = 64
ROPE_THETA = 10000.0
MOE_GROUPS = 4
MOE_EPG = 4
MOE_EXPERTS = MOE_GROUPS * MOE_EPG
MOE_FF = 256
DEPTH = 2
DEEPNORM_ALPHA = (2.0 * DEPTH) ** 0.25
LN_EPS = 1e-5
RMS_EPS = 1e-6

LANES = 128
SSD_CHUNK = 128
ATT_TQ = {"sb": 256, "mla": 512, "fox": 256}
ATT_TK = {"sb": 256, "mla": 512, "fox": 256}
ATT_DIAG_ROWS = 128
NEG_BIG = -1e30
EXP_ZERO = 110.0
MOE_TILE = 256
DMA_UNROLL = 8
MOE_PAIRS = MOE_EPG * (MOE_EPG - 1) // 2
MOE_BUCKETS = MOE_GROUPS * MOE_PAIRS
VMEM_LIMIT = 56 * 1024 * 1024


def _cparams(sem):
    return pltpu.CompilerParams(dimension_semantics=sem, vmem_limit_bytes=VMEM_LIMIT)


def _dot(a, b):
    return jnp.dot(a, b, preferred_element_type=F32)


def _dot_nt(a, b):
    return lax.dot_general(a, b, (((1,), (1,)), ((), ())), preferred_element_type=F32)


def _split3(x):
    hi = x.astype(BF16)
    r1 = x - hi.astype(F32)
    mid = r1.astype(BF16)
    lo = (r1 - mid.astype(F32)).astype(BF16)
    return hi, mid, lo


def _dot01_right(x, m01):
    hi, mid, lo = _split3(x)
    return _dot(hi, m01) + _dot(mid, m01) + _dot(lo, m01)


def _dot01_left(m01, x):
    hi, mid, lo = _split3(x)
    return _dot(m01, hi) + _dot(m01, mid) + _dot(m01, lo)


def _softplus(x):
    return jnp.maximum(x, 0.0) + jnp.log(1.0 + jnp.exp(-jnp.abs(x)))


def _silu(x):
    return x * (1.0 / (1.0 + jnp.exp(-x)))


def _iota(shape, dim):
    return lax.broadcasted_iota(jnp.int32, shape, dim)


def _proj_even_kernel(x_ref, wz_ref, wx_ref, wdt_ref, wdtt_ref, wq_ref, wk_ref, wv_ref,
                      z_ref, xbc_ref, dt_ref, dtt_ref, q_ref, k_ref, v_ref):
    x = x_ref[...].astype(BF16)
    z_ref[...] = _dot(x, wz_ref[...]).astype(z_ref.dtype)
    xbc_ref[...] = _dot(x, wx_ref[...]).astype(xbc_ref.dtype)
    dt_ref[...] = _dot(x, wdt_ref[...])
    dtt_ref[...] = _dot_nt(wdtt_ref[...], x)
    q_ref[...] = _dot(x, wq_ref[...]).astype(q_ref.dtype)
    k_ref[...] = _dot(x, wk_ref[...]).astype(k_ref.dtype)
    v_ref[...] = _dot(x, wv_ref[...]).astype(v_ref.dtype)


def _proj_even(x2, w_in, tm):
    t, d = x2.shape
    o = 0
    parts = []
    for n in (SSD_INNER, SSD_CONV_DIM, SSD_HEADS, ATT_WIDTH, ATT_WIDTH, ATT_WIDTH):
        parts.append(w_in[:, o:o + n])
        o += n
    wz, wx, wdt, wq, wk, wv = parts
    wq = wq * (1.0 / math.sqrt(ATT_HEAD_DIM))
    ws = [wz.astype(BF16), wx.astype(BF16), wdt.astype(BF16), wdt.T.astype(BF16),
          wq.astype(BF16), wk.astype(BF16), wv.astype(BF16)]
    row = lambda n: pl.BlockSpec((tm, n), lambda i: (i, 0))
    full = lambda a: pl.BlockSpec(a.shape, lambda i: (0, 0))
    return pl.pallas_call(
        _proj_even_kernel,
        grid=(t // tm,),
        in_specs=[row(d)] + [full(w) for w in ws],
        out_specs=[row(SSD_INNER), row(SSD_CONV_DIM), row(SSD_HEADS),
                   pl.BlockSpec((SSD_HEADS, tm), lambda i: (0, i)),
                   row(ATT_WIDTH), row(ATT_WIDTH), row(ATT_WIDTH)],
        out_shape=[jax.ShapeDtypeStruct((t, SSD_INNER), BF16),
                   jax.ShapeDtypeStruct((t, SSD_CONV_DIM), BF16),
                   jax.ShapeDtypeStruct((t, SSD_HEADS), F32),
                   jax.ShapeDtypeStruct((SSD_HEADS, t), F32),
                   jax.ShapeDtypeStruct((t, ATT_WIDTH), BF16),
                   jax.ShapeDtypeStruct((t, ATT_WIDTH), BF16),
                   jax.ShapeDtypeStruct((t, ATT_WIDTH), BF16)],
        compiler_params=_cparams(("parallel",)),
        name="proj_even",
    )(x2, *ws)


def _mla_rows(cq, ckv, kpe, kps, cos1, sin1, qg, kvg, wqm, wqs, wk, wv):
    def rms(x, g):
        return x * lax.rsqrt(jnp.mean(x * x, axis=1, keepdims=True) + RMS_EPS) * g

    cqn = rms(cq, qg).astype(BF16)
    ckvn = rms(ckv, kvg).astype(BF16)
    cos8 = jnp.concatenate([cos1] * ATT_HEADS, axis=1)
    sin8 = jnp.concatenate([sin1] * ATT_HEADS, axis=1)
    scale = 1.0 / math.sqrt(MLA_NOPE + MLA_ROPE)
    q = (_dot(cqn, wqm) * cos8 + _dot(cqn, wqs) * sin8) * scale
    k_rope = kpe * cos1 + kps * sin1
    k = _dot(ckvn, wk) + jnp.concatenate([k_rope] * ATT_HEADS, axis=1)
    return q, k, _dot(ckvn, wv)


def _proj_odd_kernel(x_ref, wcq_ref, wckv_ref, wkpe_ref, wkps_ref, wq_ref, wk_ref, wv_ref, wft_ref, fb_ref,
                     cos_ref, sin_ref, qg_ref, kvg_ref, wqm_ref, wqs_ref, wkup_ref, wvup_ref, upper_ref,
                     q_ref, k_ref, v_ref, fh_ref, qm_ref, km_ref, vm_ref, carry_ref, *, tiles_per_seq):
    i = pl.program_id(0)
    tm = x_ref.shape[0]
    x = x_ref[...].astype(BF16)
    q_ref[...] = _dot(x, wq_ref[...]).astype(q_ref.dtype)
    k_ref[...] = _dot(x, wk_ref[...]).astype(k_ref.dtype)
    v_ref[...] = _dot(x, wv_ref[...]).astype(v_ref.dtype)
    qm, km, vm = _mla_rows(_dot(x, wcq_ref[...]), _dot(x, wckv_ref[...]), _dot(x, wkpe_ref[...]),
                           _dot(x, wkps_ref[...]), cos_ref[...], sin_ref[...], qg_ref[...], kvg_ref[...],
                           wqm_ref[...], wqs_ref[...], wkup_ref[...], wvup_ref[...])
    qm_ref[...] = qm.astype(qm_ref.dtype)
    km_ref[...] = km.astype(km_ref.dtype)
    vm_ref[...] = vm.astype(vm_ref.dtype)

    @pl.when(i % tiles_per_seq == 0)
    def _():
        carry_ref[...] = jnp.zeros_like(carry_ref)

    f_raw = _dot_nt(wft_ref[...], x) + fb_ref[...]
    log_f = -_softplus(-f_raw)
    cum = _dot01_right(log_f, upper_ref[...]) + carry_ref[...]
    fh_ref[...] = cum
    carry_ref[...] = cum[:, tm - 1:tm]


def _proj_odd(x2, w_in, f_bias, q_norm_g, w_q_up, kv_norm_g, w_kv_up, seq, tm):
    t, d = x2.shape
    sizes = (MLA_Q_RANK, MLA_KV_RANK, MLA_ROPE, ATT_WIDTH, ATT_WIDTH, ATT_WIDTH, ATT_HEADS)
    o = 0
    parts = []
    for n in sizes:
        parts.append(w_in[:, o:o + n])
        o += n
    wcq, wckv, wkpe, wq, wk, wv, wf = parts
    half = MLA_ROPE // 2
    pad = LANES - MLA_NOPE - MLA_ROPE
    pad_l = jnp.zeros((d, MLA_NOPE), F32)
    pad_r = jnp.zeros((d, pad), F32)
    wkpe_p = jnp.concatenate([pad_l, wkpe, pad_r], axis=1)
    wkps_p = jnp.concatenate([pad_l, -wkpe[:, half:], wkpe[:, :half], pad_r], axis=1)
    wq = wq * (1.0 / math.sqrt(ATT_HEAD_DIM))
    ws = [wcq.astype(BF16), wckv.astype(BF16), wkpe_p.astype(BF16), wkps_p.astype(BF16),
          wq.astype(BF16), wk.astype(BF16), wv.astype(BF16), wf.T.astype(BF16)]
    fb = f_bias.reshape(ATT_HEADS, 1).astype(F32)
    inv_freq = ROPE_THETA ** (-(jnp.arange(0, MLA_ROPE, 2, dtype=F32) / MLA_ROPE))
    ang = jnp.arange(seq, dtype=F32)[:, None] * inv_freq[None, :]
    cos, sin = jnp.cos(ang), jnp.sin(ang)
    cos1 = jnp.concatenate([jnp.ones((seq, MLA_NOPE), F32), cos, cos, jnp.zeros((seq, pad), F32)], axis=1)
    sin1 = jnp.concatenate([jnp.zeros((seq, MLA_NOPE), F32), sin, sin, jnp.zeros((seq, pad), F32)], axis=1)
    wqu = w_q_up.reshape(MLA_Q_RANK, ATT_HEADS, MLA_NOPE + MLA_ROPE)
    zq = lambda n: jnp.zeros((MLA_Q_RANK, ATT_HEADS, n), F32)
    wq_main = jnp.concatenate([wqu, zq(pad)], axis=2).reshape(MLA_Q_RANK, ATT_HEADS * LANES)
    wq_swap = jnp.concatenate([zq(MLA_NOPE), -wqu[:, :, MLA_NOPE + half:], wqu[:, :, MLA_NOPE:MLA_NOPE + half],
                               zq(pad)], axis=2).reshape(MLA_Q_RANK, ATT_HEADS * LANES)
    wkv = w_kv_up.reshape(MLA_KV_RANK, ATT_HEADS, MLA_NOPE + MLA_V)
    wk_up = jnp.concatenate([wkv[:, :, :MLA_NOPE], jnp.zeros((MLA_KV_RANK, ATT_HEADS, LANES - MLA_NOPE), F32)],
                            axis=2).reshape(MLA_KV_RANK, ATT_HEADS * LANES)
    wv_up = wkv[:, :, MLA_NOPE:].reshape(MLA_KV_RANK, ATT_HEADS * MLA_V)
    mla = [q_norm_g.reshape(1, -1).astype(F32), kv_norm_g.reshape(1, -1).astype(F32),
           wq_main.astype(BF16), wq_swap.astype(BF16), wk_up.astype(BF16), wv_up.astype(BF16),
           jnp.triu(jnp.ones((tm, tm), BF16))]
    ns = seq // tm
    row = lambda n: pl.BlockSpec((tm, n), lambda i: (i, 0))
    tab = pl.BlockSpec((tm, LANES), lambda i: (i % ns, 0))
    full = lambda a: pl.BlockSpec(a.shape, lambda i: (0, 0))
    return pl.pallas_call(
        functools.partial(_proj_odd_kernel, tiles_per_seq=ns),
        grid=(t // tm,),
        in_specs=[row(d)] + [full(w) for w in ws] + [full(fb), tab, tab] + [full(a) for a in mla],
        out_specs=[row(ATT_WIDTH), row(ATT_WIDTH), row(ATT_WIDTH),
                   pl.BlockSpec((ATT_HEADS, tm), lambda i: (0, i)),
                   row(ATT_HEADS * LANES), row(ATT_HEADS * LANES), row(ATT_HEADS * MLA_V)],
        out_shape=[jax.ShapeDtypeStruct((t, ATT_WIDTH), BF16),
                   jax.ShapeDtypeStruct((t, ATT_WIDTH), BF16),
                   jax.ShapeDtypeStruct((t, ATT_WIDTH), BF16),
                   jax.ShapeDtypeStruct((ATT_HEADS, t), F32),
                   jax.ShapeDtypeStruct((t, ATT_HEADS * LANES), BF16),
                   jax.ShapeDtypeStruct((t, ATT_HEADS * LANES), BF16),
                   jax.ShapeDtypeStruct((t, ATT_HEADS * MLA_V), BF16)],
        scratch_shapes=[pltpu.VMEM((ATT_HEADS, 1), F32)],
        compiler_params=_cparams(("arbitrary",)),
        name="proj_odd",
    )(x2, *ws, fb, cos1, sin1, *mla)


def _ssd_kernel(xbc_ref, z_ref, dt_ref, dtt_ref, cw_ref, cb_ref, dtb_ref, dtbt_ref, alog_ref, alogt_ref,
                dskip_ref, ng_ref, o_ref, buf_ref, st_ref, y_ref):
    c = pl.program_id(1)
    L = SSD_CHUNK

    @pl.when(c == 0)
    def _():
        buf_ref[...] = jnp.zeros_like(buf_ref)
        st_ref[...] = jnp.zeros_like(st_ref)

    cur = xbc_ref[...]
    both = jnp.concatenate([buf_ref[(c + 1) % 2], cur], axis=0)
    shifts = range(1, SSD_CONV)
    sel = jnp.concatenate([(_iota((L, 2 * L), 1) == _iota((L, 2 * L), 0) + (L - k)) for k in shifts],
                          axis=0).astype(BF16)
    shifted = _dot(sel, both)
    acc = cb_ref[...] + cur.astype(F32) * cw_ref[SSD_CONV - 1:SSD_CONV, :]
    for n, k in enumerate(shifts):
        acc = acc + shifted[n * L:(n + 1) * L] * cw_ref[SSD_CONV - 1 - k:SSD_CONV - k, :]
    buf_ref[c % 2] = cur
    xbc = _silu(acc)
    b_mat = xbc[:, SSD_INNER:SSD_INNER + SSD_BC]
    c_mat = xbc[:, SSD_INNER + SSD_BC:]
    bt = jnp.transpose(b_mat).astype(BF16)
    c_bf = c_mat.astype(BF16)

    dt = _softplus(dt_ref[...] + dtb_ref[...])
    dtt = _softplus(dtt_ref[...] + dtbt_ref[...])
    da = dt * (-jnp.exp(alog_ref[...]))
    dat = dtt * (-jnp.exp(alogt_ref[...]))
    row = _iota((L, L), 0)
    col = _iota((L, L), 1)
    causal = col <= row
    a_cum = _dot01_left(causal.astype(BF16), da)
    a_cumt = _dot01_right(dat, (row <= col).astype(BF16))
    a_end = a_cum[L - 1:L, :]
    low_half = _iota((1, LANES), 1) < SSD_HEAD_DIM

    def pair_cols(m, h0):
        n = m.shape[0]
        return jnp.where(low_half, jnp.broadcast_to(m[:, h0:h0 + 1], (n, LANES)),
                         jnp.broadcast_to(m[:, h0 + 1:h0 + 2], (n, LANES)))

    sumsq = jnp.zeros((L, 1), F32)
    for g in range(SSD_GROUPS):
        bt_g = bt[g * SSD_STATE:(g + 1) * SSD_STATE, :]
        c_g = c_bf[:, g * SSD_STATE:(g + 1) * SSD_STATE]
        cb = _dot(c_g, bt_g)
        for pp in range(SSD_HEADS // SSD_GROUPS // 2):
            p = g * (SSD_HEADS // SSD_GROUPS // 2) + pp
            h0 = 2 * p
            lanes = slice(p * LANES, (p + 1) * LANES)
            xs = xbc[:, lanes]
            xs_bf = xs.astype(BF16)
            yd = []
            for h in (h0, h0 + 1):
                seg = a_cum[:, h:h + 1] - a_cumt[h:h + 1, :]
                dec = jnp.where(causal, jnp.exp(jnp.minimum(seg, 0.0)), 0.0)
                m = cb * dec * dtt[h:h + 1, :]
                yd.append(_dot(m.astype(BF16), xs_bf))
            y = jnp.where(low_half, yd[0], yd[1])
            a_p = pair_cols(a_cum, h0)
            dt_p = pair_cols(dt, h0)
            end_p = pair_cols(a_end, h0)
            st = st_ref[p]
            y = y + _dot(c_g, st.astype(BF16)) * jnp.exp(a_p)
            xw = xs * (jnp.exp(end_p - a_p) * dt_p)
            st_ref[p] = st * jnp.exp(end_p) + _dot(bt_g, xw.astype(BF16))
            y = y + dskip_ref[:, lanes] * xs
            y = y * _silu(z_ref[:, lanes].astype(F32))
            y_ref[:, lanes] = y
            sumsq = sumsq + jnp.sum(y * y, axis=1, keepdims=True)
    inv = lax.rsqrt(sumsq * (1.0 / SSD_INNER) + RMS_EPS)
    o_ref[...] = (y_ref[...] * inv * ng_ref[...]).astype(o_ref.dtype)


def _ssd(xbc, z, dt, dtt, conv_w, conv_b, dt_bias, a_log, d_skip, norm_g, batch, seq):
    t = xbc.shape[0]
    L = SSD_CHUNK
    nc = seq // L
    row = lambda n: pl.BlockSpec((L, n), lambda b, c: (b * nc + c, 0))
    full = lambda a: pl.BlockSpec(a.shape, lambda b, c: (0,) * a.ndim)
    params = [conv_w.astype(F32), conv_b.reshape(1, -1).astype(F32),
              dt_bias.reshape(1, -1).astype(F32), dt_bias.reshape(-1, 1).astype(F32),
              a_log.reshape(1, -1).astype(F32), a_log.reshape(-1, 1).astype(F32),
              jnp.repeat(d_skip.astype(F32), SSD_HEAD_DIM).reshape(1, -1),
              norm_g.reshape(1, -1).astype(F32)]
    return pl.pallas_call(
        _ssd_kernel,
        grid=(batch, nc),
        in_specs=[row(SSD_CONV_DIM), row(SSD_INNER), row(SSD_HEADS),
                  pl.BlockSpec((SSD_HEADS, L), lambda b, c: (0, b * nc + c))] + [full(a) for a in params],
        out_specs=row(SSD_INNER),
        out_shape=jax.ShapeDtypeStruct((t, SSD_INNER), BF16),
        scratch_shapes=[pltpu.VMEM((2, L, SSD_CONV_DIM), BF16),
                        pltpu.VMEM((SSD_HEADS // 2, SSD_STATE, LANES), F32),
                        pltpu.VMEM((L, SSD_INNER), F32)],
        compiler_params=_cparams(("parallel", "arbitrary")),
        name="ssd",
    )(xbc, z, dt, dtt, *params)


def _attention_block(i, k_norm, refs, mode, tq, tk, diag_split):
    if mode == "fox":
        q_ref, k_ref, v_ref, fh_ref, o_ref, acc_ref, m_ref, l_ref = refs
    else:
        q_ref, k_ref, v_ref, o_ref, acc_ref, m_ref, l_ref = refs
        fh_ref = None
    n_diag = tq // tk
    lane = _iota((1, LANES), 1)
    low_half = lane < ATT_HEAD_DIM
    row = _iota((tq, tk), 0)
    col = _iota((tq, tk), 1)

    def diag_mask(dd):
        c = col + dd * tk
        if mode == "sb":
            return c < row
        if mode == "mla":
            return (c // MLA_CHUNK) <= (row // MLA_CHUNK)
        return c <= row

    if mode == "sb":
        u_mat = (_iota((tk, tk), 0) > _iota((tk, tk), 1)).astype(BF16)

    q_rows = pl.ds(pl.multiple_of(i * tq, tq), tq)
    q = q_ref[q_rows, :]
    if mode == "mla":
        qs = [q[:, 0:LANES], q[:, LANES:2 * LANES]]
    else:
        zero = jnp.zeros_like(q)
        qs = [jnp.where(low_half, q, zero), jnp.where(low_half, zero, q)]

    if mode == "fox":
        fh_q = [fh_ref[hh:hh + 1, pl.ds(pl.multiple_of(i * tq, tq), LANES)][:, 0:1] for hh in range(2)]

    def rep(x, width):
        return jnp.concatenate([x] * (width // LANES), axis=1) if width > LANES else x

    def tile(chunks, split, first):
        rows = tq // split
        groups = [(hh, slice(rb * rows, (rb + 1) * rows), rb) for hh in range(2) for rb in range(split)]
        loaded = []
        for j, dd in chunks:
            k0 = pl.multiple_of(j * tk, tk)
            k_t = k_ref[pl.ds(k0, tk), :]
            k_h = [k_t[:, 0:LANES], k_t[:, LANES:2 * LANES]] if mode == "mla" else [k_t, k_t]
            bias = None
            if mode == "fox":
                bias = [fh_q[hh] - fh_ref[hh:hh + 1, pl.ds(k0, tk)] for hh in range(2)]
            loaded.append((k_h, v_ref[pl.ds(k0, tk), :], bias, None if dd is None else diag_mask(dd)))
        parts = []
        for hh, rs, rb in groups:
            ps_ = []
            for c, (j, dd) in enumerate(chunks):
                nk = tk if dd is None else min(tk, (rb + 1) * rows - dd * tk)
                if nk > 0:
                    ps_.append((c, nk, None if dd is None else loaded[c][3][rs, 0:nk]))
            parts.append(ps_)
        flat = [(g, c, nk, m) for g, ps_ in enumerate(parts) for c, nk, m in ps_]
        zs = [_dot_nt(qs[groups[g][0]][groups[g][1]], loaded[c][0][groups[g][0]][0:nk]) for g, c, nk, m in flat]
        if mode == "sb":
            if first:
                runs = [jnp.zeros((rows, LANES), F32) for _ in groups]
                accs = [None for _ in groups]
            else:
                runs = [m_ref[hh, rs] for hh, rs, _ in groups]
                accs = [acc_ref[hh, rs] for hh, rs, _ in groups]
            sps = [_softplus(z) for z in zs]
            l1ms = [-sp if m is None else jnp.where(m, -sp, 0.0) for sp, (_, _, _, m) in zip(sps, flat)]
            his = [l1m.astype(BF16) for l1m in l1ms]
            los = [(l1m - hi.astype(F32)).astype(BF16) for l1m, hi in zip(l1ms, his)]
            css = [_dot(hi, u_mat[0:nk, 0:nk]) + _dot(lo, u_mat[0:nk, 0:nk])
                   for hi, lo, (_, _, nk, _) in zip(his, los, flat)]
            tots = [cs[:, 0:1] + l1m[:, 0:1] for cs, l1m in zip(css, l1ms)]
            seen = []
            cur = list(runs)
            for n, (g, c, nk, m) in enumerate(flat):
                seen.append(cur[g])
                cur[g] = cur[g] + tots[n]
            ws = [jnp.exp((z - sp) + cs + rep(run, nk))
                  for z, sp, cs, run, (_, _, nk, _) in zip(zs, sps, css, seen, flat)]
            ws = [w if m is None else jnp.where(m, w, 0.0) for w, (_, _, _, m) in zip(ws, flat)]
            pvs = [_dot(w.astype(BF16), loaded[c][1][0:nk]) for w, (_, c, nk, _) in zip(ws, flat)]
            for n, (g, c, nk, m) in enumerate(flat):
                accs[g] = pvs[n] if accs[g] is None else accs[g] + pvs[n]
            for (hh, rs, _), acc, run in zip(groups, accs, cur):
                acc_ref[hh, rs] = acc
                m_ref[hh, rs] = run
        else:
            if mode == "fox":
                zs = [z + loaded[c][2][groups[g][0]][:, 0:nk] for z, (g, c, nk, m) in zip(zs, flat)]
            zs = [z if m is None else jnp.where(m, z, NEG_BIG) for z, (_, _, _, m) in zip(zs, flat)]
            if first:
                m_news = [jnp.full((rows, LANES), NEG_BIG, F32) for _ in groups]
            else:
                m_olds = [m_ref[hh, rs] for hh, rs, _ in groups]
                m_news = list(m_olds)
            for z, (g, _, _, _) in zip(zs, flat):
                m_news[g] = jnp.maximum(m_news[g], jnp.max(z, axis=1, keepdims=True))
            ps = [jnp.exp(z - rep(m_news[g], nk)) for z, (g, _, nk, _) in zip(zs, flat)]
            pvs = [_dot(p.astype(BF16), loaded[c][1][0:nk]) for p, (_, c, nk, _) in zip(ps, flat)]
            if first:
                l_news = [None for _ in groups]
                acc_news = [None for _ in groups]
            else:
                alphas = [jnp.exp(m_old - m_new) for m_old, m_new in zip(m_olds, m_news)]
                l_news = [alpha * l_ref[hh, rs] for alpha, (hh, rs, _) in zip(alphas, groups)]
                acc_news = [alpha * acc_ref[hh, rs] for alpha, (hh, rs, _) in zip(alphas, groups)]
            for p, pv, (g, _, _, _) in zip(ps, pvs, flat):
                ls = jnp.sum(p, axis=1, keepdims=True)
                l_news[g] = jnp.broadcast_to(ls, (rows, LANES)) if l_news[g] is None else l_news[g] + ls
                acc_news[g] = pv if acc_news[g] is None else acc_news[g] + pv
            for (hh, rs, _), m_new, l_new, acc in zip(groups, m_news, l_news, acc_news):
                l_ref[hh, rs] = l_new
                acc_ref[hh, rs] = acc
                m_ref[hh, rs] = m_new

    def more(j_done):
        if mode == "sb":
            return (jnp.max(m_ref[...]) > -EXP_ZERO).astype(jnp.int32)
        if mode == "fox":
            kl = pl.multiple_of(jnp.maximum(j_done - 1, 0) * tk, tk)
            best = None
            for hh in range(2):
                fh_last = fh_ref[hh:hh + 1, pl.ds(kl, tk)][:, tk - 1:tk]
                v = jnp.max(qk_bound - m_ref[hh], axis=(0, 1), keepdims=True) + (fh_q[hh] - fh_last)
                best = v if best is None else jnp.maximum(best, v)
            return (jnp.max(best) > -EXP_ZERO).astype(jnp.int32)
        return jnp.int32(1)

    if mode == "fox":
        qf = q.astype(F32)
        qk_bound = jnp.sqrt(jnp.sum(qf * qf, axis=1, keepdims=True)) * k_norm

    n_off = i * n_diag
    diag_chunks = [(n_off + dd, dd) for dd in reversed(range(n_diag))]

    @pl.when(i == 0)
    def _():
        tile(diag_chunks, diag_split, True)

    @pl.when(i > 0)
    def _():
        tile(diag_chunks + [(n_off - 1, None)], diag_split, True)

    def cond(c):
        return jnp.logical_and(c[0] < n_off, c[1] > 0)

    def body(c):
        j = n_off - 1 - c[0]
        tile([(j, None)], 1, False)
        return c[0] + 1, more(j)
    lax.while_loop(cond, body, (jnp.int32(1), more(jnp.maximum(n_off - 1, 0))))
    if mode == "sb":
        out = [acc_ref[0], acc_ref[1]]
    else:
        out = [acc_ref[hh] / l_ref[hh] for hh in range(2)]
    o_ref[q_rows, :] = jnp.where(low_half, out[0], out[1]).astype(o_ref.dtype)


def _attention_kernel(*refs, mode, tq, tk, diag_split, nq):
    k_norm = None
    if mode == "fox":
        kf = refs[1][...].astype(F32)
        n2 = jnp.max(jnp.sum(kf * kf, axis=1, keepdims=True), axis=0, keepdims=True)
        k_norm = jnp.broadcast_to(jnp.sqrt(n2), (1, LANES))

    def q_block(i, carry):
        _attention_block(i, k_norm, refs, mode, tq, tk, diag_split)
        return carry
    lax.fori_loop(0, nq, q_block, 0)


def _attention(mode, q, k, v, batch, seq, fh=None):
    t = v.shape[0]
    tq = min(ATT_TQ[mode], seq)
    tk = min(ATT_TK[mode], tq)
    nq = seq // tq
    qk_w = 2 * LANES if mode == "mla" else LANES
    in_specs = [pl.BlockSpec((seq, qk_w), lambda b, h: (b, h)),
                pl.BlockSpec((seq, qk_w), lambda b, h: (b, h)),
                pl.BlockSpec((seq, LANES), lambda b, h: (b, h))]
    args = [q, k, v]
    scratch = [pltpu.VMEM((2, tq, LANES), F32), pltpu.VMEM((2, tq, LANES), F32), pltpu.VMEM((2, tq, LANES), F32)]
    if mode == "fox":
        in_specs.append(pl.BlockSpec((None, 2, seq), lambda b, h: (h, 0, b)))
        args.append(fh.reshape(HEAD_PAIRS, 2, t))
    return pl.pallas_call(
        functools.partial(_attention_kernel, mode=mode, tq=tq, tk=tk, diag_split=max(1, tq // ATT_DIAG_ROWS), nq=nq),
        grid=(batch, HEAD_PAIRS),
        in_specs=in_specs,
        out_specs=pl.BlockSpec((seq, LANES), lambda b, h: (b, h)),
        out_shape=jax.ShapeDtypeStruct((t, ATT_WIDTH), BF16),
        scratch_shapes=scratch,
        compiler_params=_cparams(("parallel", "parallel")),
        name="attn_" + mode,
    )(*args)


def _layer_norm_rows(r, g, b):
    mu = jnp.mean(r, axis=1, keepdims=True)
    d = r - mu
    var = jnp.mean(d * d, axis=1, keepdims=True)
    return d * lax.rsqrt(var + LN_EPS) * g + b


def _router(x, w_hi, w_lo, bias):
    tm = x.shape[0]
    x_hi = x.astype(BF16)
    x_lo = (x - x_hi.astype(F32)).astype(BF16)
    logits = _dot(x_hi, w_hi) + (_dot(x_hi, w_lo) + _dot(x_lo, w_hi)) + bias
    e_idx = _iota((tm, LANES), 1)
    is_group = jnp.logical_and(e_idx >= MOE_EXPERTS, e_idx < MOE_EXPERTS + MOE_GROUPS)
    gl = jnp.where(is_group, logits, -jnp.inf)
    g_max = jnp.max(gl, axis=1, keepdims=True)
    g_p = 1.0 / jnp.sum(jnp.exp(gl - g_max), axis=1, keepdims=True)
    g_sel = jnp.min(jnp.where(gl == g_max, e_idx, LANES), axis=1, keepdims=True) - MOE_EXPERTS
    in_group = (e_idx // MOE_EPG) == g_sel
    masked = jnp.where(in_group, logits, -jnp.inf)
    m1 = jnp.max(masked, axis=1, keepdims=True)
    i1 = jnp.min(jnp.where(masked == m1, e_idx, LANES), axis=1, keepdims=True)
    masked2 = jnp.where(e_idx == i1, -jnp.inf, masked)
    m2 = jnp.max(masked2, axis=1, keepdims=True)
    i2 = jnp.min(jnp.where(masked2 == m2, e_idx, LANES), axis=1, keepdims=True)
    e2 = jnp.exp(m2 - m1)
    w1 = g_p / (1.0 + e2)
    w2 = w1 * e2
    gate = jnp.where(e_idx == i1, w1, 0.0) + jnp.where(e_idx == i2, w2, 0.0)
    lo = jnp.minimum(i1, i2) - g_sel * MOE_EPG
    hi = jnp.maximum(i1, i2) - g_sel * MOE_EPG
    pair = ((lo * (2 * MOE_EPG - 1 - lo)) >> 1) + (hi - lo - 1)
    return gate, g_sel * MOE_PAIRS + pair


def _outproj_ln_router_kernel(a_ref, b_ref, x_ref, wa_ref, wb_ref, g_ref, beta_ref, wr_hi_ref, wr_lo_ref, br_ref,
                              tri_ref, xg_ref, bucket_ref, rank_ref, count_ref, run_ref):
    i = pl.program_id(0)
    tm, d = x_ref.shape

    @pl.when(i == 0)
    def _():
        run_ref[...] = jnp.zeros_like(run_ref)

    y = _dot(a_ref[...], wa_ref[...]) + _dot(b_ref[...], wb_ref[...])
    x1 = _layer_norm_rows(DEEPNORM_ALPHA * x_ref[...] + y, g_ref[...], beta_ref[...])
    gate, bucket = _router(x1, wr_hi_ref[...], wr_lo_ref[...], br_ref[...])
    xg_ref[:, 0:d] = x1
    xg_ref[:, d:d + LANES] = gate
    bucket_ref[...] = bucket
    onehot = _iota((tm, LANES), 1) == bucket
    prefix = _dot(tri_ref[...], jnp.where(onehot, 1.0, 0.0).astype(BF16))
    before = run_ref[...]
    rank = jnp.sum(jnp.where(onehot, prefix + before, 0.0), axis=1, keepdims=True) - 1.0
    rank_ref[...] = rank.astype(jnp.int32)
    run_ref[...] = before + prefix[tm - 1:tm, :]
    count_ref[...] = run_ref[...]


def _outproj_ln_router(a, b, x2, w_out, ln_g, ln_b, w_group, b_group, w_expert, b_expert, tm):
    t, d = x2.shape
    ka, kb = a.shape[1], b.shape[1]
    pad = LANES - MOE_EXPERTS - MOE_GROUPS
    w_r = jnp.pad(jnp.concatenate([w_expert, w_group], axis=1).astype(F32), ((0, 0), (0, pad)))
    b_r = jnp.pad(jnp.concatenate([b_expert, b_group]).astype(F32), (0, pad)).reshape(1, LANES)
    w_r_hi = w_r.astype(BF16)
    consts = [w_out[:ka].astype(BF16), w_out[ka:].astype(BF16),
              ln_g.reshape(1, -1).astype(F32), ln_b.reshape(1, -1).astype(F32),
              w_r_hi, (w_r - w_r_hi.astype(F32)).astype(BF16), b_r,
              jnp.tril(jnp.ones((tm, tm), BF16))]
    row = lambda n: pl.BlockSpec((tm, n), lambda i: (i, 0))
    full = lambda c: pl.BlockSpec(c.shape, lambda i: (0, 0))
    return pl.pallas_call(
        _outproj_ln_router_kernel,
        grid=(t // tm,),
        in_specs=[row(ka), row(kb), row(d)] + [full(c) for c in consts],
        out_specs=[row(d + LANES), row(1), row(1), pl.BlockSpec((1, LANES), lambda i: (0, 0))],
        out_shape=[jax.ShapeDtypeStruct((t, d + LANES), F32),
                   jax.ShapeDtypeStruct((t, 1), jnp.int32),
                   jax.ShapeDtypeStruct((t, 1), jnp.int32),
                   jax.ShapeDtypeStruct((1, LANES), F32)],
        scratch_shapes=[pltpu.VMEM((1, LANES), F32)],
        compiler_params=_cparams(("arbitrary",)),
        name="outproj_ln_router",
    )(a, b, x2, *consts)


def _moe_plan(bucket, rank, counts, t):
    counts = counts[0, :MOE_BUCKETS].astype(jnp.int32)
    padded = ((counts + MOE_TILE - 1) // MOE_TILE) * MOE_TILE
    ends = jnp.cumsum(padded)
    starts = ends - padded
    pos = jnp.take(starts, bucket[:, 0]) + rank[:, 0]
    n_tiles = t // MOE_TILE + MOE_BUCKETS
    tile_start = jnp.arange(n_tiles, dtype=jnp.int32) * MOE_TILE
    tile_bucket = jnp.minimum(jnp.sum((tile_start[:, None] >= ends[None, :]).astype(jnp.int32), axis=1),
                              MOE_BUCKETS - 1)
    pairs = [(a, b) for a in range(MOE_EPG) for b in range(a + 1, MOE_EPG)]
    first = jnp.asarray([g * MOE_EPG + a for g in range(MOE_GROUPS) for a, _ in pairs], jnp.int32)
    second = jnp.asarray([g * MOE_EPG + b for g in range(MOE_GROUPS) for _, b in pairs], jnp.int32)
    n_used = (ends[MOE_BUCKETS - 1] // MOE_TILE).reshape(1)
    return pos, ends, jnp.take(first, tile_bucket), jnp.take(second, tile_bucket), n_used, n_tiles


def _moe_scatter_kernel(ends_ref, pos_ref, xg_ref, xs_hbm, zero_ref, sem, zero_sem):
    n = pos_ref.shape[0]

    @pl.when(pl.program_id(0) == 0)
    def _():
        zero_ref[...] = jnp.zeros_like(zero_ref)
        total = ends_ref[MOE_BUCKETS - 1]

        def zero_copy(row0):
            return pltpu.make_async_copy(zero_ref, xs_hbm.at[pl.ds(pl.multiple_of(row0, MOE_TILE), MOE_TILE)],
                                         zero_sem)

        jobs = [(ends_ref[b] >= MOE_TILE, ends_ref[b] - MOE_TILE) for b in range(MOE_BUCKETS)]
        jobs += [(total + u * MOE_TILE < xs_hbm.shape[0], total + u * MOE_TILE) for u in range(MOE_BUCKETS)]
        for wanted, row0 in jobs:
            @pl.when(wanted)
            def _():
                zero_copy(row0).start()
        for wanted, row0 in jobs:
            @pl.when(wanted)
            def _():
                zero_copy(row0).wait()

    def body(r, c):
        pltpu.make_async_copy(xg_ref.at[pl.ds(r, 1)], xs_hbm.at[pl.ds(pos_ref[r], 1)], sem).start()
        return c
    lax.fori_loop(0, n, body, 0, unroll=DMA_UNROLL)
    pltpu.make_async_copy(xg_ref, xs_hbm.at[pl.ds(0, n)], sem).wait()


def _moe_scatter(xg, pos, ends, n_rows, chunk):
    t, w = xg.shape
    return pl.pallas_call(
        _moe_scatter_kernel,
        grid=(t // chunk,),
        in_specs=[pl.BlockSpec(memory_space=pltpu.SMEM),
                  pl.BlockSpec((chunk,), lambda i: (i,), memory_space=pltpu.SMEM),
                  pl.BlockSpec((chunk, w), lambda i: (i, 0))],
        out_specs=pl.BlockSpec(memory_space=pl.ANY),
        out_shape=jax.ShapeDtypeStruct((n_rows, w), xg.dtype),
        scratch_shapes=[pltpu.VMEM((MOE_TILE, w), xg.dtype), pltpu.SemaphoreType.DMA(()),
                        pltpu.SemaphoreType.DMA(())],
        compiler_params=_cparams(("arbitrary",)),
        name="moe_scatter",
    )(ends, pos, xg)


def _moe_ffn_kernel(ea_ref, eb_ref, nu_ref, xs_ref, wga_ref, wua_ref, wda_ref, wgb_ref, wub_ref, wdb_ref, y_ref):
    i = pl.program_id(0)
    d = wga_ref.shape[0]

    @pl.when(i < nu_ref[0])
    def _():
        x = xs_ref[:, 0:d].astype(BF16)
        gate = xs_ref[:, d:d + LANES]
        lane = _iota(gate.shape, 1)
        y = None
        for e_ref, wg_ref, wu_ref, wd_ref in ((ea_ref, wga_ref, wua_ref, wda_ref), (eb_ref, wgb_ref, wub_ref, wdb_ref)):
            w = jnp.sum(jnp.where(lane == e_ref[i], gate, 0.0), axis=1, keepdims=True)
            hid = _silu(_dot(x, wg_ref[...])) * _dot(x, wu_ref[...]) * w
            part = _dot(hid.astype(BF16), wd_ref[...])
            y = part if y is None else y + part
        y_ref[...] = y

    @pl.when(i >= nu_ref[0])
    def _():
        y_ref[...] = jnp.zeros_like(y_ref)


def _moe_ffn(xs, ea, eb, n_used, n_tiles, w_gate, w_up, w_down):
    d = w_gate.shape[1]
    wg, wu, wd = w_gate.astype(BF16), w_up.astype(BF16), w_down.astype(BF16)
    up_a = pl.BlockSpec((None, d, MOE_FF), lambda i, ea, eb, nu: (ea[i], 0, 0))
    dn_a = pl.BlockSpec((None, MOE_FF, d), lambda i, ea, eb, nu: (ea[i], 0, 0))
    up_b = pl.BlockSpec((None, d, MOE_FF), lambda i, ea, eb, nu: (eb[i], 0, 0))
    dn_b = pl.BlockSpec((None, MOE_FF, d), lambda i, ea, eb, nu: (eb[i], 0, 0))
    return pl.pallas_call(
        _moe_ffn_kernel,
        grid_spec=pltpu.PrefetchScalarGridSpec(
            num_scalar_prefetch=3,
            grid=(n_tiles,),
            in_specs=[pl.BlockSpec((MOE_TILE, d + LANES), lambda i, ea, eb, nu: (jnp.minimum(i, nu[0] - 1), 0)),
                      up_a, up_a, dn_a, up_b, up_b, dn_b],
            out_specs=pl.BlockSpec((MOE_TILE, d), lambda i, ea, eb, nu: (i, 0))),
        out_shape=jax.ShapeDtypeStruct((n_tiles * MOE_TILE, d), F32),
        compiler_params=_cparams(("arbitrary",)),
        name="moe_ffn",
    )(ea, eb, n_used, xs, wg, wu, wd, wg, wu, wd)


def _moe_combine_kernel(pos_ref, pos_next_ref, y_hbm, x_ref, g_ref, beta_ref, o_ref, buf_ref, sem):
    i = pl.program_id(0)
    n = pos_ref.shape[0]
    slot = i % 2

    def gather(p_ref, s):
        def body(r, c):
            pltpu.make_async_copy(y_hbm.at[pl.ds(p_ref[r], 1)], buf_ref.at[s, pl.ds(r, 1)], sem.at[s]).start()
            return c
        lax.fori_loop(0, n, body, 0, unroll=DMA_UNROLL)

    @pl.when(i == 0)
    def _():
        gather(pos_ref, slot)

    @pl.when(i + 1 < pl.num_programs(0))
    def _():
        gather(pos_next_ref, 1 - slot)

    pltpu.make_async_copy(y_hbm.at[pl.ds(0, n)], buf_ref.at[slot], sem.at[slot]).wait()
    r = DEEPNORM_ALPHA * x_ref[...] + buf_ref[slot]
    o_ref[...] = _layer_norm_rows(r, g_ref[...], beta_ref[...])


def _moe_combine(y_sorted, pos, xg, ln_g, ln_b, tm):
    t = pos.shape[0]
    d = y_sorted.shape[1]
    ln = [ln_g.reshape(1, -1).astype(F32), ln_b.reshape(1, -1).astype(F32)]
    last = t // tm - 1
    return pl.pallas_call(
        _moe_combine_kernel,
        grid=(t // tm,),
        in_specs=[pl.BlockSpec((tm,), lambda i: (i,), memory_space=pltpu.SMEM),
                  pl.BlockSpec((tm,), lambda i: (jnp.minimum(i + 1, last),), memory_space=pltpu.SMEM),
                  pl.BlockSpec(memory_space=pl.ANY),
                  pl.BlockSpec((tm, d), lambda i: (i, 0))]
        + [pl.BlockSpec(c.shape, lambda i: (0, 0)) for c in ln],
        out_specs=pl.BlockSpec((tm, d), lambda i: (i, 0)),
        out_shape=jax.ShapeDtypeStruct((t, d), F32),
        scratch_shapes=[pltpu.VMEM((2, tm, d), F32), pltpu.SemaphoreType.DMA((2,))],
        compiler_params=_cparams(("arbitrary",)),
        name="moe_combine",
    )(pos, pos, y_sorted, xg, *ln)


def _mixer_out_and_moe(a, b, x2, w_out, ln1_g, ln1_b, w_group, b_group, w_expert, b_expert,
                       w_gate, w_up, w_down, ln2_g, ln2_b, tm):
    t = x2.shape[0]
    xg, bucket, rank, counts = _outproj_ln_router(a, b, x2, w_out, ln1_g, ln1_b, w_group, b_group,
                                                  w_expert, b_expert, tm)
    pos, ends, ea, eb, n_used, n_tiles = _moe_plan(bucket, rank, counts, t)
    xs = _moe_scatter(xg, pos, ends, n_tiles * MOE_TILE, _row_tile(t, 1024))
    y_sorted = _moe_ffn(xs, ea, eb, n_used, n_tiles, w_gate, w_up, w_down)
    return _moe_combine(y_sorted, pos, xg, ln2_g, ln2_b, tm)


def _row_tile(n, pref):
    tm = min(pref, n)
    assert n % tm == 0
    return tm


def kernel(x, ev_w_in, ev_conv_w, ev_conv_b, ev_dt_bias, ev_a_log, ev_d_skip, ev_norm_g, ev_w_out, od_w_in, od_q_norm_g, od_w_q_up, od_kv_norm_g, od_w_kv_up, od_f_bias, od_w_out, ln1_g, ln1_b, ln2_g, ln2_b, moe_w_group, moe_b_group, moe_w_expert, moe_b_expert, moe_w_gate, moe_w_up, moe_w_down):
    batch, seq, d = x.shape
    t = batch * seq
    assert seq % SSD_CHUNK == 0 and seq % LANES == 0 and t % MOE_TILE == 0
    tm_proj = _row_tile(seq, 512)
    x2 = x.reshape(t, d)

    z, xbc, dt, dtt, q, k, v = _proj_even(x2, ev_w_in[0], tm_proj)
    y_ssd = _ssd(xbc, z, dt, dtt, ev_conv_w[0], ev_conv_b[0], ev_dt_bias[0], ev_a_log[0], ev_d_skip[0],
                 ev_norm_g[0], batch, seq)
    y_sb = _attention("sb", q, k, v, batch, seq)
    x2 = _mixer_out_and_moe(y_ssd, y_sb, x2, ev_w_out[0], ln1_g[0], ln1_b[0], moe_w_group[0], moe_b_group[0],
                            moe_w_expert[0], moe_b_expert[0], moe_w_gate[0], moe_w_up[0], moe_w_down[0],
                            ln2_g[0], ln2_b[0], tm_proj)

    q, k, v, fh, qm, km, vm = _proj_odd(x2, od_w_in[0], od_f_bias[0], od_q_norm_g[0], od_w_q_up[0],
                                        od_kv_norm_g[0], od_w_kv_up[0], seq, tm_proj)
    y_mla = _attention("mla", qm, km, vm, batch, seq)
    y_fox = _attention("fox", q, k, v, batch, seq, fh=fh)
    x2 = _mixer_out_and_moe(y_mla, y_fox, x2, od_w_out[0], ln1_g[1], ln1_b[1], moe_w_group[1], moe_b_group[1],
                            moe_w_expert[1], moe_b_expert[1], moe_w_gate[1], moe_w_up[1], moe_w_down[1],
                            ln2_g[1], ln2_b[1], tm_proj)
    return x2.reshape(batch, seq, d)
```

```python
import functools
import math

import jax
import jax.numpy as jnp
from jax import lax
from jax.experimental import pallas as pl
from jax.experimental.pallas import tpu as pltpu

F32 = jnp.float32
BF16 = jnp.bfloat16

SSD_HEADS = 16
SSD_HEAD_DIM = 64
SSD_INNER = SSD_HEADS * SSD_HEAD_DIM
SSD_GROUPS = 2
SSD_STATE = 128
SSD_CONV = 4
SSD_BC = SSD_GROUPS * SSD_STATE
SSD_CONV_DIM = SSD_INNER + 2 * SSD_BC
ATT_HEADS = 8
ATT_HEAD_DIM = 64
ATT_WIDTH = ATT_HEADS * ATT_HEAD_DIM
HEAD_PAIRS = ATT_HEADS // 2
MLA_Q_RANK = 256
MLA_KV_RANK = 128
MLA_NOPE = 64
MLA_ROPE = 32
MLA_V = 64
MLA_CHUNK = 64
ROPE_THETA = 10000.0
MOE_GROUPS = 4
MOE_EPG = 4
MOE_EXPERTS = MOE_GROUPS * MOE_EPG
MOE_FF = 256
DEPTH = 2
DEEPNORM_ALPHA = (2.0 * DEPTH) ** 0.25
LN_EPS = 1e-5
RMS_EPS = 1e-6

LANES = 128
SSD_CHUNK = 128
ATT_TQ = {"sb": 256, "mla": 512, "fox": 256}
ATT_TK = {"sb": 256, "mla": 512, "fox": 256}
ATT_DIAG_ROWS = 128
NEG_BIG = -1e30
EXP_ZERO = 110.0
MOE_TILE = 512
DMA_UNROLL = 8
MOE_PAIRS = MOE_EPG * (MOE_EPG - 1) // 2
MOE_BUCKETS = MOE_GROUPS * MOE_PAIRS
VMEM_LIMIT = 56 * 1024 * 1024


def _cparams(sem):
    return pltpu.CompilerParams(dimension_semantics=sem, vmem_limit_bytes=VMEM_LIMIT)


def _dot(a, b):
    return jnp.dot(a, b, preferred_element_type=F32)


def _dot_nt(a, b):
    return lax.dot_general(a, b, (((1,), (1,)), ((), ())), preferred_element_type=F32)


def _split3(x):
    hi = x.astype(BF16)
    r1 = x - hi.astype(F32)
    mid = r1.astype(BF16)
    lo = (r1 - mid.astype(F32)).astype(BF16)
    return hi, mid, lo


def _dot01_right(x, m01):
    hi, mid, lo = _split3(x)
    return _dot(hi, m01) + _dot(mid, m01) + _dot(lo, m01)


def _dot01_left(m01, x):
    hi, mid, lo = _split3(x)
    return _dot(m01, hi) + _dot(m01, mid) + _dot(m01, lo)


def _softplus(x):
    return jnp.maximum(x, 0.0) + jnp.log(1.0 + jnp.exp(-jnp.abs(x)))


def _silu(x):
    return x * (1.0 / (1.0 + jnp.exp(-x)))


def _iota(shape, dim):
    return lax.broadcasted_iota(jnp.int32, shape, dim)


def _proj_even_kernel(x_ref, wz_ref, wx_ref, wdt_ref, wdtt_ref, wq_ref, wk_ref, wv_ref,
                      z_ref, xbc_ref, dt_ref, dtt_ref, q_ref, k_ref, v_ref):
    x = x_ref[...].astype(BF16)
    z_ref[...] = _dot(x, wz_ref[...]).astype(z_ref.dtype)
    xbc_ref[...] = _dot(x, wx_ref[...]).astype(xbc_ref.dtype)
    dt_ref[...] = _dot(x, wdt_ref[...])
    dtt_ref[...] = _dot_nt(wdtt_ref[...], x)
    q_ref[...] = _dot(x, wq_ref[...]).astype(q_ref.dtype)
    k_ref[...] = _dot(x, wk_ref[...]).astype(k_ref.dtype)
    v_ref[...] = _dot(x, wv_ref[...]).astype(v_ref.dtype)


def _proj_even(x2, w_in, tm):
    t, d = x2.shape
    o = 0
    parts = []
    for n in (SSD_INNER, SSD_CONV_DIM, SSD_HEADS, ATT_WIDTH, ATT_WIDTH, ATT_WIDTH):
        parts.append(w_in[:, o:o + n])
        o += n
    wz, wx, wdt, wq, wk, wv = parts
    wq = wq * (1.0 / math.sqrt(ATT_HEAD_DIM))
    ws = [wz.astype(BF16), wx.astype(BF16), wdt.astype(BF16), wdt.T.astype(BF16),
          wq.astype(BF16), wk.astype(BF16), wv.astype(BF16)]
    row = lambda n: pl.BlockSpec((tm, n), lambda i: (i, 0))
    full = lambda a: pl.BlockSpec(a.shape, lambda i: (0, 0))
    return pl.pallas_call(
        _proj_even_kernel,
        grid=(t // tm,),
        in_specs=[row(d)] + [full(w) for w in ws],
        out_specs=[row(SSD_INNER), row(SSD_CONV_DIM), row(SSD_HEADS),
                   pl.BlockSpec((SSD_HEADS, tm), lambda i: (0, i)),
                   row(ATT_WIDTH), row(ATT_WIDTH), row(ATT_WIDTH)],
        out_shape=[jax.ShapeDtypeStruct((t, SSD_INNER), BF16),
                   jax.ShapeDtypeStruct((t, SSD_CONV_DIM), BF16),
                   jax.ShapeDtypeStruct((t, SSD_HEADS), F32),
                   jax.ShapeDtypeStruct((SSD_HEADS, t), F32),
                   jax.ShapeDtypeStruct((t, ATT_WIDTH), BF16),
                   jax.ShapeDtypeStruct((t, ATT_WIDTH), BF16),
                   jax.ShapeDtypeStruct((t, ATT_WIDTH), BF16)],
        compiler_params=_cparams(("parallel",)),
        name="proj_even",
    )(x2, *ws)


def _mla_rows(cq, ckv, kpe, kps, cos1, sin1, qg, kvg, wqm, wqs, wk, wv):
    def rms(x, g):
        return x * lax.rsqrt(jnp.mean(x * x, axis=1, keepdims=True) + RMS_EPS) * g

    cqn = rms(cq, qg).astype(BF16)
    ckvn = rms(ckv, kvg).astype(BF16)
    cos8 = jnp.concatenate([cos1] * ATT_HEADS, axis=1)
    sin8 = jnp.concatenate([sin1] * ATT_HEADS, axis=1)
    scale = 1.0 / math.sqrt(MLA_NOPE + MLA_ROPE)
    q = (_dot(cqn, wqm) * cos8 + _dot(cqn, wqs) * sin8) * scale
    k_rope = kpe * cos1 + kps * sin1
    k = _dot(ckvn, wk) + jnp.concatenate([k_rope] * ATT_HEADS, axis=1)
    return q, k, _dot(ckvn, wv)


def _proj_odd_kernel(x_ref, wcq_ref, wckv_ref, wkpe_ref, wkps_ref, wq_ref, wk_ref, wv_ref, wft_ref, fb_ref,
                     cos_ref, sin_ref, qg_ref, kvg_ref, wqm_ref, wqs_ref, wkup_ref, wvup_ref, upper_ref,
                     q_ref, k_ref, v_ref, fh_ref, qm_ref, km_ref, vm_ref, carry_ref, *, tiles_per_seq):
    i = pl.program_id(0)
    tm = x_ref.shape[0]
    x = x_ref[...].astype(BF16)
    q_ref[...] = _dot(x, wq_ref[...]).astype(q_ref.dtype)
    k_ref[...] = _dot(x, wk_ref[...]).astype(k_ref.dtype)
    v_ref[...] = _dot(x, wv_ref[...]).astype(v_ref.dtype)
    qm, km, vm = _mla_rows(_dot(x, wcq_ref[...]), _dot(x, wckv_ref[...]), _dot(x, wkpe_ref[...]),
                           _dot(x, wkps_ref[...]), cos_ref[...], sin_ref[...], qg_ref[...], kvg_ref[...],
                           wqm_ref[...], wqs_ref[...], wkup_ref[...], wvup_ref[...])
    qm_ref[...] = qm.astype(qm_ref.dtype)
    km_ref[...] = km.astype(km_ref.dtype)
    vm_ref[...] = vm.astype(vm_ref.dtype)

    @pl.when(i % tiles_per_seq == 0)
    def _():
        carry_ref[...] = jnp.zeros_like(carry_ref)

    f_raw = _dot_nt(wft_ref[...], x) + fb_ref[...]
    log_f = -_softplus(-f_raw)
    cum = _dot01_right(log_f, upper_ref[...]) + carry_ref[...]
    fh_ref[...] = cum
    carry_ref[...] = cum[:, tm - 1:tm]


def _proj_odd(x2, w_in, f_bias, q_norm_g, w_q_up, kv_norm_g, w_kv_up, seq, tm):
    t, d = x2.shape
    sizes = (MLA_Q_RANK, MLA_KV_RANK, MLA_ROPE, ATT_WIDTH, ATT_WIDTH, ATT_WIDTH, ATT_HEADS)
    o = 0
    parts = []
    for n in sizes:
        parts.append(w_in[:, o:o + n])
        o += n
    wcq, wckv, wkpe, wq, wk, wv, wf = parts
    half = MLA_ROPE // 2
    pad = LANES - MLA_NOPE - MLA_ROPE
    pad_l = jnp.zeros((d, MLA_NOPE), F32)
    pad_r = jnp.zeros((d, pad), F32)
    wkpe_p = jnp.concatenate([pad_l, wkpe, pad_r], axis=1)
    wkps_p = jnp.concatenate([pad_l, -wkpe[:, half:], wkpe[:, :half], pad_r], axis=1)
    wq = wq * (1.0 / math.sqrt(ATT_HEAD_DIM))
    ws = [wcq.astype(BF16), wckv.astype(BF16), wkpe_p.astype(BF16), wkps_p.astype(BF16),
          wq.astype(BF16), wk.astype(BF16), wv.astype(BF16), wf.T.astype(BF16)]
    fb = f_bias.reshape(ATT_HEADS, 1).astype(F32)
    inv_freq = ROPE_THETA ** (-(jnp.arange(0, MLA_ROPE, 2, dtype=F32) / MLA_ROPE))
    ang = jnp.arange(seq, dtype=F32)[:, None] * inv_freq[None, :]
    cos, sin = jnp.cos(ang), jnp.sin(ang)
    cos1 = jnp.concatenate([jnp.ones((seq, MLA_NOPE), F32), cos, cos, jnp.zeros((seq, pad), F32)], axis=1)
    sin1 = jnp.concatenate([jnp.zeros((seq, MLA_NOPE), F32), sin, sin, jnp.zeros((seq, pad), F32)], axis=1)
    wqu = w_q_up.reshape(MLA_Q_RANK, ATT_HEADS, MLA_NOPE + MLA_ROPE)
    zq = lambda n: jnp.zeros((MLA_Q_RANK, ATT_HEADS, n), F32)
    wq_main = jnp.concatenate([wqu, zq(pad)], axis=2).reshape(MLA_Q_RANK, ATT_HEADS * LANES)
    wq_swap = jnp.concatenate([zq(MLA_NOPE), -wqu[:, :, MLA_NOPE + half:], wqu[:, :, MLA_NOPE:MLA_NOPE + half],
                               zq(pad)], axis=2).reshape(MLA_Q_RANK, ATT_HEADS * LANES)
    wkv = w_kv_up.reshape(MLA_KV_RANK, ATT_HEADS, MLA_NOPE + MLA_V)
    wk_up = jnp.concatenate([wkv[:, :, :MLA_NOPE], jnp.zeros((MLA_KV_RANK, ATT_HEADS, LANES - MLA_NOPE), F32)],
                            axis=2).reshape(MLA_KV_RANK, ATT_HEADS * LANES)
    wv_up = wkv[:, :, MLA_NOPE:].reshape(MLA_KV_RANK, ATT_HEADS * MLA_V)
    mla = [q_norm_g.reshape(1, -1).astype(F32), kv_norm_g.reshape(1, -1).astype(F32),
           wq_main.astype(BF16), wq_swap.astype(BF16), wk_up.astype(BF16), wv_up.astype(BF16),
           jnp.triu(jnp.ones((tm, tm), BF16))]
    ns = seq // tm
    row = lambda n: pl.BlockSpec((tm, n), lambda i: (i, 0))
    tab = pl.BlockSpec((tm, LANES), lambda i: (i % ns, 0))
    full = lambda a: pl.BlockSpec(a.shape, lambda i: (0, 0))
    return pl.pallas_call(
        functools.partial(_proj_odd_kernel, tiles_per_seq=ns),
        grid=(t // tm,),
        in_specs=[row(d)] + [full(w) for w in ws] + [full(fb), tab, tab] + [full(a) for a in mla],
        out_specs=[row(ATT_WIDTH), row(ATT_WIDTH), row(ATT_WIDTH),
                   pl.BlockSpec((ATT_HEADS, tm), lambda i: (0, i)),
                   row(ATT_HEADS * LANES), row(ATT_HEADS * LANES), row(ATT_HEADS * MLA_V)],
        out_shape=[jax.ShapeDtypeStruct((t, ATT_WIDTH), BF16),
                   jax.ShapeDtypeStruct((t, ATT_WIDTH), BF16),
                   jax.ShapeDtypeStruct((t, ATT_WIDTH), BF16),
                   jax.ShapeDtypeStruct((ATT_HEADS, t), F32),
                   jax.ShapeDtypeStruct((t, ATT_HEADS * LANES), BF16),
                   jax.ShapeDtypeStruct((t, ATT_HEADS * LANES), BF16),
                   jax.ShapeDtypeStruct((t, ATT_HEADS * MLA_V), BF16)],
        scratch_shapes=[pltpu.VMEM((ATT_HEADS, 1), F32)],
        compiler_params=_cparams(("arbitrary",)),
        name="proj_odd",
    )(x2, *ws, fb, cos1, sin1, *mla)


def _ssd_kernel(xbc_ref, z_ref, dt_ref, dtt_ref, cw_ref, cb_ref, dtb_ref, dtbt_ref, alog_ref, alogt_ref,
                dskip_ref, ng_ref, o_ref, buf_ref, st_ref, y_ref):
    c = pl.program_id(1)
    L = SSD_CHUNK

    @pl.when(c == 0)
    def _():
        buf_ref[...] = jnp.zeros_like(buf_ref)
        st_ref[...] = jnp.zeros_like(st_ref)

    cur = xbc_ref[...]
    both = jnp.concatenate([buf_ref[(c + 1) % 2], cur], axis=0)
    shifts = range(1, SSD_CONV)
    sel = jnp.concatenate([(_iota((L, 2 * L), 1) == _iota((L, 2 * L), 0) + (L - k)) for k in shifts],
                          axis=0).astype(BF16)
    shifted = _dot(sel, both)
    acc = cb_ref[...] + cur.astype(F32) * cw_ref[SSD_CONV - 1:SSD_CONV, :]
    for n, k in enumerate(shifts):
        acc = acc + shifted[n * L:(n + 1) * L] * cw_ref[SSD_CONV - 1 - k:SSD_CONV - k, :]
    buf_ref[c % 2] = cur
    xbc = _silu(acc)
    b_mat = xbc[:, SSD_INNER:SSD_INNER + SSD_BC]
    c_mat = xbc[:, SSD_INNER + SSD_BC:]
    bt = jnp.transpose(b_mat).astype(BF16)
    c_bf = c_mat.astype(BF16)

    dt = _softplus(dt_ref[...] + dtb_ref[...])
    dtt = _softplus(dtt_ref[...] + dtbt_ref[...])
    da = dt * (-jnp.exp(alog_ref[...]))
    dat = dtt * (-jnp.exp(alogt_ref[...]))
    row = _iota((L, L), 0)
    col = _iota((L, L), 1)
    causal = col <= row
    a_cum = _dot01_left(causal.astype(BF16), da)
    a_cumt = _dot01_right(dat, (row <= col).astype(BF16))
    a_end = a_cum[L - 1:L, :]
    low_half = _iota((1, LANES), 1) < SSD_HEAD_DIM

    def pair_cols(m, h0):
        n = m.shape[0]
        return jnp.where(low_half, jnp.broadcast_to(m[:, h0:h0 + 1], (n, LANES)),
                         jnp.broadcast_to(m[:, h0 + 1:h0 + 2], (n, LANES)))

    sumsq = jnp.zeros((L, 1), F32)
    for g in range(SSD_GROUPS):
        bt_g = bt[g * SSD_STATE:(g + 1) * SSD_STATE, :]
        c_g = c_bf[:, g * SSD_STATE:(g + 1) * SSD_STATE]
        cb = _dot(c_g, bt_g)
        for pp in range(SSD_HEADS // SSD_GROUPS // 2):
            p = g * (SSD_HEADS // SSD_GROUPS // 2) + pp
            h0 = 2 * p
            lanes = slice(p * LANES, (p + 1) * LANES)
            xs = xbc[:, lanes]
            xs_bf = xs.astype(BF16)
            yd = []
            for h in (h0, h0 + 1):
                seg = a_cum[:, h:h + 1] - a_cumt[h:h + 1, :]
                dec = jnp.where(causal, jnp.exp(jnp.minimum(seg, 0.0)), 0.0)
                m = cb * dec * dtt[h:h + 1, :]
                yd.append(_dot(m.astype(BF16), xs_bf))
            y = jnp.where(low_half, yd[0], yd[1])
            a_p = pair_cols(a_cum, h0)
            dt_p = pair_cols(dt, h0)
            end_p = pair_cols(a_end, h0)
            st = st_ref[p]
            y = y + _dot(c_g, st.astype(BF16)) * jnp.exp(a_p)
            xw = xs * (jnp.exp(end_p - a_p) * dt_p)
            st_ref[p] = st * jnp.exp(end_p) + _dot(bt_g, xw.astype(BF16))
            y = y + dskip_ref[:, lanes] * xs
            y = y * _silu(z_ref[:, lanes].astype(F32))
            y_ref[:, lanes] = y
            sumsq = sumsq + jnp.sum(y * y, axis=1, keepdims=True)
    inv = lax.rsqrt(sumsq * (1.0 / SSD_INNER) + RMS_EPS)
    o_ref[...] = (y_ref[...] * inv * ng_ref[...]).astype(o_ref.dtype)


def _ssd(xbc, z, dt, dtt, conv_w, conv_b, dt_bias, a_log, d_skip, norm_g, batch, seq):
    t = xbc.shape[0]
    L = SSD_CHUNK
    nc = seq // L
    row = lambda n: pl.BlockSpec((L, n), lambda b, c: (b * nc + c, 0))
    full = lambda a: pl.BlockSpec(a.shape, lambda b, c: (0,) * a.ndim)
    params = [conv_w.astype(F32), conv_b.reshape(1, -1).astype(F32),
              dt_bias.reshape(1, -1).astype(F32), dt_bias.reshape(-1, 1).astype(F32),
              a_log.reshape(1, -1).astype(F32), a_log.reshape(-1, 1).astype(F32),
              jnp.repeat(d_skip.astype(F32), SSD_HEAD_DIM).reshape(1, -1),
              norm_g.reshape(1, -1).astype(F32)]
    return pl.pallas_call(
        _ssd_kernel,
        grid=(batch, nc),
        in_specs=[row(SSD_CONV_DIM), row(SSD_INNER), row(SSD_HEADS),
                  pl.BlockSpec((SSD_HEADS, L), lambda b, c: (0, b * nc + c))] + [full(a) for a in params],
        out_specs=row(SSD_INNER),
        out_shape=jax.ShapeDtypeStruct((t, SSD_INNER), BF16),
        scratch_shapes=[pltpu.VMEM((2, L, SSD_CONV_DIM), BF16),
                        pltpu.VMEM((SSD_HEADS // 2, SSD_STATE, LANES), F32),
                        pltpu.VMEM((L, SSD_INNER), F32)],
        compiler_params=_cparams(("parallel", "arbitrary")),
        name="ssd",
    )(xbc, z, dt, dtt, *params)


def _attention_block(i, k_norm, refs, mode, tq, tk, diag_split):
    if mode == "fox":
        q_ref, k_ref, v_ref, fh_ref, o_ref, acc_ref, m_ref, l_ref = refs
    else:
        q_ref, k_ref, v_ref, o_ref, acc_ref, m_ref, l_ref = refs
        fh_ref = None
    n_diag = tq // tk
    lane = _iota((1, LANES), 1)
    low_half = lane < ATT_HEAD_DIM
    row = _iota((tq, tk), 0)
    col = _iota((tq, tk), 1)

    def diag_mask(dd):
        c = col + dd * tk
        if mode == "sb":
            return c < row
        if mode == "mla":
            return (c // MLA_CHUNK) <= (row // MLA_CHUNK)
        return c <= row

    if mode == "sb":
        u_mat = (_iota((tk, tk), 0) > _iota((tk, tk), 1)).astype(BF16)

    q_rows = pl.ds(pl.multiple_of(i * tq, tq), tq)
    q = q_ref[q_rows, :]
    if mode == "mla":
        qs = [q[:, 0:LANES], q[:, LANES:2 * LANES]]
    else:
        zero = jnp.zeros_like(q)
        qs = [jnp.where(low_half, q, zero), jnp.where(low_half, zero, q)]

    if mode == "fox":
        fh_q = [fh_ref[hh:hh + 1, pl.ds(pl.multiple_of(i * tq, tq), LANES)][:, 0:1] for hh in range(2)]

    def rep(x, width):
        return jnp.concatenate([x] * (width // LANES), axis=1) if width > LANES else x

    def tile(chunks, split, first):
        rows = tq // split
        groups = [(hh, slice(rb * rows, (rb + 1) * rows), rb) for hh in range(2) for rb in range(split)]
        loaded = []
        for j, dd in chunks:
            k0 = pl.multiple_of(j * tk, tk)
            k_t = k_ref[pl.ds(k0, tk), :]
            k_h = [k_t[:, 0:LANES], k_t[:, LANES:2 * LANES]] if mode == "mla" else [k_t, k_t]
            bias = None
            if mode == "fox":
                bias = [fh_q[hh] - fh_ref[hh:hh + 1, pl.ds(k0, tk)] for hh in range(2)]
            loaded.append((k_h, v_ref[pl.ds(k0, tk), :], bias, None if dd is None else diag_mask(dd)))
        parts = []
        for hh, rs, rb in groups:
            ps_ = []
            for c, (j, dd) in enumerate(chunks):
                nk = tk if dd is None else min(tk, (rb + 1) * rows - dd * tk)
                if nk > 0:
                    ps_.append((c, nk, None if dd is None else loaded[c][3][rs, 0:nk]))
            parts.append(ps_)
        flat = [(g, c, nk, m) for g, ps_ in enumerate(parts) for c, nk, m in ps_]
        zs = [_dot_nt(qs[groups[g][0]][groups[g][1]], loaded[c][0][groups[g][0]][0:nk]) for g, c, nk, m in flat]
        if mode == "sb":
            if first:
                runs = [jnp.zeros((rows, LANES), F32) for _ in groups]
                accs = [None for _ in groups]
            else:
                runs = [m_ref[hh, rs] for hh, rs, _ in groups]
                accs = [acc_ref[hh, rs] for hh, rs, _ in groups]
            sps = [_softplus(z) for z in zs]
            l1ms = [-sp if m is None else jnp.where(m, -sp, 0.0) for sp, (_, _, _, m) in zip(sps, flat)]
            css = [_dot(l1m.astype(BF16), u_mat[0:nk, 0:nk]) for l1m, (_, _, nk, _) in zip(l1ms, flat)]
            tots = [cs[:, 0:1] + l1m[:, 0:1] for cs, l1m in zip(css, l1ms)]
            seen = []
            cur = list(runs)
            for n, (g, c, nk, m) in enumerate(flat):
                seen.append(cur[g])
                cur[g] = cur[g] + tots[n]
            ws = [jnp.exp((z - sp) + cs + rep(run, nk))
                  for z, sp, cs, run, (_, _, nk, _) in zip(zs, sps, css, seen, flat)]
            ws = [w if m is None else jnp.where(m, w, 0.0) for w, (_, _, _, m) in zip(ws, flat)]
            pvs = [_dot(w.astype(BF16), loaded[c][1][0:nk]) for w, (_, c, nk, _) in zip(ws, flat)]
            for n, (g, c, nk, m) in enumerate(flat):
                accs[g] = pvs[n] if accs[g] is None else accs[g] + pvs[n]
            for (hh, rs, _), acc, run in zip(groups, accs, cur):
                acc_ref[hh, rs] = acc
                m_ref[hh, rs] = run
        else:
            if mode == "fox":
                zs = [z + loaded[c][2][groups[g][0]][:, 0:nk] for z, (g, c, nk, m) in zip(zs, flat)]
            zs = [z if m is None else jnp.where(m, z, NEG_BIG) for z, (_, _, _, m) in zip(zs, flat)]
            if first:
                m_news = [jnp.full((rows, LANES), NEG_BIG, F32) for _ in groups]
            else:
                m_olds = [m_ref[hh, rs] for hh, rs, _ in groups]
                m_news = list(m_olds)
            for z, (g, _, _, _) in zip(zs, flat):
                m_news[g] = jnp.maximum(m_news[g], jnp.max(z, axis=1, keepdims=True))
            ps = [jnp.exp(z - rep(m_news[g], nk)) for z, (g, _, nk, _) in zip(zs, flat)]
            pvs = [_dot(p.astype(BF16), loaded[c][1][0:nk]) for p, (_, c, nk, _) in zip(ps, flat)]
            if first:
                l_news = [None for _ in groups]
                acc_news = [None for _ in groups]
            else:
                alphas = [jnp.exp(m_old - m_new) for m_old, m_new in zip(m_olds, m_news)]
                l_news = [alpha * l_ref[hh, rs] for alpha, (hh, rs, _) in zip(alphas, groups)]
                acc_news = [alpha * acc_ref[hh, rs] for alpha, (hh, rs, _) in zip(alphas, groups)]
            for p, pv, (g, _, _, _) in zip(ps, pvs, flat):
                ls = jnp.sum(p, axis=1, keepdims=True)
                l_news[g] = jnp.broadcast_to(ls, (rows, LANES)) if l_news[g] is None else l_news[g] + ls
                acc_news[g] = pv if acc_news[g] is None else acc_news[g] + pv
            for (hh, rs, _), m_new, l_new, acc in zip(groups, m_news, l_news, acc_news):
                l_ref[hh, rs] = l_new
                acc_ref[hh, rs] = acc
                m_ref[hh, rs] = m_new

    def more(j_done):
        if mode == "sb":
            return (jnp.max(m_ref[...]) > -EXP_ZERO).astype(jnp.int32)
        if mode == "fox":
            kl = pl.multiple_of(jnp.maximum(j_done - 1, 0) * tk, tk)
            best = None
            for hh in range(2):
                fh_last = fh_ref[hh:hh + 1, pl.ds(kl, tk)][:, tk - 1:tk]
                v = jnp.max(qk_bound - m_ref[hh], axis=(0, 1), keepdims=True) + (fh_q[hh] - fh_last)
                best = v if best is None else jnp.maximum(best, v)
            return (jnp.max(best) > -EXP_ZERO).astype(jnp.int32)
        return jnp.int32(1)

    if mode == "fox":
        qf = q.astype(F32)
        qk_bound = jnp.sqrt(jnp.sum(qf * qf, axis=1, keepdims=True)) * k_norm

    n_off = i * n_diag
    diag_chunks = [(n_off + dd, dd) for dd in reversed(range(n_diag))]

    @pl.when(i == 0)
    def _():
        tile(diag_chunks, diag_split, True)

    @pl.when(i > 0)
    def _():
        tile(diag_chunks + [(n_off - 1, None)], diag_split, True)

    def cond(c):
        return jnp.logical_and(c[0] < n_off, c[1] > 0)

    def body(c):
        j = n_off - 1 - c[0]
        tile([(j, None)], 1, False)
        return c[0] + 1, more(j)
    lax.while_loop(cond, body, (jnp.int32(1), more(jnp.maximum(n_off - 1, 0))))
    if mode == "sb":
        out = [acc_ref[0], acc_ref[1]]
    else:
        out = [acc_ref[hh] / l_ref[hh] for hh in range(2)]
    o_ref[q_rows, :] = jnp.where(low_half, out[0], out[1]).astype(o_ref.dtype)


def _attention_kernel(*refs, mode, tq, tk, diag_split, nq):
    k_norm = None
    if mode == "fox":
        kf = refs[1][...].astype(F32)
        n2 = jnp.max(jnp.sum(kf * kf, axis=1, keepdims=True), axis=0, keepdims=True)
        k_norm = jnp.broadcast_to(jnp.sqrt(n2), (1, LANES))

    def q_block(i, carry):
        _attention_block(i, k_norm, refs, mode, tq, tk, diag_split)
        return carry
    lax.fori_loop(0, nq, q_block, 0)


def _attention(mode, q, k, v, batch, seq, fh=None):
    t = v.shape[0]
    tq = min(ATT_TQ[mode], seq)
    tk = min(ATT_TK[mode], tq)
    nq = seq // tq
    qk_w = 2 * LANES if mode == "mla" else LANES
    in_specs = [pl.BlockSpec((seq, qk_w), lambda b, h: (b, h)),
                pl.BlockSpec((seq, qk_w), lambda b, h: (b, h)),
                pl.BlockSpec((seq, LANES), lambda b, h: (b, h))]
    args = [q, k, v]
    scratch = [pltpu.VMEM((2, tq, LANES), F32), pltpu.VMEM((2, tq, LANES), F32), pltpu.VMEM((2, tq, LANES), F32)]
    if mode == "fox":
        in_specs.append(pl.BlockSpec((None, 2, seq), lambda b, h: (h, 0, b)))
        args.append(fh.reshape(HEAD_PAIRS, 2, t))
    return pl.pallas_call(
        functools.partial(_attention_kernel, mode=mode, tq=tq, tk=tk, diag_split=max(1, tq // ATT_DIAG_ROWS), nq=nq),
        grid=(batch, HEAD_PAIRS),
        in_specs=in_specs,
        out_specs=pl.BlockSpec((seq, LANES), lambda b, h: (b, h)),
        out_shape=jax.ShapeDtypeStruct((t, ATT_WIDTH), BF16),
        scratch_shapes=scratch,
        compiler_params=_cparams(("parallel", "parallel")),
        name="attn_" + mode,
    )(*args)


def _layer_norm_rows(r, g, b):
    mu = jnp.mean(r, axis=1, keepdims=True)
    d = r - mu
    var = jnp.mean(d * d, axis=1, keepdims=True)
    return d * lax.rsqrt(var + LN_EPS) * g + b


def _router(x, w_hi, w_lo, bias):
    tm = x.shape[0]
    x_hi = x.astype(BF16)
    x_lo = (x - x_hi.astype(F32)).astype(BF16)
    logits = _dot(x_hi, w_hi) + (_dot(x_hi, w_lo) + _dot(x_lo, w_hi)) + bias
    e_idx = _iota((tm, LANES), 1)
    is_group = jnp.logical_and(e_idx >= MOE_EXPERTS, e_idx < MOE_EXPERTS + MOE_GROUPS)
    gl = jnp.where(is_group, logits, -jnp.inf)
    g_max = jnp.max(gl, axis=1, keepdims=True)
    g_p = 1.0 / jnp.sum(jnp.exp(gl - g_max), axis=1, keepdims=True)
    g_sel = jnp.min(jnp.where(gl == g_max, e_idx, LANES), axis=1, keepdims=True) - MOE_EXPERTS
    in_group = (e_idx // MOE_EPG) == g_sel
    masked = jnp.where(in_group, logits, -jnp.inf)
    m1 = jnp.max(masked, axis=1, keepdims=True)
    i1 = jnp.min(jnp.where(masked == m1, e_idx, LANES), axis=1, keepdims=True)
    masked2 = jnp.where(e_idx == i1, -jnp.inf, masked)
    m2 = jnp.max(masked2, axis=1, keepdims=True)
    i2 = jnp.min(jnp.where(masked2 == m2, e_idx, LANES), axis=1, keepdims=True)
    e2 = jnp.exp(m2 - m1)
    w1 = g_p / (1.0 + e2)
    w2 = w1 * e2
    gate = jnp.where(e_idx == i1, w1, 0.0) + jnp.where(e_idx == i2, w2, 0.0)
    lo = jnp.minimum(i1, i2) - g_sel * MOE_EPG
    hi = jnp.maximum(i1, i2) - g_sel * MOE_EPG
    pair = ((lo * (2 * MOE_EPG - 1 - lo)) >> 1) + (hi - lo - 1)
    return gate, g_sel * MOE_PAIRS + pair


def _outproj_ln_router_kernel(a_ref, b_ref, x_ref, wa_ref, wb_ref, g_ref, beta_ref, wr_hi_ref, wr_lo_ref, br_ref,
                              tri_ref, xg_ref, bucket_ref, rank_ref, count_ref, run_ref):
    i = pl.program_id(0)
    tm, d = x_ref.shape

    @pl.when(i == 0)
    def _():
        run_ref[...] = jnp.zeros_like(run_ref)

    y = _dot(a_ref[...], wa_ref[...]) + _dot(b_ref[...], wb_ref[...])
    x1 = _layer_norm_rows(DEEPNORM_ALPHA * x_ref[...] + y, g_ref[...], beta_ref[...])
    gate, bucket = _router(x1, wr_hi_ref[...], wr_lo_ref[...], br_ref[...])
    xg_ref[:, 0:d] = x1
    xg_ref[:, d:d + LANES] = gate
    bucket_ref[...] = bucket
    onehot = _iota((tm, LANES), 1) == bucket
    prefix = _dot(tri_ref[...], jnp.where(onehot, 1.0, 0.0).astype(BF16))
    before = run_ref[...]
    rank = jnp.sum(jnp.where(onehot, prefix + before, 0.0), axis=1, keepdims=True) - 1.0
    rank_ref[...] = rank.astype(jnp.int32)
    run_ref[...] = before + prefix[tm - 1:tm, :]
    count_ref[...] = run_ref[...]


def _outproj_ln_router(a, b, x2, w_out, ln_g, ln_b, w_group, b_group, w_expert, b_expert, tm):
    t, d = x2.shape
    ka, kb = a.shape[1], b.shape[1]
    pad = LANES - MOE_EXPERTS - MOE_GROUPS
    w_r = jnp.pad(jnp.concatenate([w_expert, w_group], axis=1).astype(F32), ((0, 0), (0, pad)))
    b_r = jnp.pad(jnp.concatenate([b_expert, b_group]).astype(F32), (0, pad)).reshape(1, LANES)
    w_r_hi = w_r.astype(BF16)
    consts = [w_out[:ka].astype(BF16), w_out[ka:].astype(BF16),
              ln_g.reshape(1, -1).astype(F32), ln_b.reshape(1, -1).astype(F32),
              w_r_hi, (w_r - w_r_hi.astype(F32)).astype(BF16), b_r,
              jnp.tril(jnp.ones((tm, tm), BF16))]
    row = lambda n: pl.BlockSpec((tm, n), lambda i: (i, 0))
    full = lambda c: pl.BlockSpec(c.shape, lambda i: (0, 0))
    return pl.pallas_call(
        _outproj_ln_router_kernel,
        grid=(t // tm,),
        in_specs=[row(ka), row(kb), row(d)] + [full(c) for c in consts],
        out_specs=[row(d + LANES), row(1), row(1), pl.BlockSpec((1, LANES), lambda i: (0, 0))],
        out_shape=[jax.ShapeDtypeStruct((t, d + LANES), F32),
                   jax.ShapeDtypeStruct((t, 1), jnp.int32),
                   jax.ShapeDtypeStruct((t, 1), jnp.int32),
                   jax.ShapeDtypeStruct((1, LANES), F32)],
        scratch_shapes=[pltpu.VMEM((1, LANES), F32)],
        compiler_params=_cparams(("arbitrary",)),
        name="outproj_ln_router",
    )(a, b, x2, *consts)


def _moe_plan(bucket, rank, counts, t):
    counts = counts[0, :MOE_BUCKETS].astype(jnp.int32)
    padded = ((counts + MOE_TILE - 1) // MOE_TILE) * MOE_TILE
    ends = jnp.cumsum(padded)
    starts = ends - padded
    pos = jnp.take(starts, bucket[:, 0]) + rank[:, 0]
    n_tiles = t // MOE_TILE + MOE_BUCKETS
    tile_start = jnp.arange(n_tiles, dtype=jnp.int32) * MOE_TILE
    tile_bucket = jnp.minimum(jnp.sum((tile_start[:, None] >= ends[None, :]).astype(jnp.int32), axis=1),
                              MOE_BUCKETS - 1)
    pairs = [(a, b) for a in range(MOE_EPG) for b in range(a + 1, MOE_EPG)]
    first = jnp.asarray([g * MOE_EPG + a for g in range(MOE_GROUPS) for a, _ in pairs], jnp.int32)
    second = jnp.asarray([g * MOE_EPG + b for g in range(MOE_GROUPS) for _, b in pairs], jnp.int32)
    n_used = (ends[MOE_BUCKETS - 1] // MOE_TILE).reshape(1)
    return pos, ends, jnp.take(first, tile_bucket), jnp.take(second, tile_bucket), n_used, n_tiles


def _moe_scatter_kernel(ends_ref, pos_ref, xg_ref, xs_hbm, zero_ref, sem, zero_sem):
    n = pos_ref.shape[0]

    @pl.when(pl.program_id(0) == 0)
    def _():
        zero_ref[...] = jnp.zeros_like(zero_ref)
        total = ends_ref[MOE_BUCKETS - 1]

        def zero_copy(row0):
            return pltpu.make_async_copy(zero_ref, xs_hbm.at[pl.ds(pl.multiple_of(row0, MOE_TILE), MOE_TILE)],
                                         zero_sem)

        jobs = [(ends_ref[b] >= MOE_TILE, ends_ref[b] - MOE_TILE) for b in range(MOE_BUCKETS)]
        jobs += [(total + u * MOE_TILE < xs_hbm.shape[0], total + u * MOE_TILE) for u in range(MOE_BUCKETS)]
        for wanted, row0 in jobs:
            @pl.when(wanted)
            def _():
                zero_copy(row0).start()
        for wanted, row0 in jobs:
            @pl.when(wanted)
            def _():
                zero_copy(row0).wait()

    def body(r, c):
        pltpu.make_async_copy(xg_ref.at[pl.ds(r, 1)], xs_hbm.at[pl.ds(pos_ref[r], 1)], sem).start()
        return c
    lax.fori_loop(0, n, body, 0, unroll=DMA_UNROLL)
    pltpu.make_async_copy(xg_ref, xs_hbm.at[pl.ds(0, n)], sem).wait()


def _moe_scatter(xg, pos, ends, n_rows, chunk):
    t, w = xg.shape
    return pl.pallas_call(
        _moe_scatter_kernel,
        grid=(t // chunk,),
        in_specs=[pl.BlockSpec(memory_space=pltpu.SMEM),
                  pl.BlockSpec((chunk,), lambda i: (i,), memory_space=pltpu.SMEM),
                  pl.BlockSpec((chunk, w), lambda i: (i, 0))],
        out_specs=pl.BlockSpec(memory_space=pl.ANY),
        out_shape=jax.ShapeDtypeStruct((n_rows, w), xg.dtype),
        scratch_shapes=[pltpu.VMEM((MOE_TILE, w), xg.dtype), pltpu.SemaphoreType.DMA(()),
                        pltpu.SemaphoreType.DMA(())],
        compiler_params=_cparams(("arbitrary",)),
        name="moe_scatter",
    )(ends, pos, xg)


def _moe_ffn_kernel(ea_ref, eb_ref, nu_ref, xs_ref, wga_ref, wua_ref, wda_ref, wgb_ref, wub_ref, wdb_ref, y_ref):
    i = pl.program_id(0)
    d = wga_ref.shape[0]

    @pl.when(i < nu_ref[0])
    def _():
        x = xs_ref[:, 0:d].astype(BF16)
        gate = xs_ref[:, d:d + LANES]
        lane = _iota(gate.shape, 1)
        y = None
        for e_ref, wg_ref, wu_ref, wd_ref in ((ea_ref, wga_ref, wua_ref, wda_ref), (eb_ref, wgb_ref, wub_ref, wdb_ref)):
            w = jnp.sum(jnp.where(lane == e_ref[i], gate, 0.0), axis=1, keepdims=True)
            hid = _silu(_dot(x, wg_ref[...])) * _dot(x, wu_ref[...]) * w
            part = _dot(hid.astype(BF16), wd_ref[...])
            y = part if y is None else y + part
        y_ref[...] = y

    @pl.when(i >= nu_ref[0])
    def _():
        y_ref[...] = jnp.zeros_like(y_ref)


def _moe_ffn(xs, ea, eb, n_used, n_tiles, w_gate, w_up, w_down):
    d = w_gate.shape[1]
    wg, wu, wd = w_gate.astype(BF16), w_up.astype(BF16), w_down.astype(BF16)
    up_a = pl.BlockSpec((None, d, MOE_FF), lambda i, ea, eb, nu: (ea[i], 0, 0))
    dn_a = pl.BlockSpec((None, MOE_FF, d), lambda i, ea, eb, nu: (ea[i], 0, 0))
    up_b = pl.BlockSpec((None, d, MOE_FF), lambda i, ea, eb, nu: (eb[i], 0, 0))
    dn_b = pl.BlockSpec((None, MOE_FF, d), lambda i, ea, eb, nu: (eb[i], 0, 0))
    return pl.pallas_call(
        _moe_ffn_kernel,
        grid_spec=pltpu.PrefetchScalarGridSpec(
            num_scalar_prefetch=3,
            grid=(n_tiles,),
            in_specs=[pl.BlockSpec((MOE_TILE, d + LANES), lambda i, ea, eb, nu: (jnp.minimum(i, nu[0] - 1), 0)),
                      up_a, up_a, dn_a, up_b, up_b, dn_b],
            out_specs=pl.BlockSpec((MOE_TILE, d), lambda i, ea, eb, nu: (i, 0))),
        out_shape=jax.ShapeDtypeStruct((n_tiles * MOE_TILE, d), F32),
        compiler_params=_cparams(("arbitrary",)),
        name="moe_ffn",
    )(ea, eb, n_used, xs, wg, wu, wd, wg, wu, wd)


def _moe_combine_kernel(pos_ref, pos_next_ref, y_hbm, x_ref, g_ref, beta_ref, o_ref, buf_ref, sem):
    i = pl.program_id(0)
    n = pos_ref.shape[0]
    slot = i % 2

    def gather(p_ref, s):
        def body(r, c):
            pltpu.make_async_copy(y_hbm.at[pl.ds(p_ref[r], 1)], buf_ref.at[s, pl.ds(r, 1)], sem.at[s]).start()
            return c
        lax.fori_loop(0, n, body, 0, unroll=DMA_UNROLL)

    @pl.when(i == 0)
    def _():
        gather(pos_ref, slot)

    @pl.when(i + 1 < pl.num_programs(0))
    def _():
        gather(pos_next_ref, 1 - slot)

    pltpu.make_async_copy(y_hbm.at[pl.ds(0, n)], buf_ref.at[slot], sem.at[slot]).wait()
    r = DEEPNORM_ALPHA * x_ref[...] + buf_ref[slot]
    o_ref[...] = _layer_norm_rows(r, g_ref[...], beta_ref[...])


def _moe_combine(y_sorted, pos, xg, ln_g, ln_b, tm):
    t = pos.shape[0]
    d = y_sorted.shape[1]
    ln = [ln_g.reshape(1, -1).astype(F32), ln_b.reshape(1, -1).astype(F32)]
    last = t // tm - 1
    return pl.pallas_call(
        _moe_combine_kernel,
        grid=(t // tm,),
        in_specs=[pl.BlockSpec((tm,), lambda i: (i,), memory_space=pltpu.SMEM),
                  pl.BlockSpec((tm,), lambda i: (jnp.minimum(i + 1, last),), memory_space=pltpu.SMEM),
                  pl.BlockSpec(memory_space=pl.ANY),
                  pl.BlockSpec((tm, d), lambda i: (i, 0))]
        + [pl.BlockSpec(c.shape, lambda i: (0, 0)) for c in ln],
        out_specs=pl.BlockSpec((tm, d), lambda i: (i, 0)),
        out_shape=jax.ShapeDtypeStruct((t, d), F32),
        scratch_shapes=[pltpu.VMEM((2, tm, d), F32), pltpu.SemaphoreType.DMA((2,))],
        compiler_params=_cparams(("arbitrary",)),
        name="moe_combine",
    )(pos, pos, y_sorted, xg, *ln)


def _mixer_out_and_moe(a, b, x2, w_out, ln1_g, ln1_b, w_group, b_group, w_expert, b_expert,
                       w_gate, w_up, w_down, ln2_g, ln2_b, tm):
    t = x2.shape[0]
    xg, bucket, rank, counts = _outproj_ln_router(a, b, x2, w_out, ln1_g, ln1_b, w_group, b_group,
                                                  w_expert, b_expert, tm)
    pos, ends, ea, eb, n_used, n_tiles = _moe_plan(bucket, rank, counts, t)
    xs = _moe_scatter(xg, pos, ends, n_tiles * MOE_TILE, _row_tile(t, 1024))
    y_sorted = _moe_ffn(xs, ea, eb, n_used, n_tiles, w_gate, w_up, w_down)
    return _moe_combine(y_sorted, pos, xg, ln2_g, ln2_b, tm)


def _row_tile(n, pref):
    tm = min(pref, n)
    assert n % tm == 0
    return tm


def kernel(x, ev_w_in, ev_conv_w, ev_conv_b, ev_dt_bias, ev_a_log, ev_d_skip, ev_norm_g, ev_w_out, od_w_in, od_q_norm_g, od_w_q_up, od_kv_norm_g, od_w_kv_up, od_f_bias, od_w_out, ln1_g, ln1_b, ln2_g, ln2_b, moe_w_group, moe_b_group, moe_w_expert, moe_b_expert, moe_w_gate, moe_w_up, moe_w_down):
    batch, seq, d = x.shape
    t = batch * seq
    assert seq % SSD_CHUNK == 0 and seq % LANES == 0 and t % MOE_TILE == 0
    tm_proj = _row_tile(seq, 512)
    x2 = x.reshape(t, d)

    z, xbc, dt, dtt, q, k, v = _proj_even(x2, ev_w_in[0], tm_proj)
    y_ssd = _ssd(xbc, z, dt, dtt, ev_conv_w[0], ev_conv_b[0], ev_dt_bias[0], ev_a_log[0], ev_d_skip[0],
                 ev_norm_g[0], batch, seq)
    y_sb = _attention("sb", q, k, v, batch, seq)
    x2 = _mixer_out_and_moe(y_ssd, y_sb, x2, ev_w_out[0], ln1_g[0], ln1_b[0], moe_w_group[0], moe_b_group[0],
                            moe_w_expert[0], moe_b_expert[0], moe_w_gate[0], moe_w_up[0], moe_w_down[0],
                            ln2_g[0], ln2_b[0], tm_proj)

    q, k, v, fh, qm, km, vm = _proj_odd(x2, od_w_in[0], od_f_bias[0], od_q_norm_g[0], od_w_q_up[0],
                                        od_kv_norm_g[0], od_w_kv_up[0], seq, tm_proj)
    y_mla = _attention("mla", qm, km, vm, batch, seq)
    y_fox = _attention("fox", q, k, v, batch, seq, fh=fh)
    x2 = _mixer_out_and_moe(y_mla, y_fox, x2, od_w_out[0], ln1_g[1], ln1_b[1], moe_w_group[1], moe_b_group[1],
                            moe_w_expert[1], moe_b_expert[1], moe_w_gate[1], moe_w_up[1], moe_w_down[1],
                            ln2_g[1], ln2_b[1], tm_proj)
    return x2.reshape(batch, seq, d)
```

```python
import functools
import math

import jax
import jax.numpy as jnp
from jax import lax
from jax.experimental import pallas as pl
from jax.experimental.pallas import tpu as pltpu

F32 = jnp.float32
BF16 = jnp.bfloat16

SSD_HEADS = 16
SSD_HEAD_DIM = 64
SSD_INNER = SSD_HEADS * SSD_HEAD_DIM
SSD_GROUPS = 2
SSD_STATE = 128
SSD_CONV = 4
SSD_BC = SSD_GROUPS * SSD_STATE
SSD_CONV_DIM = SSD_INNER + 2 * SSD_BC
ATT_HEADS = 8
ATT_HEAD_DIM = 64
ATT_WIDTH = ATT_HEADS * ATT_HEAD_DIM
HEAD_PAIRS = ATT_HEADS // 2
MLA_Q_RANK = 256
MLA_KV_RANK = 128
MLA_NOPE = 64
MLA_ROPE = 32
MLA_V = 64
MLA_CHUNK = 64
ROPE_THETA = 10000.0
MOE_GROUPS = 4
MOE_EPG = 4
MOE_EXPERTS = MOE_GROUPS * MOE_EPG
MOE_FF = 256
DEPTH = 2
DEEPNORM_ALPHA = (2.0 * DEPTH) ** 0.25
LN_EPS = 1e-5
RMS_EPS = 1e-6

LANES = 128
SSD_CHUNK = 128
ATT_TQ = {"sb": 256, "mla": 512, "fox": 256}
ATT_TK = {"sb": 256, "mla": 512, "fox": 256}
ATT_DIAG_ROWS = 128
NEG_BIG = -1e30
EXP_ZERO = 110.0
MOE_TILE = 512
DMA_UNROLL = 8
MOE_PAIRS = MOE_EPG * (MOE_EPG - 1) // 2
MOE_BUCKETS = MOE_GROUPS * MOE_PAIRS
VMEM_LIMIT = 56 * 1024 * 1024


def _cparams(sem):
    return pltpu.CompilerParams(dimension_semantics=sem, vmem_limit_bytes=VMEM_LIMIT)


def _dot(a, b):
    return jnp.dot(a, b, preferred_element_type=F32)


def _dot_nt(a, b):
    return lax.dot_general(a, b, (((1,), (1,)), ((), ())), preferred_element_type=F32)


def _split3(x):
    hi = x.astype(BF16)
    r1 = x - hi.astype(F32)
    mid = r1.astype(BF16)
    lo = (r1 - mid.astype(F32)).astype(BF16)
    return hi, mid, lo


def _dot01_right(x, m01):
    hi, mid, lo = _split3(x)
    return _dot(hi, m01) + _dot(mid, m01) + _dot(lo, m01)


def _dot01_left(m01, x):
    hi, mid, lo = _split3(x)
    return _dot(m01, hi) + _dot(m01, mid) + _dot(m01, lo)


def _softplus(x):
    return jnp.maximum(x, 0.0) + jnp.log(1.0 + jnp.exp(-jnp.abs(x)))


def _silu(x):
    return x * (1.0 / (1.0 + jnp.exp(-x)))


def _iota(shape, dim):
    return lax.broadcasted_iota(jnp.int32, shape, dim)


def _proj_even_kernel(x_ref, wz_ref, wx_ref, wdt_ref, wdtt_ref, wq_ref, wk_ref, wv_ref,
                      z_ref, xbc_ref, dt_ref, dtt_ref, q_ref, k_ref, v_ref):
    x = x_ref[...].astype(BF16)
    z_ref[...] = _dot(x, wz_ref[...]).astype(z_ref.dtype)
    xbc_ref[...] = _dot(x, wx_ref[...]).astype(xbc_ref.dtype)
    dt_ref[...] = _dot(x, wdt_ref[...])
    dtt_ref[...] = _dot_nt(wdtt_ref[...], x)
    q_ref[...] = _dot(x, wq_ref[...]).astype(q_ref.dtype)
    k_ref[...] = _dot(x, wk_ref[...]).astype(k_ref.dtype)
    v_ref[...] = _dot(x, wv_ref[...]).astype(v_ref.dtype)


def _proj_even(x2, w_in, tm):
    t, d = x2.shape
    o = 0
    parts = []
    for n in (SSD_INNER, SSD_CONV_DIM, SSD_HEADS, ATT_WIDTH, ATT_WIDTH, ATT_WIDTH):
        parts.append(w_in[:, o:o + n])
        o += n
    wz, wx, wdt, wq, wk, wv = parts
    wq = wq * (1.0 / math.sqrt(ATT_HEAD_DIM))
    ws = [wz.astype(BF16), wx.astype(BF16), wdt.astype(BF16), wdt.T.astype(BF16),
          wq.astype(BF16), wk.astype(BF16), wv.astype(BF16)]
    row = lambda n: pl.BlockSpec((tm, n), lambda i: (i, 0))
    full = lambda a: pl.BlockSpec(a.shape, lambda i: (0, 0))
    return pl.pallas_call(
        _proj_even_kernel,
        grid=(t // tm,),
        in_specs=[row(d)] + [full(w) for w in ws],
        out_specs=[row(SSD_INNER), row(SSD_CONV_DIM), row(SSD_HEADS),
                   pl.BlockSpec((SSD_HEADS, tm), lambda i: (0, i)),
                   row(ATT_WIDTH), row(ATT_WIDTH), row(ATT_WIDTH)],
        out_shape=[jax.ShapeDtypeStruct((t, SSD_INNER), BF16),
                   jax.ShapeDtypeStruct((t, SSD_CONV_DIM), BF16),
                   jax.ShapeDtypeStruct((t, SSD_HEADS), F32),
                   jax.ShapeDtypeStruct((SSD_HEADS, t), F32),
                   jax.ShapeDtypeStruct((t, ATT_WIDTH), BF16),
                   jax.ShapeDtypeStruct((t, ATT_WIDTH), BF16),
                   jax.ShapeDtypeStruct((t, ATT_WIDTH), BF16)],
        compiler_params=_cparams(("parallel",)),
        name="proj_even",
    )(x2, *ws)


def _mla_rows(cq, ckv, kpe, kps, cos1, sin1, qg, kvg, wqm, wqs, wk, wv):
    def rms(x, g):
        return x * lax.rsqrt(jnp.mean(x * x, axis=1, keepdims=True) + RMS_EPS) * g

    cqn = rms(cq, qg).astype(BF16)
    ckvn = rms(ckv, kvg).astype(BF16)
    cos8 = jnp.concatenate([cos1] * ATT_HEADS, axis=1)
    sin8 = jnp.concatenate([sin1] * ATT_HEADS, axis=1)
    scale = 1.0 / math.sqrt(MLA_NOPE + MLA_ROPE)
    q = (_dot(cqn, wqm) * cos8 + _dot(cqn, wqs) * sin8) * scale
    k_rope = kpe * cos1 + kps * sin1
    k = _dot(ckvn, wk) + jnp.concatenate([k_rope] * ATT_HEADS, axis=1)
    return q, k, _dot(ckvn, wv)


def _proj_odd_kernel(x_ref, wcq_ref, wckv_ref, wkpe_ref, wkps_ref, wq_ref, wk_ref, wv_ref, wft_ref, fb_ref,
                     cos_ref, sin_ref, qg_ref, kvg_ref, wqm_ref, wqs_ref, wkup_ref, wvup_ref, upper_ref,
                     q_ref, k_ref, v_ref, fh_ref, qm_ref, km_ref, vm_ref, carry_ref, *, tiles_per_seq):
    i = pl.program_id(0)
    tm = x_ref.shape[0]
    x = x_ref[...].astype(BF16)
    q_ref[...] = _dot(x, wq_ref[...]).astype(q_ref.dtype)
    k_ref[...] = _dot(x, wk_ref[...]).astype(k_ref.dtype)
    v_ref[...] = _dot(x, wv_ref[...]).astype(v_ref.dtype)
    qm, km, vm = _mla_rows(_dot(x, wcq_ref[...]), _dot(x, wckv_ref[...]), _dot(x, wkpe_ref[...]),
                           _dot(x, wkps_ref[...]), cos_ref[...], sin_ref[...], qg_ref[...], kvg_ref[...],
                           wqm_ref[...], wqs_ref[...], wkup_ref[...], wvup_ref[...])
    qm_ref[...] = qm.astype(qm_ref.dtype)
    km_ref[...] = km.astype(km_ref.dtype)
    vm_ref[...] = vm.astype(vm_ref.dtype)

    @pl.when(i % tiles_per_seq == 0)
    def _():
        carry_ref[...] = jnp.zeros_like(carry_ref)

    f_raw = _dot_nt(wft_ref[...], x) + fb_ref[...]
    log_f = -_softplus(-f_raw)
    cum = _dot01_right(log_f, upper_ref[...]) + carry_ref[...]
    fh_ref[...] = cum
    carry_ref[...] = cum[:, tm - 1:tm]


def _proj_odd(x2, w_in, f_bias, q_norm_g, w_q_up, kv_norm_g, w_kv_up, seq, tm):
    t, d = x2.shape
    sizes = (MLA_Q_RANK, MLA_KV_RANK, MLA_ROPE, ATT_WIDTH, ATT_WIDTH, ATT_WIDTH, ATT_HEADS)
    o = 0
    parts = []
    for n in sizes:
        parts.append(w_in[:, o:o + n])
        o += n
    wcq, wckv, wkpe, wq, wk, wv, wf = parts
    half = MLA_ROPE // 2
    pad = LANES - MLA_NOPE - MLA_ROPE
    pad_l = jnp.zeros((d, MLA_NOPE), F32)
    pad_r = jnp.zeros((d, pad), F32)
    wkpe_p = jnp.concatenate([pad_l, wkpe, pad_r], axis=1)
    wkps_p = jnp.concatenate([pad_l, -wkpe[:, half:], wkpe[:, :half], pad_r], axis=1)
    wq = wq * (1.0 / math.sqrt(ATT_HEAD_DIM))
    ws = [wcq.astype(BF16), wckv.astype(BF16), wkpe_p.astype(BF16), wkps_p.astype(BF16),
          wq.astype(BF16), wk.astype(BF16), wv.astype(BF16), wf.T.astype(BF16)]
    fb = f_bias.reshape(ATT_HEADS, 1).astype(F32)
    inv_freq = ROPE_THETA ** (-(jnp.arange(0, MLA_ROPE, 2, dtype=F32) / MLA_ROPE))
    ang = jnp.arange(seq, dtype=F32)[:, None] * inv_freq[None, :]
    cos, sin = jnp.cos(ang), jnp.sin(ang)
    cos1 = jnp.concatenate([jnp.ones((seq, MLA_NOPE), F32), cos, cos, jnp.zeros((seq, pad), F32)], axis=1)
    sin1 = jnp.concatenate([jnp.zeros((seq, MLA_NOPE), F32), sin, sin, jnp.zeros((seq, pad), F32)], axis=1)
    wqu = w_q_up.reshape(MLA_Q_RANK, ATT_HEADS, MLA_NOPE + MLA_ROPE)
    zq = lambda n: jnp.zeros((MLA_Q_RANK, ATT_HEADS, n), F32)
    wq_main = jnp.concatenate([wqu, zq(pad)], axis=2).reshape(MLA_Q_RANK, ATT_HEADS * LANES)
    wq_swap = jnp.concatenate([zq(MLA_NOPE), -wqu[:, :, MLA_NOPE + half:], wqu[:, :, MLA_NOPE:MLA_NOPE + half],
                               zq(pad)], axis=2).reshape(MLA_Q_RANK, ATT_HEADS * LANES)
    wkv = w_kv_up.reshape(MLA_KV_RANK, ATT_HEADS, MLA_NOPE + MLA_V)
    wk_up = jnp.concatenate([wkv[:, :, :MLA_NOPE], jnp.zeros((MLA_KV_RANK, ATT_HEADS, LANES - MLA_NOPE), F32)],
                            axis=2).reshape(MLA_KV_RANK, ATT_HEADS * LANES)
    wv_up = wkv[:, :, MLA_NOPE:].reshape(MLA_KV_RANK, ATT_HEADS * MLA_V)
    mla = [q_norm_g.reshape(1, -1).astype(F32), kv_norm_g.reshape(1, -1).astype(F32),
           wq_main.astype(BF16), wq_swap.astype(BF16), wk_up.astype(BF16), wv_up.astype(BF16),
           jnp.triu(jnp.ones((tm, tm), BF16))]
    ns = seq // tm
    row = lambda n: pl.BlockSpec((tm, n), lambda i: (i, 0))
    tab = pl.BlockSpec((tm, LANES), lambda i: (i % ns, 0))
    full = lambda a: pl.BlockSpec(a.shape, lambda i: (0, 0))
    return pl.pallas_call(
        functools.partial(_proj_odd_kernel, tiles_per_seq=ns),
        grid=(t // tm,),
        in_specs=[row(d)] + [full(w) for w in ws] + [full(fb), tab, tab] + [full(a) for a in mla],
        out_specs=[row(ATT_WIDTH), row(ATT_WIDTH), row(ATT_WIDTH),
                   pl.BlockSpec((ATT_HEADS, tm), lambda i: (0, i)),
                   row(ATT_HEADS * LANES), row(ATT_HEADS * LANES), row(ATT_HEADS * MLA_V)],
        out_shape=[jax.ShapeDtypeStruct((t, ATT_WIDTH), BF16),
                   jax.ShapeDtypeStruct((t, ATT_WIDTH), BF16),
                   jax.ShapeDtypeStruct((t, ATT_WIDTH), BF16),
                   jax.ShapeDtypeStruct((ATT_HEADS, t), F32),
                   jax.ShapeDtypeStruct((t, ATT_HEADS * LANES), BF16),
                   jax.ShapeDtypeStruct((t, ATT_HEADS * LANES), BF16),
                   jax.ShapeDtypeStruct((t, ATT_HEADS * MLA_V), BF16)],
        scratch_shapes=[pltpu.VMEM((ATT_HEADS, 1), F32)],
        compiler_params=_cparams(("arbitrary",)),
        name="proj_odd",
    )(x2, *ws, fb, cos1, sin1, *mla)


def _ssd_kernel(xbc_ref, z_ref, dt_ref, dtt_ref, cw_ref, cb_ref, dtb_ref, dtbt_ref, alog_ref, alogt_ref,
                dskip_ref, ng_ref, o_ref, buf_ref, st_ref, y_ref):
    c = pl.program_id(1)
    L = SSD_CHUNK

    @pl.when(c == 0)
    def _():
        buf_ref[...] = jnp.zeros_like(buf_ref)
        st_ref[...] = jnp.zeros_like(st_ref)

    cur = xbc_ref[...]
    both = jnp.concatenate([buf_ref[(c + 1) % 2], cur], axis=0)
    shifts = range(1, SSD_CONV)
    sel = jnp.concatenate([(_iota((L, 2 * L), 1) == _iota((L, 2 * L), 0) + (L - k)) for k in shifts],
                          axis=0).astype(BF16)
    shifted = _dot(sel, both)
    acc = cb_ref[...] + cur.astype(F32) * cw_ref[SSD_CONV - 1:SSD_CONV, :]
    for n, k in enumerate(shifts):
        acc = acc + shifted[n * L:(n + 1) * L] * cw_ref[SSD_CONV - 1 - k:SSD_CONV - k, :]
    buf_ref[c % 2] = cur
    xbc = _silu(acc)
    b_mat = xbc[:, SSD_INNER:SSD_INNER + SSD_BC]
    c_mat = xbc[:, SSD_INNER + SSD_BC:]
    bt = jnp.transpose(b_mat).astype(BF16)
    c_bf = c_mat.astype(BF16)

    dt = _softplus(dt_ref[...] + dtb_ref[...])
    dtt = _softplus(dtt_ref[...] + dtbt_ref[...])
    da = dt * (-jnp.exp(alog_ref[...]))
    dat = dtt * (-jnp.exp(alogt_ref[...]))
    row = _iota((L, L), 0)
    col = _iota((L, L), 1)
    causal = col <= row
    a_cum = _dot01_left(causal.astype(BF16), da)
    a_cumt = _dot01_right(dat, (row <= col).astype(BF16))
    a_end = a_cum[L - 1:L, :]
    low_half = _iota((1, LANES), 1) < SSD_HEAD_DIM

    def pair_cols(m, h0):
        n = m.shape[0]
        return jnp.where(low_half, jnp.broadcast_to(m[:, h0:h0 + 1], (n, LANES)),
                         jnp.broadcast_to(m[:, h0 + 1:h0 + 2], (n, LANES)))

    sumsq = jnp.zeros((L, 1), F32)
    for g in range(SSD_GROUPS):
        bt_g = bt[g * SSD_STATE:(g + 1) * SSD_STATE, :]
        c_g = c_bf[:, g * SSD_STATE:(g + 1) * SSD_STATE]
        cb = _dot(c_g, bt_g)
        for pp in range(SSD_HEADS // SSD_GROUPS // 2):
            p = g * (SSD_HEADS // SSD_GROUPS // 2) + pp
            h0 = 2 * p
            lanes = slice(p * LANES, (p + 1) * LANES)
            xs = xbc[:, lanes]
            xs_bf = xs.astype(BF16)
            yd = []
            for h in (h0, h0 + 1):
                seg = a_cum[:, h:h + 1] - a_cumt[h:h + 1, :]
                dec = jnp.where(causal, jnp.exp(jnp.minimum(seg, 0.0)), 0.0)
                m = cb * dec * dtt[h:h + 1, :]
                yd.append(_dot(m.astype(BF16), xs_bf))
            y = jnp.where(low_half, yd[0], yd[1])
            a_p = pair_cols(a_cum, h0)
            dt_p = pair_cols(dt, h0)
            end_p = pair_cols(a_end, h0)
            st = st_ref[p]
            y = y + _dot(c_g, st.astype(BF16)) * jnp.exp(a_p)
            xw = xs * (jnp.exp(end_p - a_p) * dt_p)
            st_ref[p] = st * jnp.exp(end_p) + _dot(bt_g, xw.astype(BF16))
            y = y + dskip_ref[:, lanes] * xs
            y = y * _silu(z_ref[:, lanes].astype(F32))
            y_ref[:, lanes] = y
            sumsq = sumsq + jnp.sum(y * y, axis=1, keepdims=True)
    inv = lax.rsqrt(sumsq * (1.0 / SSD_INNER) + RMS_EPS)
    o_ref[...] = (y_ref[...] * inv * ng_ref[...]).astype(o_ref.dtype)


def _ssd(xbc, z, dt, dtt, conv_w, conv_b, dt_bias, a_log, d_skip, norm_g, batch, seq):
    t = xbc.shape[0]
    L = SSD_CHUNK
    nc = seq // L
    row = lambda n: pl.BlockSpec((L, n), lambda b, c: (b * nc + c, 0))
    full = lambda a: pl.BlockSpec(a.shape, lambda b, c: (0,) * a.ndim)
    params = [conv_w.astype(F32), conv_b.reshape(1, -1).astype(F32),
              dt_bias.reshape(1, -1).astype(F32), dt_bias.reshape(-1, 1).astype(F32),
              a_log.reshape(1, -1).astype(F32), a_log.reshape(-1, 1).astype(F32),
              jnp.repeat(d_skip.astype(F32), SSD_HEAD_DIM).reshape(1, -1),
              norm_g.reshape(1, -1).astype(F32)]
    return pl.pallas_call(
        _ssd_kernel,
        grid=(batch, nc),
        in_specs=[row(SSD_CONV_DIM), row(SSD_INNER), row(SSD_HEADS),
                  pl.BlockSpec((SSD_HEADS, L), lambda b, c: (0, b * nc + c))] + [full(a) for a in params],
        out_specs=row(SSD_INNER),
        out_shape=jax.ShapeDtypeStruct((t, SSD_INNER), BF16),
        scratch_shapes=[pltpu.VMEM((2, L, SSD_CONV_DIM), BF16),
                        pltpu.VMEM((SSD_HEADS // 2, SSD_STATE, LANES), F32),
                        pltpu.VMEM((L, SSD_INNER), F32)],
        compiler_params=_cparams(("parallel", "arbitrary")),
        name="ssd",
    )(xbc, z, dt, dtt, *params)


def _attention_block(i, k_norm, refs, mode, tq, tk, diag_split):
    if mode == "fox":
        q_ref, k_ref, v_ref, fh_ref, o_ref, acc_ref, m_ref, l_ref = refs
    else:
        q_ref, k_ref, v_ref, o_ref, acc_ref, m_ref, l_ref = refs
        fh_ref = None
    n_diag = tq // tk
    lane = _iota((1, LANES), 1)
    low_half = lane < ATT_HEAD_DIM
    row = _iota((tq, tk), 0)
    col = _iota((tq, tk), 1)

    def diag_mask(dd):
        c = col + dd * tk
        if mode == "sb":
            return c < row
        if mode == "mla":
            return (c // MLA_CHUNK) <= (row // MLA_CHUNK)
        return c <= row

    if mode == "sb":
        u_mat = (_iota((tk, tk), 0) > _iota((tk, tk), 1)).astype(BF16)

    q_rows = pl.ds(pl.multiple_of(i * tq, tq), tq)
    q = q_ref[q_rows, :]
    if mode == "mla":
        qs = [q[:, 0:LANES], q[:, LANES:2 * LANES]]
    else:
        zero = jnp.zeros_like(q)
        qs = [jnp.where(low_half, q, zero), jnp.where(low_half, zero, q)]

    if mode == "fox":
        fh_q = [fh_ref[hh:hh + 1, pl.ds(pl.multiple_of(i * tq, tq), LANES)][:, 0:1] for hh in range(2)]

    def rep(x, width):
        return jnp.concatenate([x] * (width // LANES), axis=1) if width > LANES else x

    def tile(chunks, split, first):
        rows = tq // split
        groups = [(hh, slice(rb * rows, (rb + 1) * rows), rb) for hh in range(2) for rb in range(split)]
        loaded = []
        for j, dd in chunks:
            k0 = pl.multiple_of(j * tk, tk)
            k_t = k_ref[pl.ds(k0, tk), :]
            k_h = [k_t[:, 0:LANES], k_t[:, LANES:2 * LANES]] if mode == "mla" else [k_t, k_t]
            bias = None
            if mode == "fox":
                bias = [fh_q[hh] - fh_ref[hh:hh + 1, pl.ds(k0, tk)] for hh in range(2)]
            loaded.append((k_h, v_ref[pl.ds(k0, tk), :], bias, None if dd is None else diag_mask(dd)))
        parts = []
        for hh, rs, rb in groups:
            ps_ = []
            for c, (j, dd) in enumerate(chunks):
                nk = tk if dd is None else min(tk, (rb + 1) * rows - dd * tk)
                if nk > 0:
                    ps_.append((c, nk, None if dd is None else loaded[c][3][rs, 0:nk]))
            parts.append(ps_)
        flat = [(g, c, nk, m) for g, ps_ in enumerate(parts) for c, nk, m in ps_]
        zs = [_dot_nt(qs[groups[g][0]][groups[g][1]], loaded[c][0][groups[g][0]][0:nk]) for g, c, nk, m in flat]
        if mode == "sb":
            if first:
                runs = [jnp.zeros((rows, LANES), F32) for _ in groups]
                accs = [None for _ in groups]
            else:
                runs = [m_ref[hh, rs] for hh, rs, _ in groups]
                accs = [acc_ref[hh, rs] for hh, rs, _ in groups]
            sps = [_softplus(z) for z in zs]
            l1ms = [-sp if m is None else jnp.where(m, -sp, 0.0) for sp, (_, _, _, m) in zip(sps, flat)]
            css = [_dot(l1m.astype(BF16), u_mat[0:nk, 0:nk]) for l1m, (_, _, nk, _) in zip(l1ms, flat)]
            tots = [cs[:, 0:1] + l1m[:, 0:1] for cs, l1m in zip(css, l1ms)]
            seen = []
            cur = list(runs)
            for n, (g, c, nk, m) in enumerate(flat):
                seen.append(cur[g])
                cur[g] = cur[g] + tots[n]
            ws = [jnp.exp((z - sp) + cs + rep(run, nk))
                  for z, sp, cs, run, (_, _, nk, _) in zip(zs, sps, css, seen, flat)]
            ws = [w if m is None else jnp.where(m, w, 0.0) for w, (_, _, _, m) in zip(ws, flat)]
            pvs = [_dot(w.astype(BF16), loaded[c][1][0:nk]) for w, (_, c, nk, _) in zip(ws, flat)]
            for n, (g, c, nk, m) in enumerate(flat):
                accs[g] = pvs[n] if accs[g] is None else accs[g] + pvs[n]
            for (hh, rs, _), acc, run in zip(groups, accs, cur):
                acc_ref[hh, rs] = acc
                m_ref[hh, rs] = run
        else:
            if mode == "fox":
                zs = [z + loaded[c][2][groups[g][0]][:, 0:nk] for z, (g, c, nk, m) in zip(zs, flat)]
            zs = [z if m is None else jnp.where(m, z, NEG_BIG) for z, (_, _, _, m) in zip(zs, flat)]
            if first:
                m_news = [jnp.full((rows, LANES), NEG_BIG, F32) for _ in groups]
            else:
                m_olds = [m_ref[hh, rs] for hh, rs, _ in groups]
                m_news = list(m_olds)
            for z, (g, _, _, _) in zip(zs, flat):
                m_news[g] = jnp.maximum(m_news[g], jnp.max(z, axis=1, keepdims=True))
            ps = [jnp.exp(z - rep(m_news[g], nk)) for z, (g, _, nk, _) in zip(zs, flat)]
            pvs = [_dot(p.astype(BF16), loaded[c][1][0:nk]) for p, (_, c, nk, _) in zip(ps, flat)]
            if first:
                l_news = [None for _ in groups]
                acc_news = [None for _ in groups]
            else:
                alphas = [jnp.exp(m_old - m_new) for m_old, m_new in zip(m_olds, m_news)]
                l_news = [alpha * l_ref[hh, rs] for alpha, (hh, rs, _) in zip(alphas, groups)]
                acc_news = [alpha * acc_ref[hh, rs] for alpha, (hh, rs, _) in zip(alphas, groups)]
            for p, pv, (g, _, _, _) in zip(ps, pvs, flat):
                ls = jnp.sum(p, axis=1, keepdims=True)
                l_news[g] = jnp.broadcast_to(ls, (rows, LANES)) if l_news[g] is None else l_news[g] + ls
                acc_news[g] = pv if acc_news[g] is None else acc_news[g] + pv
            for (hh, rs, _), m_new, l_new, acc in zip(groups, m_news, l_news, acc_news):
                l_ref[hh, rs] = l_new
                acc_ref[hh, rs] = acc
                m_ref[hh, rs] = m_new

    def more(j_done):
        if mode == "sb":
            return (jnp.max(m_ref[...]) > -EXP_ZERO).astype(jnp.int32)
        if mode == "fox":
            kl = pl.multiple_of(jnp.maximum(j_done - 1, 0) * tk, tk)
            best = None
            for hh in range(2):
                fh_last = fh_ref[hh:hh + 1, pl.ds(kl, tk)][:, tk - 1:tk]
                v = jnp.max(qk_bound - m_ref[hh], axis=(0, 1), keepdims=True) + (fh_q[hh] - fh_last)
                best = v if best is None else jnp.maximum(best, v)
            return (jnp.max(best) > -EXP_ZERO).astype(jnp.int32)
        return jnp.int32(1)

    if mode == "fox":
        qf = q.astype(F32)
        qk_bound = jnp.sqrt(jnp.sum(qf * qf, axis=1, keepdims=True)) * k_norm

    n_off = i * n_diag
    diag_chunks = [(n_off + dd, dd) for dd in reversed(range(n_diag))]

    @pl.when(i == 0)
    def _():
        tile(diag_chunks, diag_split, True)

    @pl.when(i > 0)
    def _():
        tile(diag_chunks + [(n_off - 1, None)], diag_split, True)

    def cond(c):
        return jnp.logical_and(c[0] < n_off, c[1] > 0)

    def body(c):
        j = n_off - 1 - c[0]
        tile([(j, None)], 1, False)
        return c[0] + 1, more(j)
    lax.while_loop(cond, body, (jnp.int32(1), more(jnp.maximum(n_off - 1, 0))))
    if mode == "sb":
        out = [acc_ref[0], acc_ref[1]]
    else:
        out = [acc_ref[hh] / l_ref[hh] for hh in range(2)]
    o_ref[q_rows, :] = jnp.where(low_half, out[0], out[1]).astype(o_ref.dtype)


def _attention_kernel(*refs, mode, tq, tk, diag_split, nq):
    k_norm = None
    if mode == "fox":
        kf = refs[1][...].astype(F32)
        n2 = jnp.max(jnp.sum(kf * kf, axis=1, keepdims=True), axis=0, keepdims=True)
        k_norm = jnp.broadcast_to(jnp.sqrt(n2), (1, LANES))

    def q_block(i, carry):
        _attention_block(i, k_norm, refs, mode, tq, tk, diag_split)
        return carry
    lax.fori_loop(0, nq, q_block, 0)


def _attention(mode, q, k, v, batch, seq, fh=None):
    t = v.shape[0]
    tq = min(ATT_TQ[mode], seq)
    tk = min(ATT_TK[mode], tq)
    nq = seq // tq
    qk_w = 2 * LANES if mode == "mla" else LANES
    in_specs = [pl.BlockSpec((seq, qk_w), lambda b, h: (b, h)),
                pl.BlockSpec((seq, qk_w), lambda b, h: (b, h)),
                pl.BlockSpec((seq, LANES), lambda b, h: (b, h))]
    args = [q, k, v]
    scratch = [pltpu.VMEM((2, tq, LANES), F32), pltpu.VMEM((2, tq, LANES), F32), pltpu.VMEM((2, tq, LANES), F32)]
    if mode == "fox":
        in_specs.append(pl.BlockSpec((None, 2, seq), lambda b, h: (h, 0, b)))
        args.append(fh.reshape(HEAD_PAIRS, 2, t))
    return pl.pallas_call(
        functools.partial(_attention_kernel, mode=mode, tq=tq, tk=tk, diag_split=max(1, tq // ATT_DIAG_ROWS), nq=nq),
        grid=(batch, HEAD_PAIRS),
        in_specs=in_specs,
        out_specs=pl.BlockSpec((seq, LANES), lambda b, h: (b, h)),
        out_shape=jax.ShapeDtypeStruct((t, ATT_WIDTH), BF16),
        scratch_shapes=scratch,
        compiler_params=_cparams(("parallel", "parallel")),
        name="attn_" + mode,
    )(*args)


def _layer_norm_rows(r, g, b):
    mu = jnp.mean(r, axis=1, keepdims=True)
    d = r - mu
    var = jnp.mean(d * d, axis=1, keepdims=True)
    return d * lax.rsqrt(var + LN_EPS) * g + b


def _router(x, w_hi, w_lo, bias):
    tm = x.shape[0]
    x_hi = x.astype(BF16)
    x_lo = (x - x_hi.astype(F32)).astype(BF16)
    logits = _dot(x_hi, w_hi) + (_dot(x_hi, w_lo) + _dot(x_lo, w_hi)) + bias
    e_idx = _iota((tm, LANES), 1)
    e_pos = e_idx.astype(F32)

    def first_lane(hit):
        return jnp.min(jnp.where(hit, e_pos, float(LANES)), axis=1, keepdims=True).astype(jnp.int32)

    is_group = jnp.logical_and(e_idx >= MOE_EXPERTS, e_idx < MOE_EXPERTS + MOE_GROUPS)
    gl = jnp.where(is_group, logits, -jnp.inf)
    g_max = jnp.max(gl, axis=1, keepdims=True)
    g_p = 1.0 / jnp.sum(jnp.exp(gl - g_max), axis=1, keepdims=True)
    g_sel = first_lane(gl == g_max) - MOE_EXPERTS
    in_group = (e_idx // MOE_EPG) == g_sel
    masked = jnp.where(in_group, logits, -jnp.inf)
    m1 = jnp.max(masked, axis=1, keepdims=True)
    i1 = first_lane(masked == m1)
    masked2 = jnp.where(e_idx == i1, -jnp.inf, masked)
    m2 = jnp.max(masked2, axis=1, keepdims=True)
    i2 = first_lane(masked2 == m2)
    e2 = jnp.exp(m2 - m1)
    w1 = g_p / (1.0 + e2)
    w2 = w1 * e2
    gate = jnp.where(e_idx == i1, w1, 0.0) + jnp.where(e_idx == i2, w2, 0.0)
    lo = jnp.minimum(i1, i2) - g_sel * MOE_EPG
    hi = jnp.maximum(i1, i2) - g_sel * MOE_EPG
    pair = ((lo * (2 * MOE_EPG - 1 - lo)) >> 1) + (hi - lo - 1)
    return gate, g_sel * MOE_PAIRS + pair


def _outproj_ln_router_kernel(a_ref, b_ref, x_ref, wa_ref, wb_ref, g_ref, beta_ref, wr_hi_ref, wr_lo_ref, br_ref,
                              tri_ref, xg_ref, bucket_ref, rank_ref, count_ref, run_ref):
    i = pl.program_id(0)
    tm, d = x_ref.shape

    @pl.when(i == 0)
    def _():
        run_ref[...] = jnp.zeros_like(run_ref)

    y = _dot(a_ref[...], wa_ref[...]) + _dot(b_ref[...], wb_ref[...])
    x1 = _layer_norm_rows(DEEPNORM_ALPHA * x_ref[...] + y, g_ref[...], beta_ref[...])
    gate, bucket = _router(x1, wr_hi_ref[...], wr_lo_ref[...], br_ref[...])
    xg_ref[:, 0:d] = x1
    xg_ref[:, d:d + LANES] = gate
    bucket_ref[...] = bucket
    onehot = _iota((tm, LANES), 1) == bucket
    prefix = _dot(tri_ref[...], jnp.where(onehot, 1.0, 0.0).astype(BF16))
    before = run_ref[...]
    rank = jnp.sum(jnp.where(onehot, prefix + before, 0.0), axis=1, keepdims=True) - 1.0
    rank_ref[...] = rank.astype(jnp.int32)
    run_ref[...] = before + prefix[tm - 1:tm, :]
    count_ref[...] = run_ref[...]


def _outproj_ln_router(a, b, x2, w_out, ln_g, ln_b, w_group, b_group, w_expert, b_expert, tm):
    t, d = x2.shape
    ka, kb = a.shape[1], b.shape[1]
    pad = LANES - MOE_EXPERTS - MOE_GROUPS
    w_r = jnp.pad(jnp.concatenate([w_expert, w_group], axis=1).astype(F32), ((0, 0), (0, pad)))
    b_r = jnp.pad(jnp.concatenate([b_expert, b_group]).astype(F32), (0, pad)).reshape(1, LANES)
    w_r_hi = w_r.astype(BF16)
    consts = [w_out[:ka].astype(BF16), w_out[ka:].astype(BF16),
              ln_g.reshape(1, -1).astype(F32), ln_b.reshape(1, -1).astype(F32),
              w_r_hi, (w_r - w_r_hi.astype(F32)).astype(BF16), b_r,
              jnp.tril(jnp.ones((tm, tm), BF16))]
    row = lambda n: pl.BlockSpec((tm, n), lambda i: (i, 0))
    full = lambda c: pl.BlockSpec(c.shape, lambda i: (0, 0))
    return pl.pallas_call(
        _outproj_ln_router_kernel,
        grid=(t // tm,),
        in_specs=[row(ka), row(kb), row(d)] + [full(c) for c in consts],
        out_specs=[row(d + LANES), row(1), row(1), pl.BlockSpec((1, LANES), lambda i: (0, 0))],
        out_shape=[jax.ShapeDtypeStruct((t, d + LANES), F32),
                   jax.ShapeDtypeStruct((t, 1), jnp.int32),
                   jax.ShapeDtypeStruct((t, 1), jnp.int32),
                   jax.ShapeDtypeStruct((1, LANES), F32)],
        scratch_shapes=[pltpu.VMEM((1, LANES), F32)],
        compiler_params=_cparams(("arbitrary",)),
        name="outproj_ln_router",
    )(a, b, x2, *consts)


def _moe_plan(bucket, rank, counts, t):
    counts = counts[0, :MOE_BUCKETS].astype(jnp.int32)
    padded = ((counts + MOE_TILE - 1) // MOE_TILE) * MOE_TILE
    ends = jnp.cumsum(padded)
    starts = ends - padded
    pos = jnp.take(starts, bucket[:, 0]) + rank[:, 0]
    n_tiles = t // MOE_TILE + MOE_BUCKETS
    tile_start = jnp.arange(n_tiles, dtype=jnp.int32) * MOE_TILE
    tile_bucket = jnp.minimum(jnp.sum((tile_start[:, None] >= ends[None, :]).astype(jnp.int32), axis=1),
                              MOE_BUCKETS - 1)
    pairs = [(a, b) for a in range(MOE_EPG) for b in range(a + 1, MOE_EPG)]
    first = jnp.asarray([g * MOE_EPG + a for g in range(MOE_GROUPS) for a, _ in pairs], jnp.int32)
    second = jnp.asarray([g * MOE_EPG + b for g in range(MOE_GROUPS) for _, b in pairs], jnp.int32)
    n_used = (ends[MOE_BUCKETS - 1] // MOE_TILE).reshape(1)
    return pos, ends, jnp.take(first, tile_bucket), jnp.take(second, tile_bucket), n_used, n_tiles


def _moe_scatter_kernel(ends_ref, pos_ref, xg_ref, xs_hbm, zero_ref, sem, zero_sem):
    n = pos_ref.shape[0]

    @pl.when(pl.program_id(0) == 0)
    def _():
        zero_ref[...] = jnp.zeros_like(zero_ref)
        total = ends_ref[MOE_BUCKETS - 1]

        def zero_copy(row0):
            return pltpu.make_async_copy(zero_ref, xs_hbm.at[pl.ds(pl.multiple_of(row0, MOE_TILE), MOE_TILE)],
                                         zero_sem)

        jobs = [(ends_ref[b] >= MOE_TILE, ends_ref[b] - MOE_TILE) for b in range(MOE_BUCKETS)]
        jobs += [(total + u * MOE_TILE < xs_hbm.shape[0], total + u * MOE_TILE) for u in range(MOE_BUCKETS)]
        for wanted, row0 in jobs:
            @pl.when(wanted)
            def _():
                zero_copy(row0).start()
        for wanted, row0 in jobs:
            @pl.when(wanted)
            def _():
                zero_copy(row0).wait()

    def body(r, c):
        pltpu.make_async_copy(xg_ref.at[pl.ds(r, 1)], xs_hbm.at[pl.ds(pos_ref[r], 1)], sem).start()
        return c
    lax.fori_loop(0, n, body, 0, unroll=DMA_UNROLL)
    pltpu.make_async_copy(xg_ref, xs_hbm.at[pl.ds(0, n)], sem).wait()


def _moe_scatter(xg, pos, ends, n_rows, chunk):
    t, w = xg.shape
    return pl.pallas_call(
        _moe_scatter_kernel,
        grid=(t // chunk,),
        in_specs=[pl.BlockSpec(memory_space=pltpu.SMEM),
                  pl.BlockSpec((chunk,), lambda i: (i,), memory_space=pltpu.SMEM),
                  pl.BlockSpec((chunk, w), lambda i: (i, 0))],
        out_specs=pl.BlockSpec(memory_space=pl.ANY),
        out_shape=jax.ShapeDtypeStruct((n_rows, w), xg.dtype),
        scratch_shapes=[pltpu.VMEM((MOE_TILE, w), xg.dtype), pltpu.SemaphoreType.DMA(()),
                        pltpu.SemaphoreType.DMA(())],
        compiler_params=_cparams(("arbitrary",)),
        name="moe_scatter",
    )(ends, pos, xg)


def _moe_ffn_kernel(ea_ref, eb_ref, nu_ref, xs_ref, wga_ref, wua_ref, wda_ref, wgb_ref, wub_ref, wdb_ref, y_ref):
    i = pl.program_id(0)
    d = wga_ref.shape[0]

    @pl.when(i < nu_ref[0])
    def _():
        x = xs_ref[:, 0:d].astype(BF16)
        gate = xs_ref[:, d:d + LANES]
        lane = _iota(gate.shape, 1)
        y = None
        for e_ref, wg_ref, wu_ref, wd_ref in ((ea_ref, wga_ref, wua_ref, wda_ref), (eb_ref, wgb_ref, wub_ref, wdb_ref)):
            w = jnp.sum(jnp.where(lane == e_ref[i], gate, 0.0), axis=1, keepdims=True)
            hid = _silu(_dot(x, wg_ref[...])) * _dot(x, wu_ref[...]) * w
            part = _dot(hid.astype(BF16), wd_ref[...])
            y = part if y is None else y + part
        y_ref[...] = y

    @pl.when(i >= nu_ref[0])
    def _():
        y_ref[...] = jnp.zeros_like(y_ref)


def _moe_ffn(xs, ea, eb, n_used, n_tiles, w_gate, w_up, w_down):
    d = w_gate.shape[1]
    wg, wu, wd = w_gate.astype(BF16), w_up.astype(BF16), w_down.astype(BF16)
    up_a = pl.BlockSpec((None, d, MOE_FF), lambda i, ea, eb, nu: (ea[i], 0, 0))
    dn_a = pl.BlockSpec((None, MOE_FF, d), lambda i, ea, eb, nu: (ea[i], 0, 0))
    up_b = pl.BlockSpec((None, d, MOE_FF), lambda i, ea, eb, nu: (eb[i], 0, 0))
    dn_b = pl.BlockSpec((None, MOE_FF, d), lambda i, ea, eb, nu: (eb[i], 0, 0))
    return pl.pallas_call(
        _moe_ffn_kernel,
        grid_spec=pltpu.PrefetchScalarGridSpec(
            num_scalar_prefetch=3,
            grid=(n_tiles,),
            in_specs=[pl.BlockSpec((MOE_TILE, d + LANES), lambda i, ea, eb, nu: (jnp.minimum(i, nu[0] - 1), 0)),
                      up_a, up_a, dn_a, up_b, up_b, dn_b],
            out_specs=pl.BlockSpec((MOE_TILE, d), lambda i, ea, eb, nu: (i, 0))),
        out_shape=jax.ShapeDtypeStruct((n_tiles * MOE_TILE, d), F32),
        compiler_params=_cparams(("arbitrary",)),
        name="moe_ffn",
    )(ea, eb, n_used, xs, wg, wu, wd, wg, wu, wd)


def _moe_combine_kernel(pos_ref, pos_next_ref, y_hbm, x_ref, g_ref, beta_ref, o_ref, buf_ref, sem):
    i = pl.program_id(0)
    n = pos_ref.shape[0]
    slot = i % 2

    def gather(p_ref, s):
        def body(r, c):
            pltpu.make_async_copy(y_hbm.at[pl.ds(p_ref[r], 1)], buf_ref.at[s, pl.ds(r, 1)], sem.at[s]).start()
            return c
        lax.fori_loop(0, n, body, 0, unroll=DMA_UNROLL)

    @pl.when(i == 0)
    def _():
        gather(pos_ref, slot)

    @pl.when(i + 1 < pl.num_programs(0))
    def _():
        gather(pos_next_ref, 1 - slot)

    pltpu.make_async_copy(y_hbm.at[pl.ds(0, n)], buf_ref.at[slot], sem.at[slot]).wait()
    r = DEEPNORM_ALPHA * x_ref[...] + buf_ref[slot]
    o_ref[...] = _layer_norm_rows(r, g_ref[...], beta_ref[...])


def _moe_combine(y_sorted, pos, xg, ln_g, ln_b, tm):
    t = pos.shape[0]
    d = y_sorted.shape[1]
    ln = [ln_g.reshape(1, -1).astype(F32), ln_b.reshape(1, -1).astype(F32)]
    last = t // tm - 1
    return pl.pallas_call(
        _moe_combine_kernel,
        grid=(t // tm,),
        in_specs=[pl.BlockSpec((tm,), lambda i: (i,), memory_space=pltpu.SMEM),
                  pl.BlockSpec((tm,), lambda i: (jnp.minimum(i + 1, last),), memory_space=pltpu.SMEM),
                  pl.BlockSpec(memory_space=pl.ANY),
                  pl.BlockSpec((tm, d), lambda i: (i, 0))]
        + [pl.BlockSpec(c.shape, lambda i: (0, 0)) for c in ln],
        out_specs=pl.BlockSpec((tm, d), lambda i: (i, 0)),
        out_shape=jax.ShapeDtypeStruct((t, d), F32),
        scratch_shapes=[pltpu.VMEM((2, tm, d), F32), pltpu.SemaphoreType.DMA((2,))],
        compiler_params=_cparams(("arbitrary",)),
        name="moe_combine",
    )(pos, pos, y_sorted, xg, *ln)


def _mixer_out_and_moe(a, b, x2, w_out, ln1_g, ln1_b, w_group, b_group, w_expert, b_expert,
                       w_gate, w_up, w_down, ln2_g, ln2_b, tm):
    t = x2.shape[0]
    xg, bucket, rank, counts = _outproj_ln_router(a, b, x2, w_out, ln1_g, ln1_b, w_group, b_group,
                                                  w_expert, b_expert, tm)
    pos, ends, ea, eb, n_used, n_tiles = _moe_plan(bucket, rank, counts, t)
    xs = _moe_scatter(xg, pos, ends, n_tiles * MOE_TILE, _row_tile(t, 1024))
    y_sorted = _moe_ffn(xs, ea, eb, n_used, n_tiles, w_gate, w_up, w_down)
    return _moe_combine(y_sorted, pos, xg, ln2_g, ln2_b, tm)


def _row_tile(n, pref):
    tm = min(pref, n)
    assert n % tm == 0
    return tm


def kernel(x, ev_w_in, ev_conv_w, ev_conv_b, ev_dt_bias, ev_a_log, ev_d_skip, ev_norm_g, ev_w_out, od_w_in, od_q_norm_g, od_w_q_up, od_kv_norm_g, od_w_kv_up, od_f_bias, od_w_out, ln1_g, ln1_b, ln2_g, ln2_b, moe_w_group, moe_b_group, moe_w_expert, moe_b_expert, moe_w_gate, moe_w_up, moe_w_down):
    batch, seq, d = x.shape
    t = batch * seq
    assert seq % SSD_CHUNK == 0 and seq % LANES == 0 and t % MOE_TILE == 0
    tm_proj = _row_tile(seq, 512)
    x2 = x.reshape(t, d)

    z, xbc, dt, dtt, q, k, v = _proj_even(x2, ev_w_in[0], tm_proj)
    y_ssd = _ssd(xbc, z, dt, dtt, ev_conv_w[0], ev_conv_b[0], ev_dt_bias[0], ev_a_log[0], ev_d_skip[0],
                 ev_norm_g[0], batch, seq)
    y_sb = _attention("sb", q, k, v, batch, seq)
    x2 = _mixer_out_and_moe(y_ssd, y_sb, x2, ev_w_out[0], ln1_g[0], ln1_b[0], moe_w_group[0], moe_b_group[0],
                            moe_w_expert[0], moe_b_expert[0], moe_w_gate[0], moe_w_up[0], moe_w_down[0],
                            ln2_g[0], ln2_b[0], tm_proj)

    q, k, v, fh, qm, km, vm = _proj_odd(x2, od_w_in[0], od_f_bias[0], od_q_norm_g[0], od_w_q_up[0],
                                        od_kv_norm_g[0], od_w_kv_up[0], seq, tm_proj)
    y_mla = _attention("mla", qm, km, vm, batch, seq)
    y_fox = _attention("fox", q, k, v, batch, seq, fh=fh)
    x2 = _mixer_out_and_moe(y_mla, y_fox, x2, od_w_out[0], ln1_g[1], ln1_b[1], moe_w_group[1], moe_b_group[1],
                            moe_w_expert[1], moe_b_expert[1], moe_w_gate[1], moe_w_up[1], moe_w_down[1],
                            ln2_g[1], ln2_b[1], tm_proj)
    return x2.reshape(batch, seq, d)
```

```python
import functools
import math

import jax
import jax.numpy as jnp
from jax import lax
from jax.experimental import pallas as pl
from jax.experimental.pallas import tpu as pltpu

F32 = jnp.float32
BF16 = jnp.bfloat16

SSD_HEADS = 16
SSD_HEAD_DIM = 64
SSD_INNER = SSD_HEADS * SSD_HEAD_DIM
SSD_GROUPS = 2
SSD_STATE = 128
SSD_CONV = 4
SSD_BC = SSD_GROUPS * SSD_STATE
SSD_CONV_DIM = SSD_INNER + 2 * SSD_BC
ATT_HEADS = 8
ATT_HEAD_DIM = 64
ATT_WIDTH = ATT_HEADS * ATT_HEAD_DIM
HEAD_PAIRS = ATT_HEADS // 2
MLA_Q_RANK = 256
MLA_KV_RANK = 128
MLA_NOPE = 64
MLA_ROPE = 32
MLA_V = 64
MLA_CHUNK = 64
ROPE_THETA = 10000.0
MOE_GROUPS = 4
MOE_EPG = 4
MOE_EXPERTS = MOE_GROUPS * MOE_EPG
MOE_FF = 256
DEPTH = 2
DEEPNORM_ALPHA = (2.0 * DEPTH) ** 0.25
LN_EPS = 1e-5
RMS_EPS = 1e-6

LANES = 128
SSD_CHUNK = 128
ATT_TQ = {"sb": 256, "mla": 512, "fox": 256}
ATT_TK = {"sb": 256, "mla": 512, "fox": 256}
ATT_DIAG_ROWS = 128
NEG_BIG = -1e30
EXP_ZERO = 110.0
MOE_TILE = 512
DMA_UNROLL = 8
MOE_PAIRS = MOE_EPG * (MOE_EPG - 1) // 2
MOE_BUCKETS = MOE_GROUPS * MOE_PAIRS
VMEM_LIMIT = 56 * 1024 * 1024


def _cparams(sem):
    return pltpu.CompilerParams(dimension_semantics=sem, vmem_limit_bytes=VMEM_LIMIT)


def _dot(a, b):
    return jnp.dot(a, b, preferred_element_type=F32)


def _dot_nt(a, b):
    return lax.dot_general(a, b, (((1,), (1,)), ((), ())), preferred_element_type=F32)


def _split3(x):
    hi = x.astype(BF16)
    r1 = x - hi.astype(F32)
    mid = r1.astype(BF16)
    lo = (r1 - mid.astype(F32)).astype(BF16)
    return hi, mid, lo


def _dot01_right(x, m01):
    hi, mid, lo = _split3(x)
    return _dot(hi, m01) + _dot(mid, m01) + _dot(lo, m01)


def _dot01_left(m01, x):
    hi, mid, lo = _split3(x)
    return _dot(m01, hi) + _dot(m01, mid) + _dot(m01, lo)


def _softplus(x):
    return jnp.maximum(x, 0.0) + jnp.log(1.0 + jnp.exp(-jnp.abs(x)))


def _silu(x):
    return x * (1.0 / (1.0 + jnp.exp(-x)))


def _iota(shape, dim):
    return lax.broadcasted_iota(jnp.int32, shape, dim)


def _proj_even_kernel(x_ref, wz_ref, wx_ref, wdt_ref, wdtt_ref, wq_ref, wk_ref, wv_ref,
                      z_ref, xbc_ref, dt_ref, dtt_ref, q_ref, k_ref, v_ref):
    x = x_ref[...].astype(BF16)
    z_ref[...] = _dot(x, wz_ref[...]).astype(z_ref.dtype)
    xbc_ref[...] = _dot(x, wx_ref[...]).astype(xbc_ref.dtype)
    dt_ref[...] = _dot(x, wdt_ref[...])
    dtt_ref[...] = _dot_nt(wdtt_ref[...], x)
    q_ref[...] = _dot(x, wq_ref[...]).astype(q_ref.dtype)
    k_ref[...] = _dot(x, wk_ref[...]).astype(k_ref.dtype)
    v_ref[...] = _dot(x, wv_ref[...]).astype(v_ref.dtype)


def _proj_even(x2, w_in, tm):
    t, d = x2.shape
    o = 0
    parts = []
    for n in (SSD_INNER, SSD_CONV_DIM, SSD_HEADS, ATT_WIDTH, ATT_WIDTH, ATT_WIDTH):
        parts.append(w_in[:, o:o + n])
        o += n
    wz, wx, wdt, wq, wk, wv = parts
    wq = wq * (1.0 / math.sqrt(ATT_HEAD_DIM))
    ws = [wz.astype(BF16), wx.astype(BF16), wdt.astype(BF16), wdt.T.astype(BF16),
          wq.astype(BF16), wk.astype(BF16), wv.astype(BF16)]
    row = lambda n: pl.BlockSpec((tm, n), lambda i: (i, 0))
    full = lambda a: pl.BlockSpec(a.shape, lambda i: (0, 0))
    return pl.pallas_call(
        _proj_even_kernel,
        grid=(t // tm,),
        in_specs=[row(d)] + [full(w) for w in ws],
        out_specs=[row(SSD_INNER), row(SSD_CONV_DIM), row(SSD_HEADS),
                   pl.BlockSpec((SSD_HEADS, tm), lambda i: (0, i)),
                   row(ATT_WIDTH), row(ATT_WIDTH), row(ATT_WIDTH)],
        out_shape=[jax.ShapeDtypeStruct((t, SSD_INNER), BF16),
                   jax.ShapeDtypeStruct((t, SSD_CONV_DIM), BF16),
                   jax.ShapeDtypeStruct((t, SSD_HEADS), F32),
                   jax.ShapeDtypeStruct((SSD_HEADS, t), F32),
                   jax.ShapeDtypeStruct((t, ATT_WIDTH), BF16),
                   jax.ShapeDtypeStruct((t, ATT_WIDTH), BF16),
                   jax.ShapeDtypeStruct((t, ATT_WIDTH), BF16)],
        compiler_params=_cparams(("parallel",)),
        name="proj_even",
    )(x2, *ws)


def _mla_rows(cq, ckv, kpe, kps, cos1, sin1, qg, kvg, wqm, wqs, wk, wv):
    def rms(x, g):
        return x * lax.rsqrt(jnp.mean(x * x, axis=1, keepdims=True) + RMS_EPS) * g

    cqn = rms(cq, qg).astype(BF16)
    ckvn = rms(ckv, kvg).astype(BF16)
    cos8 = jnp.concatenate([cos1] * ATT_HEADS, axis=1)
    sin8 = jnp.concatenate([sin1] * ATT_HEADS, axis=1)
    scale = 1.0 / math.sqrt(MLA_NOPE + MLA_ROPE)
    q = (_dot(cqn, wqm) * cos8 + _dot(cqn, wqs) * sin8) * scale
    k_rope = kpe * cos1 + kps * sin1
    k = _dot(ckvn, wk) + jnp.concatenate([k_rope] * ATT_HEADS, axis=1)
    return q, k, _dot(ckvn, wv)


def _proj_odd_kernel(x_ref, wlat_ref, wq_ref, wk_ref, wv_ref, wft_ref, fb_ref,
                     cos_ref, sin_ref, qg_ref, kvg_ref, wqm_ref, wqs_ref, wkup_ref, wvup_ref, upper_ref,
                     q_ref, k_ref, v_ref, fh_ref, qm_ref, km_ref, vm_ref, carry_ref, *, tiles_per_seq):
    i = pl.program_id(0)
    tm = x_ref.shape[0]
    x = x_ref[...].astype(BF16)
    q_ref[...] = _dot(x, wq_ref[...]).astype(q_ref.dtype)
    k_ref[...] = _dot(x, wk_ref[...]).astype(k_ref.dtype)
    v_ref[...] = _dot(x, wv_ref[...]).astype(v_ref.dtype)
    lat = _dot(x, wlat_ref[...])
    kv0 = MLA_Q_RANK + MLA_KV_RANK
    qm, km, vm = _mla_rows(lat[:, 0:MLA_Q_RANK], lat[:, MLA_Q_RANK:kv0], lat[:, kv0:kv0 + LANES],
                           lat[:, kv0 + LANES:kv0 + 2 * LANES], cos_ref[...], sin_ref[...], qg_ref[...], kvg_ref[...],
                           wqm_ref[...], wqs_ref[...], wkup_ref[...], wvup_ref[...])
    qm_ref[...] = qm.astype(qm_ref.dtype)
    km_ref[...] = km.astype(km_ref.dtype)
    vm_ref[...] = vm.astype(vm_ref.dtype)

    @pl.when(i % tiles_per_seq == 0)
    def _():
        carry_ref[...] = jnp.zeros_like(carry_ref)

    f_raw = _dot_nt(wft_ref[...], x) + fb_ref[...]
    log_f = -_softplus(-f_raw)
    cum = _dot01_right(log_f, upper_ref[...]) + carry_ref[...]
    fh_ref[...] = cum
    carry_ref[...] = cum[:, tm - 1:tm]


def _proj_odd(x2, w_in, f_bias, q_norm_g, w_q_up, kv_norm_g, w_kv_up, seq, tm):
    t, d = x2.shape
    sizes = (MLA_Q_RANK, MLA_KV_RANK, MLA_ROPE, ATT_WIDTH, ATT_WIDTH, ATT_WIDTH, ATT_HEADS)
    o = 0
    parts = []
    for n in sizes:
        parts.append(w_in[:, o:o + n])
        o += n
    wcq, wckv, wkpe, wq, wk, wv, wf = parts
    half = MLA_ROPE // 2
    pad = LANES - MLA_NOPE - MLA_ROPE
    pad_l = jnp.zeros((d, MLA_NOPE), F32)
    pad_r = jnp.zeros((d, pad), F32)
    wkpe_p = jnp.concatenate([pad_l, wkpe, pad_r], axis=1)
    wkps_p = jnp.concatenate([pad_l, -wkpe[:, half:], wkpe[:, :half], pad_r], axis=1)
    wq = wq * (1.0 / math.sqrt(ATT_HEAD_DIM))
    ws = [jnp.concatenate([wcq, wckv, wkpe_p, wkps_p], axis=1).astype(BF16),
          wq.astype(BF16), wk.astype(BF16), wv.astype(BF16), wf.T.astype(BF16)]
    fb = f_bias.reshape(ATT_HEADS, 1).astype(F32)
    inv_freq = ROPE_THETA ** (-(jnp.arange(0, MLA_ROPE, 2, dtype=F32) / MLA_ROPE))
    ang = jnp.arange(seq, dtype=F32)[:, None] * inv_freq[None, :]
    cos, sin = jnp.cos(ang), jnp.sin(ang)
    cos1 = jnp.concatenate([jnp.ones((seq, MLA_NOPE), F32), cos, cos, jnp.zeros((seq, pad), F32)], axis=1)
    sin1 = jnp.concatenate([jnp.zeros((seq, MLA_NOPE), F32), sin, sin, jnp.zeros((seq, pad), F32)], axis=1)
    wqu = w_q_up.reshape(MLA_Q_RANK, ATT_HEADS, MLA_NOPE + MLA_ROPE)
    zq = lambda n: jnp.zeros((MLA_Q_RANK, ATT_HEADS, n), F32)
    wq_main = jnp.concatenate([wqu, zq(pad)], axis=2).reshape(MLA_Q_RANK, ATT_HEADS * LANES)
    wq_swap = jnp.concatenate([zq(MLA_NOPE), -wqu[:, :, MLA_NOPE + half:], wqu[:, :, MLA_NOPE:MLA_NOPE + half],
                               zq(pad)], axis=2).reshape(MLA_Q_RANK, ATT_HEADS * LANES)
    wkv = w_kv_up.reshape(MLA_KV_RANK, ATT_HEADS, MLA_NOPE + MLA_V)
    wk_up = jnp.concatenate([wkv[:, :, :MLA_NOPE], jnp.zeros((MLA_KV_RANK, ATT_HEADS, LANES - MLA_NOPE), F32)],
                            axis=2).reshape(MLA_KV_RANK, ATT_HEADS * LANES)
    wv_up = wkv[:, :, MLA_NOPE:].reshape(MLA_KV_RANK, ATT_HEADS * MLA_V)
    mla = [q_norm_g.reshape(1, -1).astype(F32), kv_norm_g.reshape(1, -1).astype(F32),
           wq_main.astype(BF16), wq_swap.astype(BF16), wk_up.astype(BF16), wv_up.astype(BF16),
           jnp.triu(jnp.ones((tm, tm), BF16))]
    ns = seq // tm
    row = lambda n: pl.BlockSpec((tm, n), lambda i: (i, 0))
    tab = pl.BlockSpec((tm, LANES), lambda i: (i % ns, 0))
    full = lambda a: pl.BlockSpec(a.shape, lambda i: (0, 0))
    return pl.pallas_call(
        functools.partial(_proj_odd_kernel, tiles_per_seq=ns),
        grid=(t // tm,),
        in_specs=[row(d)] + [full(w) for w in ws] + [full(fb), tab, tab] + [full(a) for a in mla],
        out_specs=[row(ATT_WIDTH), row(ATT_WIDTH), row(ATT_WIDTH),
                   pl.BlockSpec((ATT_HEADS, tm), lambda i: (0, i)),
                   row(ATT_HEADS * LANES), row(ATT_HEADS * LANES), row(ATT_HEADS * MLA_V)],
        out_shape=[jax.ShapeDtypeStruct((t, ATT_WIDTH), BF16),
                   jax.ShapeDtypeStruct((t, ATT_WIDTH), BF16),
                   jax.ShapeDtypeStruct((t, ATT_WIDTH), BF16),
                   jax.ShapeDtypeStruct((ATT_HEADS, t), F32),
                   jax.ShapeDtypeStruct((t, ATT_HEADS * LANES), BF16),
                   jax.ShapeDtypeStruct((t, ATT_HEADS * LANES), BF16),
                   jax.ShapeDtypeStruct((t, ATT_HEADS * MLA_V), BF16)],
        scratch_shapes=[pltpu.VMEM((ATT_HEADS, 1), F32)],
        compiler_params=_cparams(("arbitrary",)),
        name="proj_odd",
    )(x2, *ws, fb, cos1, sin1, *mla)


def _ssd_kernel(xbc_ref, z_ref, dt_ref, dtt_ref, cw_ref, cb_ref, dtb_ref, dtbt_ref, alog_ref, alogt_ref,
                dskip_ref, ng_ref, o_ref, buf_ref, st_ref, y_ref):
    c = pl.program_id(1)
    L = SSD_CHUNK

    @pl.when(c == 0)
    def _():
        buf_ref[...] = jnp.zeros_like(buf_ref)
        st_ref[...] = jnp.zeros_like(st_ref)

    cur = xbc_ref[...]
    both = jnp.concatenate([buf_ref[(c + 1) % 2], cur], axis=0)
    shifts = range(1, SSD_CONV)
    sel = jnp.concatenate([(_iota((L, 2 * L), 1) == _iota((L, 2 * L), 0) + (L - k)) for k in shifts],
                          axis=0).astype(BF16)
    shifted = _dot(sel, both)
    acc = cb_ref[...] + cur.astype(F32) * cw_ref[SSD_CONV - 1:SSD_CONV, :]
    for n, k in enumerate(shifts):
        acc = acc + shifted[n * L:(n + 1) * L] * cw_ref[SSD_CONV - 1 - k:SSD_CONV - k, :]
    buf_ref[c % 2] = cur
    xbc = _silu(acc)
    b_mat = xbc[:, SSD_INNER:SSD_INNER + SSD_BC]
    c_mat = xbc[:, SSD_INNER + SSD_BC:]
    bt = jnp.transpose(b_mat).astype(BF16)
    c_bf = c_mat.astype(BF16)

    dt = _softplus(dt_ref[...] + dtb_ref[...])
    dtt = _softplus(dtt_ref[...] + dtbt_ref[...])
    da = dt * (-jnp.exp(alog_ref[...]))
    dat = dtt * (-jnp.exp(alogt_ref[...]))
    row = _iota((L, L), 0)
    col = _iota((L, L), 1)
    causal = col <= row
    a_cum = _dot01_left(causal.astype(BF16), da)
    a_cumt = _dot01_right(dat, (row <= col).astype(BF16))
    a_end = a_cum[L - 1:L, :]
    low_half = _iota((1, LANES), 1) < SSD_HEAD_DIM

    def pair_cols(m, h0):
        n = m.shape[0]
        return jnp.where(low_half, jnp.broadcast_to(m[:, h0:h0 + 1], (n, LANES)),
                         jnp.broadcast_to(m[:, h0 + 1:h0 + 2], (n, LANES)))

    sumsq = jnp.zeros((L, 1), F32)
    for g in range(SSD_GROUPS):
        bt_g = bt[g * SSD_STATE:(g + 1) * SSD_STATE, :]
        c_g = c_bf[:, g * SSD_STATE:(g + 1) * SSD_STATE]
        cb = _dot(c_g, bt_g)
        for pp in range(SSD_HEADS // SSD_GROUPS // 2):
            p = g * (SSD_HEADS // SSD_GROUPS // 2) + pp
            h0 = 2 * p
            lanes = slice(p * LANES, (p + 1) * LANES)
            xs = xbc[:, lanes]
            xs_bf = xs.astype(BF16)
            yd = []
            for h in (h0, h0 + 1):
                seg = a_cum[:, h:h + 1] - a_cumt[h:h + 1, :]
                dec = jnp.where(causal, jnp.exp(jnp.minimum(seg, 0.0)), 0.0)
                m = cb * dec * dtt[h:h + 1, :]
                yd.append(_dot(m.astype(BF16), xs_bf))
            y = jnp.where(low_half, yd[0], yd[1])
            a_p = pair_cols(a_cum, h0)
            dt_p = pair_cols(dt, h0)
            end_p = pair_cols(a_end, h0)
            st = st_ref[p]
            y = y + _dot(c_g, st.astype(BF16)) * jnp.exp(a_p)
            xw = xs * (jnp.exp(end_p - a_p) * dt_p)
            st_ref[p] = st * jnp.exp(end_p) + _dot(bt_g, xw.astype(BF16))
            y = y + dskip_ref[:, lanes] * xs
            y = y * _silu(z_ref[:, lanes].astype(F32))
            y_ref[:, lanes] = y
            sumsq = sumsq + jnp.sum(y * y, axis=1, keepdims=True)
    inv = lax.rsqrt(sumsq * (1.0 / SSD_INNER) + RMS_EPS)
    o_ref[...] = (y_ref[...] * inv * ng_ref[...]).astype(o_ref.dtype)


def _ssd(xbc, z, dt, dtt, conv_w, conv_b, dt_bias, a_log, d_skip, norm_g, batch, seq):
    t = xbc.shape[0]
    L = SSD_CHUNK
    nc = seq // L
    row = lambda n: pl.BlockSpec((L, n), lambda b, c: (b * nc + c, 0))
    full = lambda a: pl.BlockSpec(a.shape, lambda b, c: (0,) * a.ndim)
    params = [conv_w.astype(F32), conv_b.reshape(1, -1).astype(F32),
              dt_bias.reshape(1, -1).astype(F32), dt_bias.reshape(-1, 1).astype(F32),
              a_log.reshape(1, -1).astype(F32), a_log.reshape(-1, 1).astype(F32),
              jnp.repeat(d_skip.astype(F32), SSD_HEAD_DIM).reshape(1, -1),
              norm_g.reshape(1, -1).astype(F32)]
    return pl.pallas_call(
        _ssd_kernel,
        grid=(batch, nc),
        in_specs=[row(SSD_CONV_DIM), row(SSD_INNER), row(SSD_HEADS),
                  pl.BlockSpec((SSD_HEADS, L), lambda b, c: (0, b * nc + c))] + [full(a) for a in params],
        out_specs=row(SSD_INNER),
        out_shape=jax.ShapeDtypeStruct((t, SSD_INNER), BF16),
        scratch_shapes=[pltpu.VMEM((2, L, SSD_CONV_DIM), BF16),
                        pltpu.VMEM((SSD_HEADS // 2, SSD_STATE, LANES), F32),
                        pltpu.VMEM((L, SSD_INNER), F32)],
        compiler_params=_cparams(("parallel", "arbitrary")),
        name="ssd",
    )(xbc, z, dt, dtt, *params)


def _attention_block(i, k_norm, refs, mode, tq, tk, diag_split):
    if mode == "fox":
        q_ref, k_ref, v_ref, fh_ref, o_ref, acc_ref, m_ref, l_ref = refs
    else:
        q_ref, k_ref, v_ref, o_ref, acc_ref, m_ref, l_ref = refs
        fh_ref = None
    n_diag = tq // tk
    lane = _iota((1, LANES), 1)
    low_half = lane < ATT_HEAD_DIM
    row = _iota((tq, tk), 0)
    col = _iota((tq, tk), 1)

    def diag_mask(dd):
        c = col + dd * tk
        if mode == "sb":
            return c < row
        if mode == "mla":
            return (c // MLA_CHUNK) <= (row // MLA_CHUNK)
        return c <= row

    if mode == "sb":
        u_mat = (_iota((tk, tk), 0) > _iota((tk, tk), 1)).astype(BF16)

    q_rows = pl.ds(pl.multiple_of(i * tq, tq), tq)
    q = q_ref[q_rows, :]
    if mode == "mla":
        qs = [q[:, 0:LANES], q[:, LANES:2 * LANES]]
    else:
        zero = jnp.zeros_like(q)
        qs = [jnp.where(low_half, q, zero), jnp.where(low_half, zero, q)]

    if mode == "fox":
        fh_q = [fh_ref[hh:hh + 1, pl.ds(pl.multiple_of(i * tq, tq), LANES)][:, 0:1] for hh in range(2)]

    def rep(x, width):
        return jnp.concatenate([x] * (width // LANES), axis=1) if width > LANES else x

    def tile(chunks, split, first):
        rows = tq // split
        groups = [(hh, slice(rb * rows, (rb + 1) * rows), rb) for hh in range(2) for rb in range(split)]
        loaded = []
        for j, dd in chunks:
            k0 = pl.multiple_of(j * tk, tk)
            k_t = k_ref[pl.ds(k0, tk), :]
            k_h = [k_t[:, 0:LANES], k_t[:, LANES:2 * LANES]] if mode == "mla" else [k_t, k_t]
            bias = None
            if mode == "fox":
                bias = [fh_q[hh] - fh_ref[hh:hh + 1, pl.ds(k0, tk)] for hh in range(2)]
            loaded.append((k_h, v_ref[pl.ds(k0, tk), :], bias, None if dd is None else diag_mask(dd)))
        parts = []
        for hh, rs, rb in groups:
            ps_ = []
            for c, (j, dd) in enumerate(chunks):
                nk = tk if dd is None else min(tk, (rb + 1) * rows - dd * tk)
                if nk > 0:
                    ps_.append((c, nk, None if dd is None else loaded[c][3][rs, 0:nk]))
            parts.append(ps_)
        flat = [(g, c, nk, m) for g, ps_ in enumerate(parts) for c, nk, m in ps_]
        zs = [_dot_nt(qs[groups[g][0]][groups[g][1]], loaded[c][0][groups[g][0]][0:nk]) for g, c, nk, m in flat]
        if mode == "sb":
            if first:
                runs = [jnp.zeros((rows, LANES), F32) for _ in groups]
                accs = [None for _ in groups]
            else:
                runs = [m_ref[hh, rs] for hh, rs, _ in groups]
                accs = [acc_ref[hh, rs] for hh, rs, _ in groups]
            sps = [_softplus(z) for z in zs]
            l1ms = [-sp if m is None else jnp.where(m, -sp, 0.0) for sp, (_, _, _, m) in zip(sps, flat)]
            css = [_dot(l1m.astype(BF16), u_mat[0:nk, 0:nk]) for l1m, (_, _, nk, _) in zip(l1ms, flat)]
            tots = [cs[:, 0:1] + l1m[:, 0:1] for cs, l1m in zip(css, l1ms)]
            seen = []
            cur = list(runs)
            for n, (g, c, nk, m) in enumerate(flat):
                seen.append(cur[g])
                cur[g] = cur[g] + tots[n]
            ws = [jnp.exp((z - sp) + cs + rep(run, nk))
                  for z, sp, cs, run, (_, _, nk, _) in zip(zs, sps, css, seen, flat)]
            ws = [w if m is None else jnp.where(m, w, 0.0) for w, (_, _, _, m) in zip(ws, flat)]
            pvs = [_dot(w.astype(BF16), loaded[c][1][0:nk]) for w, (_, c, nk, _) in zip(ws, flat)]
            for n, (g, c, nk, m) in enumerate(flat):
                accs[g] = pvs[n] if accs[g] is None else accs[g] + pvs[n]
            for (hh, rs, _), acc, run in zip(groups, accs, cur):
                acc_ref[hh, rs] = acc
                m_ref[hh, rs] = run
        else:
            if mode == "fox":
                zs = [z + loaded[c][2][groups[g][0]][:, 0:nk] for z, (g, c, nk, m) in zip(zs, flat)]
            zs = [z if m is None else jnp.where(m, z, NEG_BIG) for z, (_, _, _, m) in zip(zs, flat)]
            if first:
                m_news = [jnp.full((rows, LANES), NEG_BIG, F32) for _ in groups]
            else:
                m_olds = [m_ref[hh, rs] for hh, rs, _ in groups]
                m_news = list(m_olds)
            for z, (g, _, _, _) in zip(zs, flat):
                m_news[g] = jnp.maximum(m_news[g], jnp.max(z, axis=1, keepdims=True))
            ps = [jnp.exp(z - rep(m_news[g], nk)) for z, (g, _, nk, _) in zip(zs, flat)]
            pvs = [_dot(p.astype(BF16), loaded[c][1][0:nk]) for p, (_, c, nk, _) in zip(ps, flat)]
            if first:
                l_news = [None for _ in groups]
                acc_news = [None for _ in groups]
            else:
                alphas = [jnp.exp(m_old - m_new) for m_old, m_new in zip(m_olds, m_news)]
                l_news = [alpha * l_ref[hh, rs] for alpha, (hh, rs, _) in zip(alphas, groups)]
                acc_news = [alpha * acc_ref[hh, rs] for alpha, (hh, rs, _) in zip(alphas, groups)]
            for p, pv, (g, _, _, _) in zip(ps, pvs, flat):
                ls = jnp.sum(p, axis=1, keepdims=True)
                l_news[g] = jnp.broadcast_to(ls, (rows, LANES)) if l_news[g] is None else l_news[g] + ls
                acc_news[g] = pv if acc_news[g] is None else acc_news[g] + pv
            for (hh, rs, _), m_new, l_new, acc in zip(groups, m_news, l_news, acc_news):
                l_ref[hh, rs] = l_new
                acc_ref[hh, rs] = acc
                m_ref[hh, rs] = m_new

    def more(j_done):
        if mode == "sb":
            return (jnp.max(m_ref[...]) > -EXP_ZERO).astype(jnp.int32)
        if mode == "fox":
            kl = pl.multiple_of(jnp.maximum(j_done - 1, 0) * tk, tk)
            best = None
            for hh in range(2):
                fh_last = fh_ref[hh:hh + 1, pl.ds(kl, tk)][:, tk - 1:tk]
                v = jnp.max(qk_bound - m_ref[hh], axis=(0, 1), keepdims=True) + (fh_q[hh] - fh_last)
                best = v if best is None else jnp.maximum(best, v)
            return (jnp.max(best) > -EXP_ZERO).astype(jnp.int32)
        return jnp.int32(1)

    if mode == "fox":
        qf = q.astype(F32)
        qk_bound = jnp.sqrt(jnp.sum(qf * qf, axis=1, keepdims=True)) * k_norm

    n_off = i * n_diag
    diag_chunks = [(n_off + dd, dd) for dd in reversed(range(n_diag))]

    @pl.when(i == 0)
    def _():
        tile(diag_chunks, diag_split, True)

    @pl.when(i > 0)
    def _():
        tile(diag_chunks + [(n_off - 1, None)], diag_split, True)

    def cond(c):
        return jnp.logical_and(c[0] < n_off, c[1] > 0)

    def body(c):
        j = n_off - 1 - c[0]
        tile([(j, None)], 1, False)
        return c[0] + 1, more(j)
    lax.while_loop(cond, body, (jnp.int32(1), more(jnp.maximum(n_off - 1, 0))))
    if mode == "sb":
        out = [acc_ref[0], acc_ref[1]]
    else:
        out = [acc_ref[hh] / l_ref[hh] for hh in range(2)]
    o_ref[q_rows, :] = jnp.where(low_half, out[0], out[1]).astype(o_ref.dtype)


def _attention_kernel(*refs, mode, tq, tk, diag_split, nq):
    k_norm = None
    if mode == "fox":
        kf = refs[1][...].astype(F32)
        n2 = jnp.max(jnp.sum(kf * kf, axis=1, keepdims=True), axis=0, keepdims=True)
        k_norm = jnp.broadcast_to(jnp.sqrt(n2), (1, LANES))

    def q_block(i, carry):
        _attention_block(i, k_norm, refs, mode, tq, tk, diag_split)
        return carry
    lax.fori_loop(0, nq, q_block, 0)


def _attention(mode, q, k, v, batch, seq, fh=None):
    t = v.shape[0]
    tq = min(ATT_TQ[mode], seq)
    tk = min(ATT_TK[mode], tq)
    nq = seq // tq
    qk_w = 2 * LANES if mode == "mla" else LANES
    in_specs = [pl.BlockSpec((seq, qk_w), lambda b, h: (b, h)),
                pl.BlockSpec((seq, qk_w), lambda b, h: (b, h)),
                pl.BlockSpec((seq, LANES), lambda b, h: (b, h))]
    args = [q, k, v]
    scratch = [pltpu.VMEM((2, tq, LANES), F32), pltpu.VMEM((2, tq, LANES), F32), pltpu.VMEM((2, tq, LANES), F32)]
    if mode == "fox":
        in_specs.append(pl.BlockSpec((None, 2, seq), lambda b, h: (h, 0, b)))
        args.append(fh.reshape(HEAD_PAIRS, 2, t))
    return pl.pallas_call(
        functools.partial(_attention_kernel, mode=mode, tq=tq, tk=tk, diag_split=max(1, tq // ATT_DIAG_ROWS), nq=nq),
        grid=(batch, HEAD_PAIRS),
        in_specs=in_specs,
        out_specs=pl.BlockSpec((seq, LANES), lambda b, h: (b, h)),
        out_shape=jax.ShapeDtypeStruct((t, ATT_WIDTH), BF16),
        scratch_shapes=scratch,
        compiler_params=_cparams(("parallel", "parallel")),
        name="attn_" + mode,
    )(*args)


def _layer_norm_rows(r, g, b):
    mu = jnp.mean(r, axis=1, keepdims=True)
    d = r - mu
    var = jnp.mean(d * d, axis=1, keepdims=True)
    return d * lax.rsqrt(var + LN_EPS) * g + b


def _router(x, w_both, bias):
    tm = x.shape[0]
    x_hi = x.astype(BF16)
    x_lo = (x - x_hi.astype(F32)).astype(BF16)
    both = _dot(x_hi, w_both)
    logits = both[:, 0:LANES] + (both[:, LANES:2 * LANES] + _dot(x_lo, w_both[:, 0:LANES])) + bias
    e_idx = _iota((tm, LANES), 1)
    e_pos = e_idx.astype(F32)

    def first_lane(hit):
        return jnp.min(jnp.where(hit, e_pos, float(LANES)), axis=1, keepdims=True).astype(jnp.int32)

    is_group = jnp.logical_and(e_idx >= MOE_EXPERTS, e_idx < MOE_EXPERTS + MOE_GROUPS)
    gl = jnp.where(is_group, logits, -jnp.inf)
    g_max = jnp.max(gl, axis=1, keepdims=True)
    g_p = 1.0 / jnp.sum(jnp.exp(gl - g_max), axis=1, keepdims=True)
    g_sel = first_lane(gl == g_max) - MOE_EXPERTS
    in_group = (e_idx // MOE_EPG) == g_sel
    masked = jnp.where(in_group, logits, -jnp.inf)
    m1 = jnp.max(masked, axis=1, keepdims=True)
    i1 = first_lane(masked == m1)
    masked2 = jnp.where(e_idx == i1, -jnp.inf, masked)
    m2 = jnp.max(masked2, axis=1, keepdims=True)
    i2 = first_lane(masked2 == m2)
    e2 = jnp.exp(m2 - m1)
    w1 = g_p / (1.0 + e2)
    w2 = w1 * e2
    gate = jnp.where(e_idx == i1, w1, 0.0) + jnp.where(e_idx == i2, w2, 0.0)
    lo = jnp.minimum(i1, i2) - g_sel * MOE_EPG
    hi = jnp.maximum(i1, i2) - g_sel * MOE_EPG
    pair = ((lo * (2 * MOE_EPG - 1 - lo)) >> 1) + (hi - lo - 1)
    return gate, g_sel * MOE_PAIRS + pair


def _outproj_ln_router_kernel(a_ref, b_ref, x_ref, wa_ref, wb_ref, g_ref, beta_ref, wr_ref, br_ref,
                              tri_ref, xg_ref, bucket_ref, rank_ref, count_ref, run_ref):
    i = pl.program_id(0)
    tm, d = x_ref.shape

    @pl.when(i == 0)
    def _():
        run_ref[...] = jnp.zeros_like(run_ref)

    y = _dot(a_ref[...], wa_ref[...]) + _dot(b_ref[...], wb_ref[...])
    x1 = _layer_norm_rows(DEEPNORM_ALPHA * x_ref[...] + y, g_ref[...], beta_ref[...])
    gate, bucket = _router(x1, wr_ref[...], br_ref[...])
    xg_ref[:, 0:d] = x1
    xg_ref[:, d:d + LANES] = gate
    bucket_ref[...] = bucket
    onehot = _iota((tm, LANES), 1) == bucket
    prefix = _dot(tri_ref[...], jnp.where(onehot, 1.0, 0.0).astype(BF16))
    before = run_ref[...]
    rank = jnp.sum(jnp.where(onehot, prefix + before, 0.0), axis=1, keepdims=True) - 1.0
    rank_ref[...] = rank.astype(jnp.int32)
    run_ref[...] = before + prefix[tm - 1:tm, :]
    count_ref[...] = run_ref[...]


def _outproj_ln_router(a, b, x2, w_out, ln_g, ln_b, w_group, b_group, w_expert, b_expert, tm):
    t, d = x2.shape
    ka, kb = a.shape[1], b.shape[1]
    pad = LANES - MOE_EXPERTS - MOE_GROUPS
    w_r = jnp.pad(jnp.concatenate([w_expert, w_group], axis=1).astype(F32), ((0, 0), (0, pad)))
    b_r = jnp.pad(jnp.concatenate([b_expert, b_group]).astype(F32), (0, pad)).reshape(1, LANES)
    w_r_hi = w_r.astype(BF16)
    consts = [w_out[:ka].astype(BF16), w_out[ka:].astype(BF16),
              ln_g.reshape(1, -1).astype(F32), ln_b.reshape(1, -1).astype(F32),
              jnp.concatenate([w_r_hi, (w_r - w_r_hi.astype(F32)).astype(BF16)], axis=1), b_r,
              jnp.tril(jnp.ones((tm, tm), BF16))]
    row = lambda n: pl.BlockSpec((tm, n), lambda i: (i, 0))
    full = lambda c: pl.BlockSpec(c.shape, lambda i: (0, 0))
    return pl.pallas_call(
        _outproj_ln_router_kernel,
        grid=(t // tm,),
        in_specs=[row(ka), row(kb), row(d)] + [full(c) for c in consts],
        out_specs=[row(d + LANES), row(1), row(1), pl.BlockSpec((1, LANES), lambda i: (0, 0))],
        out_shape=[jax.ShapeDtypeStruct((t, d + LANES), F32),
                   jax.ShapeDtypeStruct((t, 1), jnp.int32),
                   jax.ShapeDtypeStruct((t, 1), jnp.int32),
                   jax.ShapeDtypeStruct((1, LANES), F32)],
        scratch_shapes=[pltpu.VMEM((1, LANES), F32)],
        compiler_params=_cparams(("arbitrary",)),
        name="outproj_ln_router",
    )(a, b, x2, *consts)


def _moe_plan(bucket, rank, counts, t):
    counts = counts[0, :MOE_BUCKETS].astype(jnp.int32)
    padded = ((counts + MOE_TILE - 1) // MOE_TILE) * MOE_TILE
    ends = jnp.cumsum(padded)
    starts = ends - padded
    pos = jnp.take(starts, bucket[:, 0]) + rank[:, 0]
    n_tiles = t // MOE_TILE + MOE_BUCKETS
    tile_start = jnp.arange(n_tiles, dtype=jnp.int32) * MOE_TILE
    tile_bucket = jnp.minimum(jnp.sum((tile_start[:, None] >= ends[None, :]).astype(jnp.int32), axis=1),
                              MOE_BUCKETS - 1)
    pairs = [(a, b) for a in range(MOE_EPG) for b in range(a + 1, MOE_EPG)]
    first = jnp.asarray([g * MOE_EPG + a for g in range(MOE_GROUPS) for a, _ in pairs], jnp.int32)
    second = jnp.asarray([g * MOE_EPG + b for g in range(MOE_GROUPS) for _, b in pairs], jnp.int32)
    n_used = (ends[MOE_BUCKETS - 1] // MOE_TILE).reshape(1)
    return pos, ends, jnp.take(first, tile_bucket), jnp.take(second, tile_bucket), n_used, n_tiles


def _moe_scatter_kernel(ends_ref, pos_ref, xg_ref, xs_hbm, zero_ref, sem, zero_sem):
    n = pos_ref.shape[0]

    @pl.when(pl.program_id(0) == 0)
    def _():
        zero_ref[...] = jnp.zeros_like(zero_ref)
        total = ends_ref[MOE_BUCKETS - 1]

        def zero_copy(row0):
            return pltpu.make_async_copy(zero_ref, xs_hbm.at[pl.ds(pl.multiple_of(row0, MOE_TILE), MOE_TILE)],
                                         zero_sem)

        jobs = [(ends_ref[b] >= MOE_TILE, ends_ref[b] - MOE_TILE) for b in range(MOE_BUCKETS)]
        jobs += [(total + u * MOE_TILE < xs_hbm.shape[0], total + u * MOE_TILE) for u in range(MOE_BUCKETS)]
        for wanted, row0 in jobs:
            @pl.when(wanted)
            def _():
                zero_copy(row0).start()
        for wanted, row0 in jobs:
            @pl.when(wanted)
            def _():
                zero_copy(row0).wait()

    def body(r, c):
        pltpu.make_async_copy(xg_ref.at[pl.ds(r, 1)], xs_hbm.at[pl.ds(pos_ref[r], 1)], sem).start()
        return c
    lax.fori_loop(0, n, body, 0, unroll=DMA_UNROLL)
    pltpu.make_async_copy(xg_ref, xs_hbm.at[pl.ds(0, n)], sem).wait()


def _moe_scatter(xg, pos, ends, n_rows, chunk):
    t, w = xg.shape
    return pl.pallas_call(
        _moe_scatter_kernel,
        grid=(t // chunk,),
        in_specs=[pl.BlockSpec(memory_space=pltpu.SMEM),
                  pl.BlockSpec((chunk,), lambda i: (i,), memory_space=pltpu.SMEM),
                  pl.BlockSpec((chunk, w), lambda i: (i, 0))],
        out_specs=pl.BlockSpec(memory_space=pl.ANY),
        out_shape=jax.ShapeDtypeStruct((n_rows, w), xg.dtype),
        scratch_shapes=[pltpu.VMEM((MOE_TILE, w), xg.dtype), pltpu.SemaphoreType.DMA(()),
                        pltpu.SemaphoreType.DMA(())],
        compiler_params=_cparams(("arbitrary",)),
        name="moe_scatter",
    )(ends, pos, xg)


def _moe_ffn_kernel(ea_ref, eb_ref, nu_ref, xs_ref, wga_ref, wua_ref, wda_ref, wgb_ref, wub_ref, wdb_ref, y_ref):
    i = pl.program_id(0)
    d = wga_ref.shape[0]

    @pl.when(i < nu_ref[0])
    def _():
        x = xs_ref[:, 0:d].astype(BF16)
        gate = xs_ref[:, d:d + LANES]
        lane = _iota(gate.shape, 1)
        y = None
        for e_ref, wg_ref, wu_ref, wd_ref in ((ea_ref, wga_ref, wua_ref, wda_ref), (eb_ref, wgb_ref, wub_ref, wdb_ref)):
            w = jnp.sum(jnp.where(lane == e_ref[i], gate, 0.0), axis=1, keepdims=True)
            hid = _silu(_dot(x, wg_ref[...])) * _dot(x, wu_ref[...]) * w
            part = _dot(hid.astype(BF16), wd_ref[...])
            y = part if y is None else y + part
        y_ref[...] = y

    @pl.when(i >= nu_ref[0])
    def _():
        y_ref[...] = jnp.zeros_like(y_ref)


def _moe_ffn(xs, ea, eb, n_used, n_tiles, w_gate, w_up, w_down):
    d = w_gate.shape[1]
    wg, wu, wd = w_gate.astype(BF16), w_up.astype(BF16), w_down.astype(BF16)
    up_a = pl.BlockSpec((None, d, MOE_FF), lambda i, ea, eb, nu: (ea[i], 0, 0))
    dn_a = pl.BlockSpec((None, MOE_FF, d), lambda i, ea, eb, nu: (ea[i], 0, 0))
    up_b = pl.BlockSpec((None, d, MOE_FF), lambda i, ea, eb, nu: (eb[i], 0, 0))
    dn_b = pl.BlockSpec((None, MOE_FF, d), lambda i, ea, eb, nu: (eb[i], 0, 0))
    return pl.pallas_call(
        _moe_ffn_kernel,
        grid_spec=pltpu.PrefetchScalarGridSpec(
            num_scalar_prefetch=3,
            grid=(n_tiles,),
            in_specs=[pl.BlockSpec((MOE_TILE, d + LANES), lambda i, ea, eb, nu: (jnp.minimum(i, nu[0] - 1), 0)),
                      up_a, up_a, dn_a, up_b, up_b, dn_b],
            out_specs=pl.BlockSpec((MOE_TILE, d), lambda i, ea, eb, nu: (i, 0))),
        out_shape=jax.ShapeDtypeStruct((n_tiles * MOE_TILE, d), F32),
        compiler_params=_cparams(("arbitrary",)),
        name="moe_ffn",
    )(ea, eb, n_used, xs, wg, wu, wd, wg, wu, wd)


def _moe_combine_kernel(pos_ref, pos_next_ref, y_hbm, x_ref, g_ref, beta_ref, o_ref, buf_ref, sem):
    i = pl.program_id(0)
    n = pos_ref.shape[0]
    slot = i % 2

    def gather(p_ref, s):
        def body(r, c):
            pltpu.make_async_copy(y_hbm.at[pl.ds(p_ref[r], 1)], buf_ref.at[s, pl.ds(r, 1)], sem.at[s]).start()
            return c
        lax.fori_loop(0, n, body, 0, unroll=DMA_UNROLL)

    @pl.when(i == 0)
    def _():
        gather(pos_ref, slot)

    @pl.when(i + 1 < pl.num_programs(0))
    def _():
        gather(pos_next_ref, 1 - slot)

    pltpu.make_async_copy(y_hbm.at[pl.ds(0, n)], buf_ref.at[slot], sem.at[slot]).wait()
    r = DEEPNORM_ALPHA * x_ref[...] + buf_ref[slot]
    o_ref[...] = _layer_norm_rows(r, g_ref[...], beta_ref[...])


def _moe_combine(y_sorted, pos, xg, ln_g, ln_b, tm):
    t = pos.shape[0]
    d = y_sorted.shape[1]
    ln = [ln_g.reshape(1, -1).astype(F32), ln_b.reshape(1, -1).astype(F32)]
    last = t // tm - 1
    return pl.pallas_call(
        _moe_combine_kernel,
        grid=(t // tm,),
        in_specs=[pl.BlockSpec((tm,), lambda i: (i,), memory_space=pltpu.SMEM),
                  pl.BlockSpec((tm,), lambda i: (jnp.minimum(i + 1, last),), memory_space=pltpu.SMEM),
                  pl.BlockSpec(memory_space=pl.ANY),
                  pl.BlockSpec((tm, d), lambda i: (i, 0))]
        + [pl.BlockSpec(c.shape, lambda i: (0, 0)) for c in ln],
        out_specs=pl.BlockSpec((tm, d), lambda i: (i, 0)),
        out_shape=jax.ShapeDtypeStruct((t, d), F32),
        scratch_shapes=[pltpu.VMEM((2, tm, d), F32), pltpu.SemaphoreType.DMA((2,))],
        compiler_params=_cparams(("arbitrary",)),
        name="moe_combine",
    )(pos, pos, y_sorted, xg, *ln)


def _mixer_out_and_moe(a, b, x2, w_out, ln1_g, ln1_b, w_group, b_group, w_expert, b_expert,
                       w_gate, w_up, w_down, ln2_g, ln2_b, tm):
    t = x2.shape[0]
    xg, bucket, rank, counts = _outproj_ln_router(a, b, x2, w_out, ln1_g, ln1_b, w_group, b_group,
                                                  w_expert, b_expert, tm)
    pos, ends, ea, eb, n_used, n_tiles = _moe_plan(bucket, rank, counts, t)
    xs = _moe_scatter(xg, pos, ends, n_tiles * MOE_TILE, _row_tile(t, 1024))
    y_sorted = _moe_ffn(xs, ea, eb, n_used, n_tiles, w_gate, w_up, w_down)
    return _moe_combine(y_sorted, pos, xg, ln2_g, ln2_b, tm)


def _row_tile(n, pref):
    tm = min(pref, n)
    assert n % tm == 0
    return tm


def kernel(x, ev_w_in, ev_conv_w, ev_conv_b, ev_dt_bias, ev_a_log, ev_d_skip, ev_norm_g, ev_w_out, od_w_in, od_q_norm_g, od_w_q_up, od_kv_norm_g, od_w_kv_up, od_f_bias, od_w_out, ln1_g, ln1_b, ln2_g, ln2_b, moe_w_group, moe_b_group, moe_w_expert, moe_b_expert, moe_w_gate, moe_w_up, moe_w_down):
    batch, seq, d = x.shape
    t = batch * seq
    assert seq % SSD_CHUNK == 0 and seq % LANES == 0 and t % MOE_TILE == 0
    tm_proj = _row_tile(seq, 512)
    x2 = x.reshape(t, d)

    z, xbc, dt, dtt, q, k, v = _proj_even(x2, ev_w_in[0], tm_proj)
    y_ssd = _ssd(xbc, z, dt, dtt, ev_conv_w[0], ev_conv_b[0], ev_dt_bias[0], ev_a_log[0], ev_d_skip[0],
                 ev_norm_g[0], batch, seq)
    y_sb = _attention("sb", q, k, v, batch, seq)
    x2 = _mixer_out_and_moe(y_ssd, y_sb, x2, ev_w_out[0], ln1_g[0], ln1_b[0], moe_w_group[0], moe_b_group[0],
                            moe_w_expert[0], moe_b_expert[0], moe_w_gate[0], moe_w_up[0], moe_w_down[0],
                            ln2_g[0], ln2_b[0], tm_proj)

    q, k, v, fh, qm, km, vm = _proj_odd(x2, od_w_in[0], od_f_bias[0], od_q_norm_g[0], od_w_q_up[0],
                                        od_kv_norm_g[0], od_w_kv_up[0], seq, tm_proj)
    y_mla = _attention("mla", qm, km, vm, batch, seq)
    y_fox = _attention("fox", q, k, v, batch, seq, fh=fh)
    x2 = _mixer_out_and_moe(y_mla, y_fox, x2, od_w_out[0], ln1_g[1], ln1_b[1], moe_w_group[1], moe_b_group[1],
                            moe_w_expert[1], moe_b_expert[1], moe_w_gate[1], moe_w_up[1], moe_w_down[1],
                            ln2_g[1], ln2_b[1], tm_proj)
    return x2.reshape(batch, seq, d)
```

```python
import functools
import math

import jax
import jax.numpy as jnp
from jax import lax
from jax.experimental import pallas as pl
from jax.experimental.pallas import tpu as pltpu

F32 = jnp.float32
BF16 = jnp.bfloat16

SSD_HEADS = 16
SSD_HEAD_DIM = 64
SSD_INNER = SSD_HEADS * SSD_HEAD_DIM
SSD_GROUPS = 2
SSD_STATE = 128
SSD_CONV = 4
SSD_BC = SSD_GROUPS * SSD_STATE
SSD_CONV_DIM = SSD_INNER + 2 * SSD_BC
ATT_HEADS = 8
ATT_HEAD_DIM = 64
ATT_WIDTH = ATT_HEADS * ATT_HEAD_DIM
HEAD_PAIRS = ATT_HEADS // 2
MLA_Q_RANK = 256
MLA_KV_RANK = 128
MLA_NOPE = 64
MLA_ROPE = 32
MLA_V = 64
MLA_CHUNK = 64
ROPE_THETA = 10000.0
MOE_GROUPS = 4
MOE_EPG = 4
MOE_EXPERTS = MOE_GROUPS * MOE_EPG
MOE_FF = 256
DEPTH = 2
DEEPNORM_ALPHA = (2.0 * DEPTH) ** 0.25
LN_EPS = 1e-5
RMS_EPS = 1e-6

LANES = 128
SSD_CHUNK = 128
SSD_STEP_CHUNKS = 4
ATT_TQ = {"sb": 256, "mla": 512, "fox": 256}
ATT_TK = {"sb": 256, "mla": 512, "fox": 256}
ATT_DIAG_ROWS = 128
NEG_BIG = -1e30
EXP_ZERO = 110.0
MOE_TILE = 512
DMA_UNROLL = 8
MOE_PAIRS = MOE_EPG * (MOE_EPG - 1) // 2
MOE_BUCKETS = MOE_GROUPS * MOE_PAIRS
VMEM_LIMIT = 56 * 1024 * 1024


def _cparams(sem):
    return pltpu.CompilerParams(dimension_semantics=sem, vmem_limit_bytes=VMEM_LIMIT)


def _dot(a, b):
    return jnp.dot(a, b, preferred_element_type=F32)


def _dot_nt(a, b):
    return lax.dot_general(a, b, (((1,), (1,)), ((), ())), preferred_element_type=F32)


def _split3(x):
    hi = x.astype(BF16)
    r1 = x - hi.astype(F32)
    mid = r1.astype(BF16)
    lo = (r1 - mid.astype(F32)).astype(BF16)
    return hi, mid, lo


def _dot01_right(x, m01):
    hi, mid, lo = _split3(x)
    return _dot(hi, m01) + _dot(mid, m01) + _dot(lo, m01)


def _dot01_left(m01, x):
    hi, mid, lo = _split3(x)
    return _dot(m01, hi) + _dot(m01, mid) + _dot(m01, lo)


def _softplus(x):
    return jnp.maximum(x, 0.0) + jnp.log(1.0 + jnp.exp(-jnp.abs(x)))


def _silu(x):
    return x * (1.0 / (1.0 + jnp.exp(-x)))


def _iota(shape, dim):
    return lax.broadcasted_iota(jnp.int32, shape, dim)


def _proj_even_kernel(x_ref, wz_ref, wx_ref, wdt_ref, wdtt_ref, wq_ref, wk_ref, wv_ref,
                      z_ref, xbc_ref, dt_ref, dtt_ref, q_ref, k_ref, v_ref):
    x = x_ref[...].astype(BF16)
    z_ref[...] = _dot(x, wz_ref[...]).astype(z_ref.dtype)
    xbc_ref[...] = _dot(x, wx_ref[...]).astype(xbc_ref.dtype)
    dt_ref[...] = _dot(x, wdt_ref[...])
    dtt_ref[...] = _dot_nt(wdtt_ref[...], x)
    q_ref[...] = _dot(x, wq_ref[...]).astype(q_ref.dtype)
    k_ref[...] = _dot(x, wk_ref[...]).astype(k_ref.dtype)
    v_ref[...] = _dot(x, wv_ref[...]).astype(v_ref.dtype)


def _proj_even(x2, w_in, tm):
    t, d = x2.shape
    o = 0
    parts = []
    for n in (SSD_INNER, SSD_CONV_DIM, SSD_HEADS, ATT_WIDTH, ATT_WIDTH, ATT_WIDTH):
        parts.append(w_in[:, o:o + n])
        o += n
    wz, wx, wdt, wq, wk, wv = parts
    wq = wq * (1.0 / math.sqrt(ATT_HEAD_DIM))
    ws = [wz.astype(BF16), wx.astype(BF16), wdt.astype(BF16), wdt.T.astype(BF16),
          wq.astype(BF16), wk.astype(BF16), wv.astype(BF16)]
    row = lambda n: pl.BlockSpec((tm, n), lambda i: (i, 0))
    full = lambda a: pl.BlockSpec(a.shape, lambda i: (0, 0))
    return pl.pallas_call(
        _proj_even_kernel,
        grid=(t // tm,),
        in_specs=[row(d)] + [full(w) for w in ws],
        out_specs=[row(SSD_INNER), row(SSD_CONV_DIM), row(SSD_HEADS),
                   pl.BlockSpec((SSD_HEADS, tm), lambda i: (0, i)),
                   row(ATT_WIDTH), row(ATT_WIDTH), row(ATT_WIDTH)],
        out_shape=[jax.ShapeDtypeStruct((t, SSD_INNER), BF16),
                   jax.ShapeDtypeStruct((t, SSD_CONV_DIM), BF16),
                   jax.ShapeDtypeStruct((t, SSD_HEADS), F32),
                   jax.ShapeDtypeStruct((SSD_HEADS, t), F32),
                   jax.ShapeDtypeStruct((t, ATT_WIDTH), BF16),
                   jax.ShapeDtypeStruct((t, ATT_WIDTH), BF16),
                   jax.ShapeDtypeStruct((t, ATT_WIDTH), BF16)],
        compiler_params=_cparams(("parallel",)),
        name="proj_even",
    )(x2, *ws)


def _mla_rows(cq, ckv, kpe, kps, cos1, sin1, qg, kvg, wqm, wqs, wk, wv):
    def rms(x, g):
        return x * lax.rsqrt(jnp.mean(x * x, axis=1, keepdims=True) + RMS_EPS) * g

    cqn = rms(cq, qg).astype(BF16)
    ckvn = rms(ckv, kvg).astype(BF16)
    cos8 = jnp.concatenate([cos1] * ATT_HEADS, axis=1)
    sin8 = jnp.concatenate([sin1] * ATT_HEADS, axis=1)
    scale = 1.0 / math.sqrt(MLA_NOPE + MLA_ROPE)
    q = (_dot(cqn, wqm) * cos8 + _dot(cqn, wqs) * sin8) * scale
    k_rope = kpe * cos1 + kps * sin1
    k = _dot(ckvn, wk) + jnp.concatenate([k_rope] * ATT_HEADS, axis=1)
    return q, k, _dot(ckvn, wv)


def _proj_odd_kernel(x_ref, wlat_ref, wq_ref, wk_ref, wv_ref, wft_ref, fb_ref,
                     cos_ref, sin_ref, qg_ref, kvg_ref, wqm_ref, wqs_ref, wkup_ref, wvup_ref, upper_ref,
                     q_ref, k_ref, v_ref, fh_ref, qm_ref, km_ref, vm_ref, carry_ref, *, tiles_per_seq):
    i = pl.program_id(0)
    tm = x_ref.shape[0]
    x = x_ref[...].astype(BF16)
    q_ref[...] = _dot(x, wq_ref[...]).astype(q_ref.dtype)
    k_ref[...] = _dot(x, wk_ref[...]).astype(k_ref.dtype)
    v_ref[...] = _dot(x, wv_ref[...]).astype(v_ref.dtype)
    lat = _dot(x, wlat_ref[...])
    kv0 = MLA_Q_RANK + MLA_KV_RANK
    qm, km, vm = _mla_rows(lat[:, 0:MLA_Q_RANK], lat[:, MLA_Q_RANK:kv0], lat[:, kv0:kv0 + LANES],
                           lat[:, kv0 + LANES:kv0 + 2 * LANES], cos_ref[...], sin_ref[...], qg_ref[...], kvg_ref[...],
                           wqm_ref[...], wqs_ref[...], wkup_ref[...], wvup_ref[...])
    qm_ref[...] = qm.astype(qm_ref.dtype)
    km_ref[...] = km.astype(km_ref.dtype)
    vm_ref[...] = vm.astype(vm_ref.dtype)

    @pl.when(i % tiles_per_seq == 0)
    def _():
        carry_ref[...] = jnp.zeros_like(carry_ref)

    f_raw = _dot_nt(wft_ref[...], x) + fb_ref[...]
    log_f = -_softplus(-f_raw)
    cum = _dot01_right(log_f, upper_ref[...]) + carry_ref[...]
    fh_ref[...] = cum
    carry_ref[...] = cum[:, tm - 1:tm]


def _proj_odd(x2, w_in, f_bias, q_norm_g, w_q_up, kv_norm_g, w_kv_up, seq, tm):
    t, d = x2.shape
    sizes = (MLA_Q_RANK, MLA_KV_RANK, MLA_ROPE, ATT_WIDTH, ATT_WIDTH, ATT_WIDTH, ATT_HEADS)
    o = 0
    parts = []
    for n in sizes:
        parts.append(w_in[:, o:o + n])
        o += n
    wcq, wckv, wkpe, wq, wk, wv, wf = parts
    half = MLA_ROPE // 2
    pad = LANES - MLA_NOPE - MLA_ROPE
    pad_l = jnp.zeros((d, MLA_NOPE), F32)
    pad_r = jnp.zeros((d, pad), F32)
    wkpe_p = jnp.concatenate([pad_l, wkpe, pad_r], axis=1)
    wkps_p = jnp.concatenate([pad_l, -wkpe[:, half:], wkpe[:, :half], pad_r], axis=1)
    wq = wq * (1.0 / math.sqrt(ATT_HEAD_DIM))
    ws = [jnp.concatenate([wcq, wckv, wkpe_p, wkps_p], axis=1).astype(BF16),
          wq.astype(BF16), wk.astype(BF16), wv.astype(BF16), wf.T.astype(BF16)]
    fb = f_bias.reshape(ATT_HEADS, 1).astype(F32)
    inv_freq = ROPE_THETA ** (-(jnp.arange(0, MLA_ROPE, 2, dtype=F32) / MLA_ROPE))
    ang = jnp.arange(seq, dtype=F32)[:, None] * inv_freq[None, :]
    cos, sin = jnp.cos(ang), jnp.sin(ang)
    cos1 = jnp.concatenate([jnp.ones((seq, MLA_NOPE), F32), cos, cos, jnp.zeros((seq, pad), F32)], axis=1)
    sin1 = jnp.concatenate([jnp.zeros((seq, MLA_NOPE), F32), sin, sin, jnp.zeros((seq, pad), F32)], axis=1)
    wqu = w_q_up.reshape(MLA_Q_RANK, ATT_HEADS, MLA_NOPE + MLA_ROPE)
    zq = lambda n: jnp.zeros((MLA_Q_RANK, ATT_HEADS, n), F32)
    wq_main = jnp.concatenate([wqu, zq(pad)], axis=2).reshape(MLA_Q_RANK, ATT_HEADS * LANES)
    wq_swap = jnp.concatenate([zq(MLA_NOPE), -wqu[:, :, MLA_NOPE + half:], wqu[:, :, MLA_NOPE:MLA_NOPE + half],
                               zq(pad)], axis=2).reshape(MLA_Q_RANK, ATT_HEADS * LANES)
    wkv = w_kv_up.reshape(MLA_KV_RANK, ATT_HEADS, MLA_NOPE + MLA_V)
    wk_up = jnp.concatenate([wkv[:, :, :MLA_NOPE], jnp.zeros((MLA_KV_RANK, ATT_HEADS, LANES - MLA_NOPE), F32)],
                            axis=2).reshape(MLA_KV_RANK, ATT_HEADS * LANES)
    wv_up = wkv[:, :, MLA_NOPE:].reshape(MLA_KV_RANK, ATT_HEADS * MLA_V)
    mla = [q_norm_g.reshape(1, -1).astype(F32), kv_norm_g.reshape(1, -1).astype(F32),
           wq_main.astype(BF16), wq_swap.astype(BF16), wk_up.astype(BF16), wv_up.astype(BF16),
           jnp.triu(jnp.ones((tm, tm), BF16))]
    ns = seq // tm
    row = lambda n: pl.BlockSpec((tm, n), lambda i: (i, 0))
    tab = pl.BlockSpec((tm, LANES), lambda i: (i % ns, 0))
    full = lambda a: pl.BlockSpec(a.shape, lambda i: (0, 0))
    return pl.pallas_call(
        functools.partial(_proj_odd_kernel, tiles_per_seq=ns),
        grid=(t // tm,),
        in_specs=[row(d)] + [full(w) for w in ws] + [full(fb), tab, tab] + [full(a) for a in mla],
        out_specs=[row(ATT_WIDTH), row(ATT_WIDTH), row(ATT_WIDTH),
                   pl.BlockSpec((ATT_HEADS, tm), lambda i: (0, i)),
                   row(ATT_HEADS * LANES), row(ATT_HEADS * LANES), row(ATT_HEADS * MLA_V)],
        out_shape=[jax.ShapeDtypeStruct((t, ATT_WIDTH), BF16),
                   jax.ShapeDtypeStruct((t, ATT_WIDTH), BF16),
                   jax.ShapeDtypeStruct((t, ATT_WIDTH), BF16),
                   jax.ShapeDtypeStruct((ATT_HEADS, t), F32),
                   jax.ShapeDtypeStruct((t, ATT_HEADS * LANES), BF16),
                   jax.ShapeDtypeStruct((t, ATT_HEADS * LANES), BF16),
                   jax.ShapeDtypeStruct((t, ATT_HEADS * MLA_V), BF16)],
        scratch_shapes=[pltpu.VMEM((ATT_HEADS, 1), F32)],
        compiler_params=_cparams(("arbitrary",)),
        name="proj_odd",
    )(x2, *ws, fb, cos1, sin1, *mla)


def _ssd_chunk(c, s, xbc_ref, z_ref, dt_ref, dtt_ref, cw_ref, cb_ref, dtb_ref, dtbt_ref, alog_ref, alogt_ref,
               dskip_ref, ng_ref, o_ref, buf_ref, st_ref, y_ref):
    L = SSD_CHUNK
    rows = pl.ds(pl.multiple_of(s * L, L), L)

    @pl.when(c == 0)
    def _():
        buf_ref[...] = jnp.zeros_like(buf_ref)
        st_ref[...] = jnp.zeros_like(st_ref)

    cur = xbc_ref[rows, :]
    both = jnp.concatenate([buf_ref[(c + 1) % 2], cur], axis=0)
    shifts = range(1, SSD_CONV)
    sel = jnp.concatenate([(_iota((L, 2 * L), 1) == _iota((L, 2 * L), 0) + (L - k)) for k in shifts],
                          axis=0).astype(BF16)
    shifted = _dot(sel, both)
    acc = cb_ref[...] + cur.astype(F32) * cw_ref[SSD_CONV - 1:SSD_CONV, :]
    for n, k in enumerate(shifts):
        acc = acc + shifted[n * L:(n + 1) * L] * cw_ref[SSD_CONV - 1 - k:SSD_CONV - k, :]
    buf_ref[c % 2] = cur
    xbc = _silu(acc)
    b_mat = xbc[:, SSD_INNER:SSD_INNER + SSD_BC]
    c_mat = xbc[:, SSD_INNER + SSD_BC:]
    bt = jnp.transpose(b_mat).astype(BF16)
    c_bf = c_mat.astype(BF16)

    dt = _softplus(dt_ref[rows, :] + dtb_ref[...])
    dtt = _softplus(dtt_ref[:, rows] + dtbt_ref[...])
    da = dt * (-jnp.exp(alog_ref[...]))
    dat = dtt * (-jnp.exp(alogt_ref[...]))
    row = _iota((L, L), 0)
    col = _iota((L, L), 1)
    causal = col <= row
    a_cum = _dot01_left(causal.astype(BF16), da)
    a_cumt = _dot01_right(dat, (row <= col).astype(BF16))
    a_end = a_cum[L - 1:L, :]
    low_half = _iota((1, LANES), 1) < SSD_HEAD_DIM

    def pair_cols(m, h0):
        n = m.shape[0]
        return jnp.where(low_half, jnp.broadcast_to(m[:, h0:h0 + 1], (n, LANES)),
                         jnp.broadcast_to(m[:, h0 + 1:h0 + 2], (n, LANES)))

    sumsq = jnp.zeros((L, 1), F32)
    for g in range(SSD_GROUPS):
        bt_g = bt[g * SSD_STATE:(g + 1) * SSD_STATE, :]
        c_g = c_bf[:, g * SSD_STATE:(g + 1) * SSD_STATE]
        cb = _dot(c_g, bt_g)
        for pp in range(SSD_HEADS // SSD_GROUPS // 2):
            p = g * (SSD_HEADS // SSD_GROUPS // 2) + pp
            h0 = 2 * p
            lanes = slice(p * LANES, (p + 1) * LANES)
            xs = xbc[:, lanes]
            xs_bf = xs.astype(BF16)
            yd = []
            for h in (h0, h0 + 1):
                seg = a_cum[:, h:h + 1] - a_cumt[h:h + 1, :]
                dec = jnp.where(causal, jnp.exp(jnp.minimum(seg, 0.0)), 0.0)
                m = cb * dec * dtt[h:h + 1, :]
                yd.append(_dot(m.astype(BF16), xs_bf))
            y = jnp.where(low_half, yd[0], yd[1])
            a_p = pair_cols(a_cum, h0)
            dt_p = pair_cols(dt, h0)
            end_p = pair_cols(a_end, h0)
            st = st_ref[p]
            y = y + _dot(c_g, st.astype(BF16)) * jnp.exp(a_p)
            xw = xs * (jnp.exp(end_p - a_p) * dt_p)
            st_ref[p] = st * jnp.exp(end_p) + _dot(bt_g, xw.astype(BF16))
            y = y + dskip_ref[:, lanes] * xs
            y = y * _silu(z_ref[rows, lanes].astype(F32))
            y_ref[:, lanes] = y
            sumsq = sumsq + jnp.sum(y * y, axis=1, keepdims=True)
    inv = lax.rsqrt(sumsq * (1.0 / SSD_INNER) + RMS_EPS)
    o_ref[rows, :] = (y_ref[...] * inv * ng_ref[...]).astype(o_ref.dtype)


def _ssd_kernel(*refs, chunks):
    def body(s, carry):
        _ssd_chunk(pl.program_id(1) * chunks + s, s, *refs)
        return carry
    lax.fori_loop(0, chunks, body, 0)


def _ssd(xbc, z, dt, dtt, conv_w, conv_b, dt_bias, a_log, d_skip, norm_g, batch, seq):
    t = xbc.shape[0]
    chunks = math.gcd(seq // SSD_CHUNK, SSD_STEP_CHUNKS)
    L = SSD_CHUNK
    rows = L * chunks
    nc = seq // rows
    row = lambda n: pl.BlockSpec((rows, n), lambda b, c: (b * nc + c, 0))
    full = lambda a: pl.BlockSpec(a.shape, lambda b, c: (0,) * a.ndim)
    params = [conv_w.astype(F32), conv_b.reshape(1, -1).astype(F32),
              dt_bias.reshape(1, -1).astype(F32), dt_bias.reshape(-1, 1).astype(F32),
              a_log.reshape(1, -1).astype(F32), a_log.reshape(-1, 1).astype(F32),
              jnp.repeat(d_skip.astype(F32), SSD_HEAD_DIM).reshape(1, -1),
              norm_g.reshape(1, -1).astype(F32)]
    return pl.pallas_call(
        functools.partial(_ssd_kernel, chunks=chunks),
        grid=(batch, nc),
        in_specs=[row(SSD_CONV_DIM), row(SSD_INNER), row(SSD_HEADS),
                  pl.BlockSpec((SSD_HEADS, rows), lambda b, c: (0, b * nc + c))] + [full(a) for a in params],
        out_specs=row(SSD_INNER),
        out_shape=jax.ShapeDtypeStruct((t, SSD_INNER), BF16),
        scratch_shapes=[pltpu.VMEM((2, L, SSD_CONV_DIM), BF16),
                        pltpu.VMEM((SSD_HEADS // 2, SSD_STATE, LANES), F32),
                        pltpu.VMEM((L, SSD_INNER), F32)],
        compiler_params=_cparams(("parallel", "arbitrary")),
        name="ssd",
    )(xbc, z, dt, dtt, *params)


def _attention_block(i, k_norm, refs, mode, tq, tk, diag_split):
    if mode == "fox":
        q_ref, k_ref, v_ref, fh_ref, o_ref, acc_ref, m_ref, l_ref = refs
    else:
        q_ref, k_ref, v_ref, o_ref, acc_ref, m_ref, l_ref = refs
        fh_ref = None
    n_diag = tq // tk
    lane = _iota((1, LANES), 1)
    low_half = lane < ATT_HEAD_DIM
    row = _iota((tq, tk), 0)
    col = _iota((tq, tk), 1)

    def diag_mask(dd):
        c = col + dd * tk
        if mode == "sb":
            return c < row
        if mode == "mla":
            return (c // MLA_CHUNK) <= (row // MLA_CHUNK)
        return c <= row

    if mode == "sb":
        u_mat = (_iota((tk, tk), 0) > _iota((tk, tk), 1)).astype(BF16)

    q_rows = pl.ds(pl.multiple_of(i * tq, tq), tq)
    q = q_ref[q_rows, :]
    if mode == "mla":
        qs = [q[:, 0:LANES], q[:, LANES:2 * LANES]]
    else:
        zero = jnp.zeros_like(q)
        qs = [jnp.where(low_half, q, zero), jnp.where(low_half, zero, q)]

    if mode == "fox":
        fh_q = [fh_ref[hh:hh + 1, pl.ds(pl.multiple_of(i * tq, tq), LANES)][:, 0:1] for hh in range(2)]

    def rep(x, width):
        return jnp.concatenate([x] * (width // LANES), axis=1) if width > LANES else x

    def tile(chunks, split, first):
        rows = tq // split
        groups = [(hh, slice(rb * rows, (rb + 1) * rows), rb) for hh in range(2) for rb in range(split)]
        loaded = []
        for j, dd in chunks:
            k0 = pl.multiple_of(j * tk, tk)
            k_t = k_ref[pl.ds(k0, tk), :]
            k_h = [k_t[:, 0:LANES], k_t[:, LANES:2 * LANES]] if mode == "mla" else [k_t, k_t]
            bias = None
            if mode == "fox":
                bias = [fh_q[hh] - fh_ref[hh:hh + 1, pl.ds(k0, tk)] for hh in range(2)]
            loaded.append((k_h, v_ref[pl.ds(k0, tk), :], bias, None if dd is None else diag_mask(dd)))
        parts = []
        for hh, rs, rb in groups:
            ps_ = []
            for c, (j, dd) in enumerate(chunks):
                nk = tk if dd is None else min(tk, (rb + 1) * rows - dd * tk)
                if nk > 0:
                    ps_.append((c, nk, None if dd is None else loaded[c][3][rs, 0:nk]))
            parts.append(ps_)
        flat = [(g, c, nk, m) for g, ps_ in enumerate(parts) for c, nk, m in ps_]
        zs = [_dot_nt(qs[groups[g][0]][groups[g][1]], loaded[c][0][groups[g][0]][0:nk]) for g, c, nk, m in flat]
        if mode == "sb":
            if first:
                runs = [jnp.zeros((rows, LANES), F32) for _ in groups]
                accs = [None for _ in groups]
            else:
                runs = [m_ref[hh, rs] for hh, rs, _ in groups]
                accs = [acc_ref[hh, rs] for hh, rs, _ in groups]
            sps = [_softplus(z) for z in zs]
            l1ms = [-sp if m is None else jnp.where(m, -sp, 0.0) for sp, (_, _, _, m) in zip(sps, flat)]
            css = [_dot(l1m.astype(BF16), u_mat[0:nk, 0:nk]) for l1m, (_, _, nk, _) in zip(l1ms, flat)]
            tots = [cs[:, 0:1] + l1m[:, 0:1] for cs, l1m in zip(css, l1ms)]
            seen = []
            cur = list(runs)
            for n, (g, c, nk, m) in enumerate(flat):
                seen.append(cur[g])
                cur[g] = cur[g] + tots[n]
            ws = [jnp.exp((z - sp) + cs + rep(run, nk))
                  for z, sp, cs, run, (_, _, nk, _) in zip(zs, sps, css, seen, flat)]
            ws = [w if m is None else jnp.where(m, w, 0.0) for w, (_, _, _, m) in zip(ws, flat)]
            pvs = [_dot(w.astype(BF16), loaded[c][1][0:nk]) for w, (_, c, nk, _) in zip(ws, flat)]
            for n, (g, c, nk, m) in enumerate(flat):
                accs[g] = pvs[n] if accs[g] is None else accs[g] + pvs[n]
            for (hh, rs, _), acc, run in zip(groups, accs, cur):
                acc_ref[hh, rs] = acc
                m_ref[hh, rs] = run
        else:
            if mode == "fox":
                zs = [z + loaded[c][2][groups[g][0]][:, 0:nk] for z, (g, c, nk, m) in zip(zs, flat)]
            zs = [z if m is None else jnp.where(m, z, NEG_BIG) for z, (_, _, _, m) in zip(zs, flat)]
            if first:
                m_news = [jnp.full((rows, LANES), NEG_BIG, F32) for _ in groups]
            else:
                m_olds = [m_ref[hh, rs] for hh, rs, _ in groups]
                m_news = list(m_olds)
            for z, (g, _, _, _) in zip(zs, flat):
                m_news[g] = jnp.maximum(m_news[g], jnp.max(z, axis=1, keepdims=True))
            ps = [jnp.exp(z - rep(m_news[g], nk)) for z, (g, _, nk, _) in zip(zs, flat)]
            pvs = [_dot(p.astype(BF16), loaded[c][1][0:nk]) for p, (_, c, nk, _) in zip(ps, flat)]
            if first:
                l_news = [None for _ in groups]
                acc_news = [None for _ in groups]
            else:
                alphas = [jnp.exp(m_old - m_new) for m_old, m_new in zip(m_olds, m_news)]
                l_news = [alpha * l_ref[hh, rs] for alpha, (hh, rs, _) in zip(alphas, groups)]
                acc_news = [alpha * acc_ref[hh, rs] for alpha, (hh, rs, _) in zip(alphas, groups)]
            for p, pv, (g, _, _, _) in zip(ps, pvs, flat):
                ls = jnp.sum(p, axis=1, keepdims=True)
                l_news[g] = jnp.broadcast_to(ls, (rows, LANES)) if l_news[g] is None else l_news[g] + ls
                acc_news[g] = pv if acc_news[g] is None else acc_news[g] + pv
            for (hh, rs, _), m_new, l_new, acc in zip(groups, m_news, l_news, acc_news):
                l_ref[hh, rs] = l_new
                acc_ref[hh, rs] = acc
                m_ref[hh, rs] = m_new

    def more(j_done):
        if mode == "sb":
            return (jnp.max(m_ref[...]) > -EXP_ZERO).astype(jnp.int32)
        if mode == "fox":
            kl = pl.multiple_of(jnp.maximum(j_done - 1, 0) * tk, tk)
            best = None
            for hh in range(2):
                fh_last = fh_ref[hh:hh + 1, pl.ds(kl, tk)][:, tk - 1:tk]
                v = jnp.max(qk_bound - m_ref[hh], axis=(0, 1), keepdims=True) + (fh_q[hh] - fh_last)
                best = v if best is None else jnp.maximum(best, v)
            return (jnp.max(best) > -EXP_ZERO).astype(jnp.int32)
        return jnp.int32(1)

    if mode == "fox":
        qf = q.astype(F32)
        qk_bound = jnp.sqrt(jnp.sum(qf * qf, axis=1, keepdims=True)) * k_norm

    n_off = i * n_diag
    diag_chunks = [(n_off + dd, dd) for dd in reversed(range(n_diag))]

    @pl.when(i == 0)
    def _():
        tile(diag_chunks, diag_split, True)

    @pl.when(i > 0)
    def _():
        tile(diag_chunks + [(n_off - 1, None)], diag_split, True)

    def cond(c):
        return jnp.logical_and(c[0] < n_off, c[1] > 0)

    def body(c):
        j = n_off - 1 - c[0]
        tile([(j, None)], 1, False)
        return c[0] + 1, more(j)
    lax.while_loop(cond, body, (jnp.int32(1), more(jnp.maximum(n_off - 1, 0))))
    if mode == "sb":
        out = [acc_ref[0], acc_ref[1]]
    else:
        out = [acc_ref[hh] / l_ref[hh] for hh in range(2)]
    o_ref[q_rows, :] = jnp.where(low_half, out[0], out[1]).astype(o_ref.dtype)


def _attention_kernel(*refs, mode, tq, tk, diag_split, nq):
    k_norm = None
    if mode == "fox":
        kf = refs[1][...].astype(F32)
        n2 = jnp.max(jnp.sum(kf * kf, axis=1, keepdims=True), axis=0, keepdims=True)
        k_norm = jnp.broadcast_to(jnp.sqrt(n2), (1, LANES))

    def q_block(i, carry):
        _attention_block(i, k_norm, refs, mode, tq, tk, diag_split)
        return carry
    lax.fori_loop(0, nq, q_block, 0)


def _attention(mode, q, k, v, batch, seq, fh=None):
    t = v.shape[0]
    tq = min(ATT_TQ[mode], seq)
    tk = min(ATT_TK[mode], tq)
    nq = seq // tq
    qk_w = 2 * LANES if mode == "mla" else LANES
    in_specs = [pl.BlockSpec((seq, qk_w), lambda b, h: (b, h)),
                pl.BlockSpec((seq, qk_w), lambda b, h: (b, h)),
                pl.BlockSpec((seq, LANES), lambda b, h: (b, h))]
    args = [q, k, v]
    scratch = [pltpu.VMEM((2, tq, LANES), F32), pltpu.VMEM((2, tq, LANES), F32), pltpu.VMEM((2, tq, LANES), F32)]
    if mode == "fox":
        in_specs.append(pl.BlockSpec((None, 2, seq), lambda b, h: (h, 0, b)))
        args.append(fh.reshape(HEAD_PAIRS, 2, t))
    return pl.pallas_call(
        functools.partial(_attention_kernel, mode=mode, tq=tq, tk=tk, diag_split=max(1, tq // ATT_DIAG_ROWS), nq=nq),
        grid=(batch, HEAD_PAIRS),
        in_specs=in_specs,
        out_specs=pl.BlockSpec((seq, LANES), lambda b, h: (b, h)),
        out_shape=jax.ShapeDtypeStruct((t, ATT_WIDTH), BF16),
        scratch_shapes=scratch,
        compiler_params=_cparams(("parallel", "parallel")),
        name="attn_" + mode,
    )(*args)


def _layer_norm_rows(r, g, b):
    mu = jnp.mean(r, axis=1, keepdims=True)
    d = r - mu
    var = jnp.mean(d * d, axis=1, keepdims=True)
    return d * lax.rsqrt(var + LN_EPS) * g + b


def _router(x, w_both, bias):
    tm = x.shape[0]
    x_hi = x.astype(BF16)
    x_lo = (x - x_hi.astype(F32)).astype(BF16)
    both = _dot(x_hi, w_both)
    logits = both[:, 0:LANES] + (both[:, LANES:2 * LANES] + _dot(x_lo, w_both[:, 0:LANES])) + bias
    e_idx = _iota((tm, LANES), 1)
    e_pos = e_idx.astype(F32)

    def first_lane(hit):
        return jnp.min(jnp.where(hit, e_pos, float(LANES)), axis=1, keepdims=True).astype(jnp.int32)

    is_group = jnp.logical_and(e_idx >= MOE_EXPERTS, e_idx < MOE_EXPERTS + MOE_GROUPS)
    gl = jnp.where(is_group, logits, -jnp.inf)
    g_max = jnp.max(gl, axis=1, keepdims=True)
    g_p = 1.0 / jnp.sum(jnp.exp(gl - g_max), axis=1, keepdims=True)
    g_sel = first_lane(gl == g_max) - MOE_EXPERTS
    in_group = (e_idx // MOE_EPG) == g_sel
    masked = jnp.where(in_group, logits, -jnp.inf)
    m1 = jnp.max(masked, axis=1, keepdims=True)
    i1 = first_lane(masked == m1)
    masked2 = jnp.where(e_idx == i1, -jnp.inf, masked)
    m2 = jnp.max(masked2, axis=1, keepdims=True)
    i2 = first_lane(masked2 == m2)
    e2 = jnp.exp(m2 - m1)
    w1 = g_p / (1.0 + e2)
    w2 = w1 * e2
    gate = jnp.where(e_idx == i1, w1, 0.0) + jnp.where(e_idx == i2, w2, 0.0)
    lo = jnp.minimum(i1, i2) - g_sel * MOE_EPG
    hi = jnp.maximum(i1, i2) - g_sel * MOE_EPG
    pair = ((lo * (2 * MOE_EPG - 1 - lo)) >> 1) + (hi - lo - 1)
    return gate, g_sel * MOE_PAIRS + pair


def _outproj_ln_router_kernel(a_ref, b_ref, x_ref, wa_ref, wb_ref, g_ref, beta_ref, wr_ref, br_ref,
                              tri_ref, xg_ref, bucket_ref, rank_ref, count_ref, run_ref):
    i = pl.program_id(0)
    tm, d = x_ref.shape

    @pl.when(i == 0)
    def _():
        run_ref[...] = jnp.zeros_like(run_ref)

    y = _dot(a_ref[...], wa_ref[...]) + _dot(b_ref[...], wb_ref[...])
    x1 = _layer_norm_rows(DEEPNORM_ALPHA * x_ref[...] + y, g_ref[...], beta_ref[...])
    gate, bucket = _router(x1, wr_ref[...], br_ref[...])
    xg_ref[:, 0:d] = x1
    xg_ref[:, d:d + LANES] = gate
    bucket_ref[...] = bucket
    onehot = _iota((tm, LANES), 1) == bucket
    prefix = _dot(tri_ref[...], jnp.where(onehot, 1.0, 0.0).astype(BF16))
    before = run_ref[...]
    rank = jnp.sum(jnp.where(onehot, prefix + before, 0.0), axis=1, keepdims=True) - 1.0
    rank_ref[...] = rank.astype(jnp.int32)
    run_ref[...] = before + prefix[tm - 1:tm, :]
    count_ref[...] = run_ref[...]


def _outproj_ln_router(a, b, x2, w_out, ln_g, ln_b, w_group, b_group, w_expert, b_expert, tm):
    t, d = x2.shape
    ka, kb = a.shape[1], b.shape[1]
    pad = LANES - MOE_EXPERTS - MOE_GROUPS
    w_r = jnp.pad(jnp.concatenate([w_expert, w_group], axis=1).astype(F32), ((0, 0), (0, pad)))
    b_r = jnp.pad(jnp.concatenate([b_expert, b_group]).astype(F32), (0, pad)).reshape(1, LANES)
    w_r_hi = w_r.astype(BF16)
    consts = [w_out[:ka].astype(BF16), w_out[ka:].astype(BF16),
              ln_g.reshape(1, -1).astype(F32), ln_b.reshape(1, -1).astype(F32),
              jnp.concatenate([w_r_hi, (w_r - w_r_hi.astype(F32)).astype(BF16)], axis=1), b_r,
              jnp.tril(jnp.ones((tm, tm), BF16))]
    row = lambda n: pl.BlockSpec((tm, n), lambda i: (i, 0))
    full = lambda c: pl.BlockSpec(c.shape, lambda i: (0, 0))
    return pl.pallas_call(
        _outproj_ln_router_kernel,
        grid=(t // tm,),
        in_specs=[row(ka), row(kb), row(d)] + [full(c) for c in consts],
        out_specs=[row(d + LANES), row(1), row(1), pl.BlockSpec((1, LANES), lambda i: (0, 0))],
        out_shape=[jax.ShapeDtypeStruct((t, d + LANES), F32),
                   jax.ShapeDtypeStruct((t, 1), jnp.int32),
                   jax.ShapeDtypeStruct((t, 1), jnp.int32),
                   jax.ShapeDtypeStruct((1, LANES), F32)],
        scratch_shapes=[pltpu.VMEM((1, LANES), F32)],
        compiler_params=_cparams(("arbitrary",)),
        name="outproj_ln_router",
    )(a, b, x2, *consts)


def _moe_plan(bucket, rank, counts, t):
    counts = counts[0, :MOE_BUCKETS].astype(jnp.int32)
    padded = ((counts + MOE_TILE - 1) // MOE_TILE) * MOE_TILE
    ends = jnp.cumsum(padded)
    starts = ends - padded
    pos = jnp.take(starts, bucket[:, 0]) + rank[:, 0]
    n_tiles = t // MOE_TILE + MOE_BUCKETS
    tile_start = jnp.arange(n_tiles, dtype=jnp.int32) * MOE_TILE
    tile_bucket = jnp.minimum(jnp.sum((tile_start[:, None] >= ends[None, :]).astype(jnp.int32), axis=1),
                              MOE_BUCKETS - 1)
    pairs = [(a, b) for a in range(MOE_EPG) for b in range(a + 1, MOE_EPG)]
    first = jnp.asarray([g * MOE_EPG + a for g in range(MOE_GROUPS) for a, _ in pairs], jnp.int32)
    second = jnp.asarray([g * MOE_EPG + b for g in range(MOE_GROUPS) for _, b in pairs], jnp.int32)
    n_used = (ends[MOE_BUCKETS - 1] // MOE_TILE).reshape(1)
    return pos, ends, jnp.take(first, tile_bucket), jnp.take(second, tile_bucket), n_used, n_tiles


def _moe_scatter_kernel(ends_ref, pos_ref, xg_ref, xs_hbm, zero_ref, sem, zero_sem):
    n = pos_ref.shape[0]

    @pl.when(pl.program_id(0) == 0)
    def _():
        zero_ref[...] = jnp.zeros_like(zero_ref)
        total = ends_ref[MOE_BUCKETS - 1]

        def zero_copy(row0):
            return pltpu.make_async_copy(zero_ref, xs_hbm.at[pl.ds(pl.multiple_of(row0, MOE_TILE), MOE_TILE)],
                                         zero_sem)

        jobs = [(ends_ref[b] >= MOE_TILE, ends_ref[b] - MOE_TILE) for b in range(MOE_BUCKETS)]
        jobs += [(total + u * MOE_TILE < xs_hbm.shape[0], total + u * MOE_TILE) for u in range(MOE_BUCKETS)]
        for wanted, row0 in jobs:
            @pl.when(wanted)
            def _():
                zero_copy(row0).start()
        for wanted, row0 in jobs:
            @pl.when(wanted)
            def _():
                zero_copy(row0).wait()

    def body(r, c):
        pltpu.make_async_copy(xg_ref.at[pl.ds(r, 1)], xs_hbm.at[pl.ds(pos_ref[r], 1)], sem).start()
        return c
    lax.fori_loop(0, n, body, 0, unroll=DMA_UNROLL)
    pltpu.make_async_copy(xg_ref, xs_hbm.at[pl.ds(0, n)], sem).wait()


def _moe_scatter(xg, pos, ends, n_rows, chunk):
    t, w = xg.shape
    return pl.pallas_call(
        _moe_scatter_kernel,
        grid=(t // chunk,),
        in_specs=[pl.BlockSpec(memory_space=pltpu.SMEM),
                  pl.BlockSpec((chunk,), lambda i: (i,), memory_space=pltpu.SMEM),
                  pl.BlockSpec((chunk, w), lambda i: (i, 0))],
        out_specs=pl.BlockSpec(memory_space=pl.ANY),
        out_shape=jax.ShapeDtypeStruct((n_rows, w), xg.dtype),
        scratch_shapes=[pltpu.VMEM((MOE_TILE, w), xg.dtype), pltpu.SemaphoreType.DMA(()),
                        pltpu.SemaphoreType.DMA(())],
        compiler_params=_cparams(("arbitrary",)),
        name="moe_scatter",
    )(ends, pos, xg)


def _moe_ffn_kernel(ea_ref, eb_ref, nu_ref, xs_ref, wga_ref, wua_ref, wda_ref, wgb_ref, wub_ref, wdb_ref, y_ref):
    i = pl.program_id(0)
    d = wga_ref.shape[0]

    @pl.when(i < nu_ref[0])
    def _():
        x = xs_ref[:, 0:d].astype(BF16)
        gate = xs_ref[:, d:d + LANES]
        lane = _iota(gate.shape, 1)
        y = None
        for e_ref, wg_ref, wu_ref, wd_ref in ((ea_ref, wga_ref, wua_ref, wda_ref), (eb_ref, wgb_ref, wub_ref, wdb_ref)):
            w = jnp.sum(jnp.where(lane == e_ref[i], gate, 0.0), axis=1, keepdims=True)
            hid = _silu(_dot(x, wg_ref[...])) * _dot(x, wu_ref[...]) * w
            part = _dot(hid.astype(BF16), wd_ref[...])
            y = part if y is None else y + part
        y_ref[...] = y

    @pl.when(i >= nu_ref[0])
    def _():
        y_ref[...] = jnp.zeros_like(y_ref)


def _moe_ffn(xs, ea, eb, n_used, n_tiles, w_gate, w_up, w_down):
    d = w_gate.shape[1]
    wg, wu, wd = w_gate.astype(BF16), w_up.astype(BF16), w_down.astype(BF16)
    up_a = pl.BlockSpec((None, d, MOE_FF), lambda i, ea, eb, nu: (ea[i], 0, 0))
    dn_a = pl.BlockSpec((None, MOE_FF, d), lambda i, ea, eb, nu: (ea[i], 0, 0))
    up_b = pl.BlockSpec((None, d, MOE_FF), lambda i, ea, eb, nu: (eb[i], 0, 0))
    dn_b = pl.BlockSpec((None, MOE_FF, d), lambda i, ea, eb, nu: (eb[i], 0, 0))
    return pl.pallas_call(
        _moe_ffn_kernel,
        grid_spec=pltpu.PrefetchScalarGridSpec(
            num_scalar_prefetch=3,
            grid=(n_tiles,),
            in_specs=[pl.BlockSpec((MOE_TILE, d + LANES), lambda i, ea, eb, nu: (jnp.minimum(i, nu[0] - 1), 0)),
                      up_a, up_a, dn_a, up_b, up_b, dn_b],
            out_specs=pl.BlockSpec((MOE_TILE, d), lambda i, ea, eb, nu: (i, 0))),
        out_shape=jax.ShapeDtypeStruct((n_tiles * MOE_TILE, d), F32),
        compiler_params=_cparams(("arbitrary",)),
        name="moe_ffn",
    )(ea, eb, n_used, xs, wg, wu, wd, wg, wu, wd)


def _moe_combine_kernel(pos_ref, pos_next_ref, y_hbm, x_ref, g_ref, beta_ref, o_ref, buf_ref, sem):
    i = pl.program_id(0)
    n = pos_ref.shape[0]
    slot = i % 2

    def gather(p_ref, s):
        def body(r, c):
            pltpu.make_async_copy(y_hbm.at[pl.ds(p_ref[r], 1)], buf_ref.at[s, pl.ds(r, 1)], sem.at[s]).start()
            return c
        lax.fori_loop(0, n, body, 0, unroll=DMA_UNROLL)

    @pl.when(i == 0)
    def _():
        gather(pos_ref, slot)

    @pl.when(i + 1 < pl.num_programs(0))
    def _():
        gather(pos_next_ref, 1 - slot)

    pltpu.make_async_copy(y_hbm.at[pl.ds(0, n)], buf_ref.at[slot], sem.at[slot]).wait()
    r = DEEPNORM_ALPHA * x_ref[...] + buf_ref[slot]
    o_ref[...] = _layer_norm_rows(r, g_ref[...], beta_ref[...])


def _moe_combine(y_sorted, pos, xg, ln_g, ln_b, tm):
    t = pos.shape[0]
    d = y_sorted.shape[1]
    ln = [ln_g.reshape(1, -1).astype(F32), ln_b.reshape(1, -1).astype(F32)]
    last = t // tm - 1
    return pl.pallas_call(
        _moe_combine_kernel,
        grid=(t // tm,),
        in_specs=[pl.BlockSpec((tm,), lambda i: (i,), memory_space=pltpu.SMEM),
                  pl.BlockSpec((tm,), lambda i: (jnp.minimum(i + 1, last),), memory_space=pltpu.SMEM),
                  pl.BlockSpec(memory_space=pl.ANY),
                  pl.BlockSpec((tm, d), lambda i: (i, 0))]
        + [pl.BlockSpec(c.shape, lambda i: (0, 0)) for c in ln],
        out_specs=pl.BlockSpec((tm, d), lambda i: (i, 0)),
        out_shape=jax.ShapeDtypeStruct((t, d), F32),
        scratch_shapes=[pltpu.VMEM((2, tm, d), F32), pltpu.SemaphoreType.DMA((2,))],
        compiler_params=_cparams(("arbitrary",)),
        name="moe_combine",
    )(pos, pos, y_sorted, xg, *ln)


def _mixer_out_and_moe(a, b, x2, w_out, ln1_g, ln1_b, w_group, b_group, w_expert, b_expert,
                       w_gate, w_up, w_down, ln2_g, ln2_b, tm):
    t = x2.shape[0]
    xg, bucket, rank, counts = _outproj_ln_router(a, b, x2, w_out, ln1_g, ln1_b, w_group, b_group,
                                                  w_expert, b_expert, tm)
    pos, ends, ea, eb, n_used, n_tiles = _moe_plan(bucket, rank, counts, t)
    xs = _moe_scatter(xg, pos, ends, n_tiles * MOE_TILE, _row_tile(t, 1024))
    y_sorted = _moe_ffn(xs, ea, eb, n_used, n_tiles, w_gate, w_up, w_down)
    return _moe_combine(y_sorted, pos, xg, ln2_g, ln2_b, tm)


def _row_tile(n, pref):
    tm = min(pref, n)
    assert n % tm == 0
    return tm


def kernel(x, ev_w_in, ev_conv_w, ev_conv_b, ev_dt_bias, ev_a_log, ev_d_skip, ev_norm_g, ev_w_out, od_w_in, od_q_norm_g, od_w_q_up, od_kv_norm_g, od_w_kv_up, od_f_bias, od_w_out, ln1_g, ln1_b, ln2_g, ln2_b, moe_w_group, moe_b_group, moe_w_expert, moe_b_expert, moe_w_gate, moe_w_up, moe_w_down):
    batch, seq, d = x.shape
    t = batch * seq
    assert seq % SSD_CHUNK == 0 and seq % LANES == 0 and t % MOE_TILE == 0
    tm_proj = _row_tile(seq, 512)
    x2 = x.reshape(t, d)

    z, xbc, dt, dtt, q, k, v = _proj_even(x2, ev_w_in[0], tm_proj)
    y_ssd = _ssd(xbc, z, dt, dtt, ev_conv_w[0], ev_conv_b[0], ev_dt_bias[0], ev_a_log[0], ev_d_skip[0],
                 ev_norm_g[0], batch, seq)
    y_sb = _attention("sb", q, k, v, batch, seq)
    x2 = _mixer_out_and_moe(y_ssd, y_sb, x2, ev_w_out[0], ln1_g[0], ln1_b[0], moe_w_group[0], moe_b_group[0],
                            moe_w_expert[0], moe_b_expert[0], moe_w_gate[0], moe_w_up[0], moe_w_down[0],
                            ln2_g[0], ln2_b[0], tm_proj)

    q, k, v, fh, qm, km, vm = _proj_odd(x2, od_w_in[0], od_f_bias[0], od_q_norm_g[0], od_w_q_up[0],
                                        od_kv_norm_g[0], od_w_kv_up[0], seq, tm_proj)
    y_mla = _attention("mla", qm, km, vm, batch, seq)
    y_fox = _attention("fox", q, k, v, batch, seq, fh=fh)
    x2 = _mixer_out_and_moe(y_mla, y_fox, x2, od_w_out[0], ln1_g[1], ln1_b[1], moe_w_group[1], moe_b_group[1],
                            moe_w_expert[1], moe_b_expert[1], moe_w_gate[1], moe_w_up[1], moe_w_down[1],
                            ln2_g[1], ln2_b[1], tm_proj)
    return x2.reshape(batch, seq, d)
```

```python
import functools
import math

import jax
import jax.numpy as jnp
from jax import lax
from jax.experimental import pallas as pl
from jax.experimental.pallas import tpu as pltpu

F32 = jnp.float32
BF16 = jnp.bfloat16

SSD_HEADS = 16
SSD_HEAD_DIM = 64
SSD_INNER = SSD_HEADS * SSD_HEAD_DIM
SSD_GROUPS = 2
SSD_STATE = 128
SSD_CONV = 4
SSD_BC = SSD_GROUPS * SSD_STATE
SSD_CONV_DIM = SSD_INNER + 2 * SSD_BC
ATT_HEADS = 8
ATT_HEAD_DIM = 64
ATT_WIDTH = ATT_HEADS * ATT_HEAD_DIM
HEAD_PAIRS = ATT_HEADS // 2
MLA_Q_RANK = 256
MLA_KV_RANK = 128
MLA_NOPE = 64
MLA_ROPE = 32
MLA_V = 64
MLA_CHUNK = 64
ROPE_THETA = 10000.0
MOE_GROUPS = 4
MOE_EPG = 4
MOE_EXPERTS = MOE_GROUPS * MOE_EPG
MOE_FF = 256
DEPTH = 2
DEEPNORM_ALPHA = (2.0 * DEPTH) ** 0.25
LN_EPS = 1e-5
RMS_EPS = 1e-6

LANES = 128
SSD_CHUNK = 128
SSD_STEP_CHUNKS = 4
ATT_TQ = {"sb": 256, "mla": 512, "fox": 256}
ATT_TK = {"sb": 256, "mla": 512, "fox": 256}
ATT_DIAG_ROWS = 128
NEG_BIG = -1e30
EXP_ZERO = 110.0
MOE_TILE = 512
DMA_UNROLL = 8
MOE_PAIRS = MOE_EPG * (MOE_EPG - 1) // 2
MOE_BUCKETS = MOE_GROUPS * MOE_PAIRS
VMEM_LIMIT = 56 * 1024 * 1024


def _cparams(sem):
    return pltpu.CompilerParams(dimension_semantics=sem, vmem_limit_bytes=VMEM_LIMIT)


def _dot(a, b):
    return jnp.dot(a, b, preferred_element_type=F32)


def _dot_nt(a, b):
    return lax.dot_general(a, b, (((1,), (1,)), ((), ())), preferred_element_type=F32)


def _split3(x):
    hi = x.astype(BF16)
    r1 = x - hi.astype(F32)
    mid = r1.astype(BF16)
    lo = (r1 - mid.astype(F32)).astype(BF16)
    return hi, mid, lo


def _dot01_right(x, m01):
    hi, mid, lo = _split3(x)
    return _dot(hi, m01) + _dot(mid, m01) + _dot(lo, m01)


def _dot01_left(m01, x):
    hi, mid, lo = _split3(x)
    return _dot(m01, hi) + _dot(m01, mid) + _dot(m01, lo)


def _softplus(x):
    return jnp.maximum(x, 0.0) + jnp.log(1.0 + jnp.exp(-jnp.abs(x)))


def _silu(x):
    return x * (1.0 / (1.0 + jnp.exp(-x)))


def _iota(shape, dim):
    return lax.broadcasted_iota(jnp.int32, shape, dim)


def _proj_even_kernel(x_ref, wz_ref, wx_ref, wdt_ref, wdtt_ref, wq_ref, wk_ref, wv_ref,
                      z_ref, xbc_ref, dt_ref, dtt_ref, q_ref, k_ref, v_ref):
    x = x_ref[...].astype(BF16)
    z_ref[...] = _dot(x, wz_ref[...]).astype(z_ref.dtype)
    xbc_ref[...] = _dot(x, wx_ref[...]).astype(xbc_ref.dtype)
    dt_ref[...] = _dot(x, wdt_ref[...])
    dtt_ref[...] = _dot_nt(wdtt_ref[...], x)
    q_ref[...] = _dot(x, wq_ref[...]).astype(q_ref.dtype)
    k_ref[...] = _dot(x, wk_ref[...]).astype(k_ref.dtype)
    v_ref[...] = _dot(x, wv_ref[...]).astype(v_ref.dtype)


def _proj_even(x2, w_in, tm):
    t, d = x2.shape
    o = 0
    parts = []
    for n in (SSD_INNER, SSD_CONV_DIM, SSD_HEADS, ATT_WIDTH, ATT_WIDTH, ATT_WIDTH):
        parts.append(w_in[:, o:o + n])
        o += n
    wz, wx, wdt, wq, wk, wv = parts
    wq = wq * (1.0 / math.sqrt(ATT_HEAD_DIM))
    ws = [wz.astype(BF16), wx.astype(BF16), wdt.astype(BF16), wdt.T.astype(BF16),
          wq.astype(BF16), wk.astype(BF16), wv.astype(BF16)]
    row = lambda n: pl.BlockSpec((tm, n), lambda i: (i, 0))
    full = lambda a: pl.BlockSpec(a.shape, lambda i: (0, 0))
    return pl.pallas_call(
        _proj_even_kernel,
        grid=(t // tm,),
        in_specs=[row(d)] + [full(w) for w in ws],
        out_specs=[row(SSD_INNER), row(SSD_CONV_DIM), row(SSD_HEADS),
                   pl.BlockSpec((SSD_HEADS, tm), lambda i: (0, i)),
                   row(ATT_WIDTH), row(ATT_WIDTH), row(ATT_WIDTH)],
        out_shape=[jax.ShapeDtypeStruct((t, SSD_INNER), BF16),
                   jax.ShapeDtypeStruct((t, SSD_CONV_DIM), BF16),
                   jax.ShapeDtypeStruct((t, SSD_HEADS), F32),
                   jax.ShapeDtypeStruct((SSD_HEADS, t), F32),
                   jax.ShapeDtypeStruct((t, ATT_WIDTH), BF16),
                   jax.ShapeDtypeStruct((t, ATT_WIDTH), BF16),
                   jax.ShapeDtypeStruct((t, ATT_WIDTH), BF16)],
        compiler_params=_cparams(("parallel",)),
        name="proj_even",
    )(x2, *ws)


def _mla_rows(cq, ckv, kpe, kps, cos1, sin1, qg, kvg, wqm, wqs, wk, wv):
    def rms(x, g):
        return x * lax.rsqrt(jnp.mean(x * x, axis=1, keepdims=True) + RMS_EPS) * g

    cqn = rms(cq, qg).astype(BF16)
    ckvn = rms(ckv, kvg).astype(BF16)
    cos8 = jnp.concatenate([cos1] * ATT_HEADS, axis=1)
    sin8 = jnp.concatenate([sin1] * ATT_HEADS, axis=1)
    scale = 1.0 / math.sqrt(MLA_NOPE + MLA_ROPE)
    q = (_dot(cqn, wqm) * cos8 + _dot(cqn, wqs) * sin8) * scale
    k_rope = kpe * cos1 + kps * sin1
    k = _dot(ckvn, wk) + jnp.concatenate([k_rope] * ATT_HEADS, axis=1)
    return q, k, _dot(ckvn, wv)


def _proj_odd_kernel(x_ref, wlat_ref, wq_ref, wk_ref, wv_ref, wft_ref, fb_ref,
                     cos_ref, sin_ref, qg_ref, kvg_ref, wqm_ref, wqs_ref, wkup_ref, wvup_ref, upper_ref,
                     q_ref, k_ref, v_ref, fh_ref, qm_ref, km_ref, vm_ref, carry_ref, *, tiles_per_seq):
    i = pl.program_id(0)
    tm = x_ref.shape[0]
    x = x_ref[...].astype(BF16)
    q_ref[...] = _dot(x, wq_ref[...]).astype(q_ref.dtype)
    k_ref[...] = _dot(x, wk_ref[...]).astype(k_ref.dtype)
    v_ref[...] = _dot(x, wv_ref[...]).astype(v_ref.dtype)
    lat = _dot(x, wlat_ref[...])
    kv0 = MLA_Q_RANK + MLA_KV_RANK
    qm, km, vm = _mla_rows(lat[:, 0:MLA_Q_RANK], lat[:, MLA_Q_RANK:kv0], lat[:, kv0:kv0 + LANES],
                           lat[:, kv0 + LANES:kv0 + 2 * LANES], cos_ref[...], sin_ref[...], qg_ref[...], kvg_ref[...],
                           wqm_ref[...], wqs_ref[...], wkup_ref[...], wvup_ref[...])
    qm_ref[...] = qm.astype(qm_ref.dtype)
    km_ref[...] = km.astype(km_ref.dtype)
    vm_ref[...] = vm.astype(vm_ref.dtype)

    @pl.when(i % tiles_per_seq == 0)
    def _():
        carry_ref[...] = jnp.zeros_like(carry_ref)

    f_raw = _dot_nt(wft_ref[...], x) + fb_ref[...]
    log_f = -_softplus(-f_raw)
    cum = _dot01_right(log_f, upper_ref[...]) + carry_ref[...]
    fh_ref[...] = cum
    carry_ref[...] = cum[:, tm - 1:tm]


def _proj_odd(x2, w_in, f_bias, q_norm_g, w_q_up, kv_norm_g, w_kv_up, seq, tm):
    t, d = x2.shape
    sizes = (MLA_Q_RANK, MLA_KV_RANK, MLA_ROPE, ATT_WIDTH, ATT_WIDTH, ATT_WIDTH, ATT_HEADS)
    o = 0
    parts = []
    for n in sizes:
        parts.append(w_in[:, o:o + n])
        o += n
    wcq, wckv, wkpe, wq, wk, wv, wf = parts
    half = MLA_ROPE // 2
    pad = LANES - MLA_NOPE - MLA_ROPE
    pad_l = jnp.zeros((d, MLA_NOPE), F32)
    pad_r = jnp.zeros((d, pad), F32)
    wkpe_p = jnp.concatenate([pad_l, wkpe, pad_r], axis=1)
    wkps_p = jnp.concatenate([pad_l, -wkpe[:, half:], wkpe[:, :half], pad_r], axis=1)
    wq = wq * (1.0 / math.sqrt(ATT_HEAD_DIM))
    ws = [jnp.concatenate([wcq, wckv, wkpe_p, wkps_p], axis=1).astype(BF16),
          wq.astype(BF16), wk.astype(BF16), wv.astype(BF16), wf.T.astype(BF16)]
    fb = f_bias.reshape(ATT_HEADS, 1).astype(F32)
    inv_freq = ROPE_THETA ** (-(jnp.arange(0, MLA_ROPE, 2, dtype=F32) / MLA_ROPE))
    ang = jnp.arange(seq, dtype=F32)[:, None] * inv_freq[None, :]
    cos, sin = jnp.cos(ang), jnp.sin(ang)
    cos1 = jnp.concatenate([jnp.ones((seq, MLA_NOPE), F32), cos, cos, jnp.zeros((seq, pad), F32)], axis=1)
    sin1 = jnp.concatenate([jnp.zeros((seq, MLA_NOPE), F32), sin, sin, jnp.zeros((seq, pad), F32)], axis=1)
    wqu = w_q_up.reshape(MLA_Q_RANK, ATT_HEADS, MLA_NOPE + MLA_ROPE)
    zq = lambda n: jnp.zeros((MLA_Q_RANK, ATT_HEADS, n), F32)
    wq_main = jnp.concatenate([wqu, zq(pad)], axis=2).reshape(MLA_Q_RANK, ATT_HEADS * LANES)
    wq_swap = jnp.concatenate([zq(MLA_NOPE), -wqu[:, :, MLA_NOPE + half:], wqu[:, :, MLA_NOPE:MLA_NOPE + half],
                               zq(pad)], axis=2).reshape(MLA_Q_RANK, ATT_HEADS * LANES)
    wkv = w_kv_up.reshape(MLA_KV_RANK, ATT_HEADS, MLA_NOPE + MLA_V)
    wk_up = jnp.concatenate([wkv[:, :, :MLA_NOPE], jnp.zeros((MLA_KV_RANK, ATT_HEADS, LANES - MLA_NOPE), F32)],
                            axis=2).reshape(MLA_KV_RANK, ATT_HEADS * LANES)
    wv_up = wkv[:, :, MLA_NOPE:].reshape(MLA_KV_RANK, ATT_HEADS * MLA_V)
    mla = [q_norm_g.reshape(1, -1).astype(F32), kv_norm_g.reshape(1, -1).astype(F32),
           wq_main.astype(BF16), wq_swap.astype(BF16), wk_up.astype(BF16), wv_up.astype(BF16),
           jnp.triu(jnp.ones((tm, tm), BF16))]
    ns = seq // tm
    row = lambda n: pl.BlockSpec((tm, n), lambda i: (i, 0))
    tab = pl.BlockSpec((tm, LANES), lambda i: (i % ns, 0))
    full = lambda a: pl.BlockSpec(a.shape, lambda i: (0, 0))
    return pl.pallas_call(
        functools.partial(_proj_odd_kernel, tiles_per_seq=ns),
        grid=(t // tm,),
        in_specs=[row(d)] + [full(w) for w in ws] + [full(fb), tab, tab] + [full(a) for a in mla],
        out_specs=[row(ATT_WIDTH), row(ATT_WIDTH), row(ATT_WIDTH),
                   pl.BlockSpec((ATT_HEADS, tm), lambda i: (0, i)),
                   row(ATT_HEADS * LANES), row(ATT_HEADS * LANES), row(ATT_HEADS * MLA_V)],
        out_shape=[jax.ShapeDtypeStruct((t, ATT_WIDTH), BF16),
                   jax.ShapeDtypeStruct((t, ATT_WIDTH), BF16),
                   jax.ShapeDtypeStruct((t, ATT_WIDTH), BF16),
                   jax.ShapeDtypeStruct((ATT_HEADS, t), F32),
                   jax.ShapeDtypeStruct((t, ATT_HEADS * LANES), BF16),
                   jax.ShapeDtypeStruct((t, ATT_HEADS * LANES), BF16),
                   jax.ShapeDtypeStruct((t, ATT_HEADS * MLA_V), BF16)],
        scratch_shapes=[pltpu.VMEM((ATT_HEADS, 1), F32)],
        compiler_params=_cparams(("arbitrary",)),
        name="proj_odd",
    )(x2, *ws, fb, cos1, sin1, *mla)


def _ssd_chunk(c, s, xbc_ref, z_ref, dt_ref, dtt_ref, cw_ref, cb_ref, dtb_ref, dtbt_ref, alog_ref, alogt_ref,
               dskip_ref, ng_ref, o_ref, buf_ref, st_ref, y_ref):
    L = SSD_CHUNK
    rows = pl.ds(pl.multiple_of(s * L, L), L)

    @pl.when(c == 0)
    def _():
        buf_ref[...] = jnp.zeros_like(buf_ref)
        st_ref[...] = jnp.zeros_like(st_ref)

    cur = xbc_ref[rows, :]
    both = jnp.concatenate([buf_ref[(c + 1) % 2], cur], axis=0)
    shifts = range(1, SSD_CONV)
    sel = jnp.concatenate([(_iota((L, 2 * L), 1) == _iota((L, 2 * L), 0) + (L - k)) for k in shifts],
                          axis=0).astype(BF16)
    shifted = _dot(sel, both)
    acc = cb_ref[...] + cur.astype(F32) * cw_ref[SSD_CONV - 1:SSD_CONV, :]
    for n, k in enumerate(shifts):
        acc = acc + shifted[n * L:(n + 1) * L] * cw_ref[SSD_CONV - 1 - k:SSD_CONV - k, :]
    buf_ref[c % 2] = cur
    xbc = _silu(acc)
    b_mat = xbc[:, SSD_INNER:SSD_INNER + SSD_BC]
    c_mat = xbc[:, SSD_INNER + SSD_BC:]
    bt = jnp.transpose(b_mat).astype(BF16)
    c_bf = c_mat.astype(BF16)

    dt = _softplus(dt_ref[rows, :] + dtb_ref[...])
    dtt = _softplus(dtt_ref[:, rows] + dtbt_ref[...])
    da = dt * (-jnp.exp(alog_ref[...]))
    dat = dtt * (-jnp.exp(alogt_ref[...]))
    row = _iota((L, L), 0)
    col = _iota((L, L), 1)
    causal = col <= row
    a_cum = _dot01_left(causal.astype(BF16), da)
    a_cumt = _dot01_right(dat, (row <= col).astype(BF16))
    a_end = a_cum[L - 1:L, :]
    low_half = _iota((1, LANES), 1) < SSD_HEAD_DIM

    def pair_cols(m, h0):
        n = m.shape[0]
        return jnp.where(low_half, jnp.broadcast_to(m[:, h0:h0 + 1], (n, LANES)),
                         jnp.broadcast_to(m[:, h0 + 1:h0 + 2], (n, LANES)))

    sumsq = jnp.zeros((L, 1), F32)
    for g in range(SSD_GROUPS):
        bt_g = bt[g * SSD_STATE:(g + 1) * SSD_STATE, :]
        c_g = c_bf[:, g * SSD_STATE:(g + 1) * SSD_STATE]
        cb = _dot(c_g, bt_g)
        for pp in range(SSD_HEADS // SSD_GROUPS // 2):
            p = g * (SSD_HEADS // SSD_GROUPS // 2) + pp
            h0 = 2 * p
            lanes = slice(p * LANES, (p + 1) * LANES)
            xs = xbc[:, lanes]
            xs_bf = xs.astype(BF16)
            yd = []
            for h in (h0, h0 + 1):
                seg = a_cum[:, h:h + 1] - a_cumt[h:h + 1, :]
                dec = jnp.where(causal, jnp.exp(seg), 0.0)
                m = cb * dec * dtt[h:h + 1, :]
                yd.append(_dot(m.astype(BF16), xs_bf))
            y = jnp.where(low_half, yd[0], yd[1])
            a_p = pair_cols(a_cum, h0)
            dt_p = pair_cols(dt, h0)
            end_p = pair_cols(a_end, h0)
            st = st_ref[p]
            y = y + _dot(c_g, st.astype(BF16)) * jnp.exp(a_p)
            xw = xs * (jnp.exp(end_p - a_p) * dt_p)
            st_ref[p] = st * jnp.exp(end_p) + _dot(bt_g, xw.astype(BF16))
            y = y + dskip_ref[:, lanes] * xs
            y = y * _silu(z_ref[rows, lanes].astype(F32))
            y_ref[:, lanes] = y
            sumsq = sumsq + jnp.sum(y * y, axis=1, keepdims=True)
    inv = lax.rsqrt(sumsq * (1.0 / SSD_INNER) + RMS_EPS)
    o_ref[rows, :] = (y_ref[...] * inv * ng_ref[...]).astype(o_ref.dtype)


def _ssd_kernel(*refs, chunks):
    def body(s, carry):
        _ssd_chunk(pl.program_id(1) * chunks + s, s, *refs)
        return carry
    lax.fori_loop(0, chunks, body, 0)


def _ssd(xbc, z, dt, dtt, conv_w, conv_b, dt_bias, a_log, d_skip, norm_g, batch, seq):
    t = xbc.shape[0]
    chunks = math.gcd(seq // SSD_CHUNK, SSD_STEP_CHUNKS)
    L = SSD_CHUNK
    rows = L * chunks
    nc = seq // rows
    row = lambda n: pl.BlockSpec((rows, n), lambda b, c: (b * nc + c, 0))
    full = lambda a: pl.BlockSpec(a.shape, lambda b, c: (0,) * a.ndim)
    params = [conv_w.astype(F32), conv_b.reshape(1, -1).astype(F32),
              dt_bias.reshape(1, -1).astype(F32), dt_bias.reshape(-1, 1).astype(F32),
              a_log.reshape(1, -1).astype(F32), a_log.reshape(-1, 1).astype(F32),
              jnp.repeat(d_skip.astype(F32), SSD_HEAD_DIM).reshape(1, -1),
              norm_g.reshape(1, -1).astype(F32)]
    return pl.pallas_call(
        functools.partial(_ssd_kernel, chunks=chunks),
        grid=(batch, nc),
        in_specs=[row(SSD_CONV_DIM), row(SSD_INNER), row(SSD_HEADS),
                  pl.BlockSpec((SSD_HEADS, rows), lambda b, c: (0, b * nc + c))] + [full(a) for a in params],
        out_specs=row(SSD_INNER),
        out_shape=jax.ShapeDtypeStruct((t, SSD_INNER), BF16),
        scratch_shapes=[pltpu.VMEM((2, L, SSD_CONV_DIM), BF16),
                        pltpu.VMEM((SSD_HEADS // 2, SSD_STATE, LANES), F32),
                        pltpu.VMEM((L, SSD_INNER), F32)],
        compiler_params=_cparams(("parallel", "arbitrary")),
        name="ssd",
    )(xbc, z, dt, dtt, *params)


def _attention_block(i, k_norm, refs, mode, tq, tk, diag_split):
    if mode == "fox":
        q_ref, k_ref, v_ref, fh_ref, o_ref, acc_ref, m_ref, l_ref = refs
    else:
        q_ref, k_ref, v_ref, o_ref, acc_ref, m_ref, l_ref = refs
        fh_ref = None
    n_diag = tq // tk
    lane = _iota((1, LANES), 1)
    low_half = lane < ATT_HEAD_DIM
    row = _iota((tq, tk), 0)
    col = _iota((tq, tk), 1)

    def diag_mask(dd):
        c = col + dd * tk
        if mode == "sb":
            return c < row
        if mode == "mla":
            return (c // MLA_CHUNK) <= (row // MLA_CHUNK)
        return c <= row

    if mode == "sb":
        u_mat = (_iota((tk, tk), 0) > _iota((tk, tk), 1)).astype(BF16)

    q_rows = pl.ds(pl.multiple_of(i * tq, tq), tq)
    q = q_ref[q_rows, :]
    if mode == "mla":
        qs = [q[:, 0:LANES], q[:, LANES:2 * LANES]]
    else:
        zero = jnp.zeros_like(q)
        qs = [jnp.where(low_half, q, zero), jnp.where(low_half, zero, q)]

    if mode == "fox":
        fh_q = [fh_ref[hh:hh + 1, pl.ds(pl.multiple_of(i * tq, tq), LANES)][:, 0:1] for hh in range(2)]

    def rep(x, width):
        return jnp.concatenate([x] * (width // LANES), axis=1) if width > LANES else x

    def tile(chunks, split, first):
        rows = tq // split
        groups = [(hh, slice(rb * rows, (rb + 1) * rows), rb) for hh in range(2) for rb in range(split)]
        loaded = []
        for j, dd in chunks:
            k0 = pl.multiple_of(j * tk, tk)
            k_t = k_ref[pl.ds(k0, tk), :]
            k_h = [k_t[:, 0:LANES], k_t[:, LANES:2 * LANES]] if mode == "mla" else [k_t, k_t]
            bias = None
            if mode == "fox":
                bias = [fh_q[hh] - fh_ref[hh:hh + 1, pl.ds(k0, tk)] for hh in range(2)]
            loaded.append((k_h, v_ref[pl.ds(k0, tk), :], bias, None if dd is None else diag_mask(dd)))
        parts = []
        for hh, rs, rb in groups:
            ps_ = []
            for c, (j, dd) in enumerate(chunks):
                nk = tk if dd is None else min(tk, (rb + 1) * rows - dd * tk)
                if nk > 0:
                    ps_.append((c, nk, None if dd is None else loaded[c][3][rs, 0:nk]))
            parts.append(ps_)
        flat = [(g, c, nk, m) for g, ps_ in enumerate(parts) for c, nk, m in ps_]
        zs = [_dot_nt(qs[groups[g][0]][groups[g][1]], loaded[c][0][groups[g][0]][0:nk]) for g, c, nk, m in flat]
        if mode == "sb":
            if first:
                runs = [jnp.zeros((rows, LANES), F32) for _ in groups]
                accs = [None for _ in groups]
            else:
                runs = [m_ref[hh, rs] for hh, rs, _ in groups]
                accs = [acc_ref[hh, rs] for hh, rs, _ in groups]
            sps = [_softplus(z) for z in zs]
            l1ms = [-sp if m is None else jnp.where(m, -sp, 0.0) for sp, (_, _, _, m) in zip(sps, flat)]
            css = [_dot(l1m.astype(BF16), u_mat[0:nk, 0:nk]) for l1m, (_, _, nk, _) in zip(l1ms, flat)]
            tots = [cs[:, 0:1] + l1m[:, 0:1] for cs, l1m in zip(css, l1ms)]
            seen = []
            cur = list(runs)
            for n, (g, c, nk, m) in enumerate(flat):
                seen.append(cur[g])
                cur[g] = cur[g] + tots[n]
            ws = [jnp.exp((z - sp) + cs + rep(run, nk))
                  for z, sp, cs, run, (_, _, nk, _) in zip(zs, sps, css, seen, flat)]
            ws = [w if m is None else jnp.where(m, w, 0.0) for w, (_, _, _, m) in zip(ws, flat)]
            pvs = [_dot(w.astype(BF16), loaded[c][1][0:nk]) for w, (_, c, nk, _) in zip(ws, flat)]
            for n, (g, c, nk, m) in enumerate(flat):
                accs[g] = pvs[n] if accs[g] is None else accs[g] + pvs[n]
            for (hh, rs, _), acc, run in zip(groups, accs, cur):
                acc_ref[hh, rs] = acc
                m_ref[hh, rs] = run
        else:
            if mode == "fox":
                zs = [z + loaded[c][2][groups[g][0]][:, 0:nk] for z, (g, c, nk, m) in zip(zs, flat)]
            zs = [z if m is None else jnp.where(m, z, NEG_BIG) for z, (_, _, _, m) in zip(zs, flat)]
            if first:
                m_news = [jnp.full((rows, LANES), NEG_BIG, F32) for _ in groups]
            else:
                m_olds = [m_ref[hh, rs] for hh, rs, _ in groups]
                m_news = list(m_olds)
            for z, (g, _, _, _) in zip(zs, flat):
                m_news[g] = jnp.maximum(m_news[g], jnp.max(z, axis=1, keepdims=True))
            ps = [jnp.exp(z - rep(m_news[g], nk)) for z, (g, _, nk, _) in zip(zs, flat)]
            pvs = [_dot(p.astype(BF16), loaded[c][1][0:nk]) for p, (_, c, nk, _) in zip(ps, flat)]
            if first:
                l_news = [None for _ in groups]
                acc_news = [None for _ in groups]
            else:
                alphas = [jnp.exp(m_old - m_new) for m_old, m_new in zip(m_olds, m_news)]
                l_news = [alpha * l_ref[hh, rs] for alpha, (hh, rs, _) in zip(alphas, groups)]
                acc_news = [alpha * acc_ref[hh, rs] for alpha, (hh, rs, _) in zip(alphas, groups)]
            for p, pv, (g, _, _, _) in zip(ps, pvs, flat):
                ls = jnp.sum(p, axis=1, keepdims=True)
                l_news[g] = jnp.broadcast_to(ls, (rows, LANES)) if l_news[g] is None else l_news[g] + ls
                acc_news[g] = pv if acc_news[g] is None else acc_news[g] + pv
            for (hh, rs, _), m_new, l_new, acc in zip(groups, m_news, l_news, acc_news):
                l_ref[hh, rs] = l_new
                acc_ref[hh, rs] = acc
                m_ref[hh, rs] = m_new

    def more(j_done):
        if mode == "sb":
            return (jnp.max(m_ref[...]) > -EXP_ZERO).astype(jnp.int32)
        if mode == "fox":
            kl = pl.multiple_of(jnp.maximum(j_done - 1, 0) * tk, tk)
            best = None
            for hh in range(2):
                fh_last = fh_ref[hh:hh + 1, pl.ds(kl, tk)][:, tk - 1:tk]
                v = jnp.max(qk_bound - m_ref[hh], axis=(0, 1), keepdims=True) + (fh_q[hh] - fh_last)
                best = v if best is None else jnp.maximum(best, v)
            return (jnp.max(best) > -EXP_ZERO).astype(jnp.int32)
        return jnp.int32(1)

    if mode == "fox":
        qf = q.astype(F32)
        qk_bound = jnp.sqrt(jnp.sum(qf * qf, axis=1, keepdims=True)) * k_norm

    n_off = i * n_diag
    diag_chunks = [(n_off + dd, dd) for dd in reversed(range(n_diag))]

    @pl.when(i == 0)
    def _():
        tile(diag_chunks, diag_split, True)

    @pl.when(i > 0)
    def _():
        tile(diag_chunks + [(n_off - 1, None)], diag_split, True)

    def cond(c):
        return jnp.logical_and(c[0] < n_off, c[1] > 0)

    def body(c):
        j = n_off - 1 - c[0]
        tile([(j, None)], 1, False)
        return c[0] + 1, more(j)
    lax.while_loop(cond, body, (jnp.int32(1), more(jnp.maximum(n_off - 1, 0))))
    if mode == "sb":
        out = [acc_ref[0], acc_ref[1]]
    else:
        out = [acc_ref[hh] / l_ref[hh] for hh in range(2)]
    o_ref[q_rows, :] = jnp.where(low_half, out[0], out[1]).astype(o_ref.dtype)


def _attention_kernel(*refs, mode, tq, tk, diag_split, nq):
    k_norm = None
    if mode == "fox":
        kf = refs[1][...].astype(F32)
        n2 = jnp.max(jnp.sum(kf * kf, axis=1, keepdims=True), axis=0, keepdims=True)
        k_norm = jnp.broadcast_to(jnp.sqrt(n2), (1, LANES))

    def q_block(i, carry):
        _attention_block(i, k_norm, refs, mode, tq, tk, diag_split)
        return carry
    lax.fori_loop(0, nq, q_block, 0)


def _attention(mode, q, k, v, batch, seq, fh=None):
    t = v.shape[0]
    tq = min(ATT_TQ[mode], seq)
    tk = min(ATT_TK[mode], tq)
    nq = seq // tq
    qk_w = 2 * LANES if mode == "mla" else LANES
    in_specs = [pl.BlockSpec((seq, qk_w), lambda b, h: (b, h)),
                pl.BlockSpec((seq, qk_w), lambda b, h: (b, h)),
                pl.BlockSpec((seq, LANES), lambda b, h: (b, h))]
    args = [q, k, v]
    scratch = [pltpu.VMEM((2, tq, LANES), F32), pltpu.VMEM((2, tq, LANES), F32), pltpu.VMEM((2, tq, LANES), F32)]
    if mode == "fox":
        in_specs.append(pl.BlockSpec((None, 2, seq), lambda b, h: (h, 0, b)))
        args.append(fh.reshape(HEAD_PAIRS, 2, t))
    return pl.pallas_call(
        functools.partial(_attention_kernel, mode=mode, tq=tq, tk=tk, diag_split=max(1, tq // ATT_DIAG_ROWS), nq=nq),
        grid=(batch, HEAD_PAIRS),
        in_specs=in_specs,
        out_specs=pl.BlockSpec((seq, LANES), lambda b, h: (b, h)),
        out_shape=jax.ShapeDtypeStruct((t, ATT_WIDTH), BF16),
        scratch_shapes=scratch,
        compiler_params=_cparams(("parallel", "parallel")),
        name="attn_" + mode,
    )(*args)


def _layer_norm_rows(r, g, b):
    mu = jnp.mean(r, axis=1, keepdims=True)
    d = r - mu
    var = jnp.mean(d * d, axis=1, keepdims=True)
    return d * lax.rsqrt(var + LN_EPS) * g + b


def _router(x, w_both, bias):
    tm = x.shape[0]
    x_hi = x.astype(BF16)
    x_lo = (x - x_hi.astype(F32)).astype(BF16)
    both = _dot(x_hi, w_both)
    logits = both[:, 0:LANES] + (both[:, LANES:2 * LANES] + _dot(x_lo, w_both[:, 0:LANES])) + bias
    e_idx = _iota((tm, LANES), 1)
    e_pos = e_idx.astype(F32)

    def first_lane(hit):
        return jnp.min(jnp.where(hit, e_pos, float(LANES)), axis=1, keepdims=True).astype(jnp.int32)

    is_group = jnp.logical_and(e_idx >= MOE_EXPERTS, e_idx < MOE_EXPERTS + MOE_GROUPS)
    gl = jnp.where(is_group, logits, -jnp.inf)
    g_max = jnp.max(gl, axis=1, keepdims=True)
    g_p = 1.0 / jnp.sum(jnp.exp(gl - g_max), axis=1, keepdims=True)
    g_sel = first_lane(gl == g_max) - MOE_EXPERTS
    in_group = (e_idx // MOE_EPG) == g_sel
    masked = jnp.where(in_group, logits, -jnp.inf)
    m1 = jnp.max(masked, axis=1, keepdims=True)
    i1 = first_lane(masked == m1)
    masked2 = jnp.where(e_idx == i1, -jnp.inf, masked)
    m2 = jnp.max(masked2, axis=1, keepdims=True)
    i2 = first_lane(masked2 == m2)
    e2 = jnp.exp(m2 - m1)
    w1 = g_p / (1.0 + e2)
    w2 = w1 * e2
    gate = jnp.where(e_idx == i1, w1, 0.0) + jnp.where(e_idx == i2, w2, 0.0)
    lo = jnp.minimum(i1, i2) - g_sel * MOE_EPG
    hi = jnp.maximum(i1, i2) - g_sel * MOE_EPG
    pair = ((lo * (2 * MOE_EPG - 1 - lo)) >> 1) + (hi - lo - 1)
    return gate, g_sel * MOE_PAIRS + pair


def _outproj_ln_router_kernel(a_ref, b_ref, x_ref, wa_ref, wb_ref, g_ref, beta_ref, wr_ref, br_ref,
                              tri_ref, xg_ref, bucket_ref, rank_ref, count_ref, run_ref):
    i = pl.program_id(0)
    tm, d = x_ref.shape

    @pl.when(i == 0)
    def _():
        run_ref[...] = jnp.zeros_like(run_ref)

    y = _dot(a_ref[...], wa_ref[...]) + _dot(b_ref[...], wb_ref[...])
    x1 = _layer_norm_rows(DEEPNORM_ALPHA * x_ref[...] + y, g_ref[...], beta_ref[...])
    gate, bucket = _router(x1, wr_ref[...], br_ref[...])
    xg_ref[:, 0:d] = x1
    xg_ref[:, d:d + LANES] = gate
    bucket_ref[...] = bucket
    onehot = _iota((tm, LANES), 1) == bucket
    prefix = _dot(tri_ref[...], jnp.where(onehot, 1.0, 0.0).astype(BF16))
    before = run_ref[...]
    rank = jnp.sum(jnp.where(onehot, prefix + before, 0.0), axis=1, keepdims=True) - 1.0
    rank_ref[...] = rank.astype(jnp.int32)
    run_ref[...] = before + prefix[tm - 1:tm, :]
    count_ref[...] = run_ref[...]


def _outproj_ln_router(a, b, x2, w_out, ln_g, ln_b, w_group, b_group, w_expert, b_expert, tm):
    t, d = x2.shape
    ka, kb = a.shape[1], b.shape[1]
    pad = LANES - MOE_EXPERTS - MOE_GROUPS
    w_r = jnp.pad(jnp.concatenate([w_expert, w_group], axis=1).astype(F32), ((0, 0), (0, pad)))
    b_r = jnp.pad(jnp.concatenate([b_expert, b_group]).astype(F32), (0, pad)).reshape(1, LANES)
    w_r_hi = w_r.astype(BF16)
    consts = [w_out[:ka].astype(BF16), w_out[ka:].astype(BF16),
              ln_g.reshape(1, -1).astype(F32), ln_b.reshape(1, -1).astype(F32),
              jnp.concatenate([w_r_hi, (w_r - w_r_hi.astype(F32)).astype(BF16)], axis=1), b_r,
              jnp.tril(jnp.ones((tm, tm), BF16))]
    row = lambda n: pl.BlockSpec((tm, n), lambda i: (i, 0))
    full = lambda c: pl.BlockSpec(c.shape, lambda i: (0, 0))
    return pl.pallas_call(
        _outproj_ln_router_kernel,
        grid=(t // tm,),
        in_specs=[row(ka), row(kb), row(d)] + [full(c) for c in consts],
        out_specs=[row(d + LANES), row(1), row(1), pl.BlockSpec((1, LANES), lambda i: (0, 0))],
        out_shape=[jax.ShapeDtypeStruct((t, d + LANES), F32),
                   jax.ShapeDtypeStruct((t, 1), jnp.int32),
                   jax.ShapeDtypeStruct((t, 1), jnp.int32),
                   jax.ShapeDtypeStruct((1, LANES), F32)],
        scratch_shapes=[pltpu.VMEM((1, LANES), F32)],
        compiler_params=_cparams(("arbitrary",)),
        name="outproj_ln_router",
    )(a, b, x2, *consts)


def _moe_plan(bucket, rank, counts, t):
    counts = counts[0, :MOE_BUCKETS].astype(jnp.int32)
    padded = ((counts + MOE_TILE - 1) // MOE_TILE) * MOE_TILE
    ends = jnp.cumsum(padded)
    starts = ends - padded
    pos = jnp.take(starts, bucket[:, 0]) + rank[:, 0]
    n_tiles = t // MOE_TILE + MOE_BUCKETS
    tile_start = jnp.arange(n_tiles, dtype=jnp.int32) * MOE_TILE
    tile_bucket = jnp.minimum(jnp.sum((tile_start[:, None] >= ends[None, :]).astype(jnp.int32), axis=1),
                              MOE_BUCKETS - 1)
    pairs = [(a, b) for a in range(MOE_EPG) for b in range(a + 1, MOE_EPG)]
    first = jnp.asarray([g * MOE_EPG + a for g in range(MOE_GROUPS) for a, _ in pairs], jnp.int32)
    second = jnp.asarray([g * MOE_EPG + b for g in range(MOE_GROUPS) for _, b in pairs], jnp.int32)
    n_used = (ends[MOE_BUCKETS - 1] // MOE_TILE).reshape(1)
    return pos, ends, jnp.take(first, tile_bucket), jnp.take(second, tile_bucket), n_used, n_tiles


def _moe_scatter_kernel(ends_ref, pos_ref, xg_ref, xs_hbm, zero_ref, sem, zero_sem):
    n = pos_ref.shape[0]

    @pl.when(pl.program_id(0) == 0)
    def _():
        zero_ref[...] = jnp.zeros_like(zero_ref)
        total = ends_ref[MOE_BUCKETS - 1]

        def zero_copy(row0):
            return pltpu.make_async_copy(zero_ref, xs_hbm.at[pl.ds(pl.multiple_of(row0, MOE_TILE), MOE_TILE)],
                                         zero_sem)

        jobs = [(ends_ref[b] >= MOE_TILE, ends_ref[b] - MOE_TILE) for b in range(MOE_BUCKETS)]
        jobs += [(total + u * MOE_TILE < xs_hbm.shape[0], total + u * MOE_TILE) for u in range(MOE_BUCKETS)]
        for wanted, row0 in jobs:
            @pl.when(wanted)
            def _():
                zero_copy(row0).start()
        for wanted, row0 in jobs:
            @pl.when(wanted)
            def _():
                zero_copy(row0).wait()

    def body(r, c):
        pltpu.make_async_copy(xg_ref.at[pl.ds(r, 1)], xs_hbm.at[pl.ds(pos_ref[r], 1)], sem).start()
        return c
    lax.fori_loop(0, n, body, 0, unroll=DMA_UNROLL)
    pltpu.make_async_copy(xg_ref, xs_hbm.at[pl.ds(0, n)], sem).wait()


def _moe_scatter(xg, pos, ends, n_rows, chunk):
    t, w = xg.shape
    return pl.pallas_call(
        _moe_scatter_kernel,
        grid=(t // chunk,),
        in_specs=[pl.BlockSpec(memory_space=pltpu.SMEM),
                  pl.BlockSpec((chunk,), lambda i: (i,), memory_space=pltpu.SMEM),
                  pl.BlockSpec((chunk, w), lambda i: (i, 0))],
        out_specs=pl.BlockSpec(memory_space=pl.ANY),
        out_shape=jax.ShapeDtypeStruct((n_rows, w), xg.dtype),
        scratch_shapes=[pltpu.VMEM((MOE_TILE, w), xg.dtype), pltpu.SemaphoreType.DMA(()),
                        pltpu.SemaphoreType.DMA(())],
        compiler_params=_cparams(("arbitrary",)),
        name="moe_scatter",
    )(ends, pos, xg)


def _moe_ffn_kernel(ea_ref, eb_ref, nu_ref, xs_ref, wga_ref, wua_ref, wda_ref, wgb_ref, wub_ref, wdb_ref, y_ref):
    i = pl.program_id(0)
    d = wga_ref.shape[0]

    @pl.when(i < nu_ref[0])
    def _():
        x = xs_ref[:, 0:d].astype(BF16)
        gate = xs_ref[:, d:d + LANES]
        lane = _iota(gate.shape, 1)
        y = None
        for e_ref, wg_ref, wu_ref, wd_ref in ((ea_ref, wga_ref, wua_ref, wda_ref), (eb_ref, wgb_ref, wub_ref, wdb_ref)):
            w = jnp.sum(jnp.where(lane == e_ref[i], gate, 0.0), axis=1, keepdims=True)
            hid = _silu(_dot(x, wg_ref[...])) * _dot(x, wu_ref[...]) * w
            part = _dot(hid.astype(BF16), wd_ref[...])
            y = part if y is None else y + part
        y_ref[...] = y

    @pl.when(i >= nu_ref[0])
    def _():
        y_ref[...] = jnp.zeros_like(y_ref)


def _moe_ffn(xs, ea, eb, n_used, n_tiles, w_gate, w_up, w_down):
    d = w_gate.shape[1]
    wg, wu, wd = w_gate.astype(BF16), w_up.astype(BF16), w_down.astype(BF16)
    up_a = pl.BlockSpec((None, d, MOE_FF), lambda i, ea, eb, nu: (ea[i], 0, 0))
    dn_a = pl.BlockSpec((None, MOE_FF, d), lambda i, ea, eb, nu: (ea[i], 0, 0))
    up_b = pl.BlockSpec((None, d, MOE_FF), lambda i, ea, eb, nu: (eb[i], 0, 0))
    dn_b = pl.BlockSpec((None, MOE_FF, d), lambda i, ea, eb, nu: (eb[i], 0, 0))
    return pl.pallas_call(
        _moe_ffn_kernel,
        grid_spec=pltpu.PrefetchScalarGridSpec(
            num_scalar_prefetch=3,
            grid=(n_tiles,),
            in_specs=[pl.BlockSpec((MOE_TILE, d + LANES), lambda i, ea, eb, nu: (jnp.minimum(i, nu[0] - 1), 0)),
                      up_a, up_a, dn_a, up_b, up_b, dn_b],
            out_specs=pl.BlockSpec((MOE_TILE, d), lambda i, ea, eb, nu: (i, 0))),
        out_shape=jax.ShapeDtypeStruct((n_tiles * MOE_TILE, d), F32),
        compiler_params=_cparams(("arbitrary",)),
        name="moe_ffn",
    )(ea, eb, n_used, xs, wg, wu, wd, wg, wu, wd)


def _moe_combine_kernel(pos_ref, pos_next_ref, y_hbm, x_ref, g_ref, beta_ref, o_ref, buf_ref, sem):
    i = pl.program_id(0)
    n = pos_ref.shape[0]

    def gather(p_ref, s):
        def body(r, c):
            pltpu.make_async_copy(y_hbm.at[pl.ds(p_ref[r], 1)], buf_ref.at[s, pl.ds(r, 1)], sem.at[s]).start()
            return c
        lax.fori_loop(0, n, body, 0, unroll=DMA_UNROLL)

    def step(slot):
        if slot == 0:
            @pl.when(i == 0)
            def _():
                gather(pos_ref, 0)

        @pl.when(i + 1 < pl.num_programs(0))
        def _():
            gather(pos_next_ref, 1 - slot)

        pltpu.make_async_copy(y_hbm.at[pl.ds(0, n)], buf_ref.at[slot], sem.at[slot]).wait()
        r = DEEPNORM_ALPHA * x_ref[...] + buf_ref[slot]
        o_ref[...] = _layer_norm_rows(r, g_ref[...], beta_ref[...])

    for slot in range(2):
        @pl.when(i % 2 == slot)
        def _():
            step(slot)


def _moe_combine(y_sorted, pos, xg, ln_g, ln_b, tm):
    t = pos.shape[0]
    d = y_sorted.shape[1]
    ln = [ln_g.reshape(1, -1).astype(F32), ln_b.reshape(1, -1).astype(F32)]
    last = t // tm - 1
    return pl.pallas_call(
        _moe_combine_kernel,
        grid=(t // tm,),
        in_specs=[pl.BlockSpec((tm,), lambda i: (i,), memory_space=pltpu.SMEM),
                  pl.BlockSpec((tm,), lambda i: (jnp.minimum(i + 1, last),), memory_space=pltpu.SMEM),
                  pl.BlockSpec(memory_space=pl.ANY),
                  pl.BlockSpec((tm, d), lambda i: (i, 0))]
        + [pl.BlockSpec(c.shape, lambda i: (0, 0)) for c in ln],
        out_specs=pl.BlockSpec((tm, d), lambda i: (i, 0)),
        out_shape=jax.ShapeDtypeStruct((t, d), F32),
        scratch_shapes=[pltpu.VMEM((2, tm, d), F32), pltpu.SemaphoreType.DMA((2,))],
        compiler_params=_cparams(("arbitrary",)),
        name="moe_combine",
    )(pos, pos, y_sorted, xg, *ln)


def _mixer_out_and_moe(a, b, x2, w_out, ln1_g, ln1_b, w_group, b_group, w_expert, b_expert,
                       w_gate, w_up, w_down, ln2_g, ln2_b, tm):
    t = x2.shape[0]
    xg, bucket, rank, counts = _outproj_ln_router(a, b, x2, w_out, ln1_g, ln1_b, w_group, b_group,
                                                  w_expert, b_expert, tm)
    pos, ends, ea, eb, n_used, n_tiles = _moe_plan(bucket, rank, counts, t)
    xs = _moe_scatter(xg, pos, ends, n_tiles * MOE_TILE, _row_tile(t, 1024))
    y_sorted = _moe_ffn(xs, ea, eb, n_used, n_tiles, w_gate, w_up, w_down)
    return _moe_combine(y_sorted, pos, xg, ln2_g, ln2_b, tm)


def _row_tile(n, pref):
    tm = min(pref, n)
    assert n % tm == 0
    return tm


def kernel(x, ev_w_in, ev_conv_w, ev_conv_b, ev_dt_bias, ev_a_log, ev_d_skip, ev_norm_g, ev_w_out, od_w_in, od_q_norm_g, od_w_q_up, od_kv_norm_g, od_w_kv_up, od_f_bias, od_w_out, ln1_g, ln1_b, ln2_g, ln2_b, moe_w_group, moe_b_group, moe_w_expert, moe_b_expert, moe_w_gate, moe_w_up, moe_w_down):
    batch, seq, d = x.shape
    t = batch * seq
    assert seq % SSD_CHUNK == 0 and seq % LANES == 0 and t % MOE_TILE == 0
    tm_proj = _row_tile(seq, 512)
    x2 = x.reshape(t, d)

    z, xbc, dt, dtt, q, k, v = _proj_even(x2, ev_w_in[0], tm_proj)
    y_ssd = _ssd(xbc, z, dt, dtt, ev_conv_w[0], ev_conv_b[0], ev_dt_bias[0], ev_a_log[0], ev_d_skip[0],
                 ev_norm_g[0], batch, seq)
    y_sb = _attention("sb", q, k, v, batch, seq)
    x2 = _mixer_out_and_moe(y_ssd, y_sb, x2, ev_w_out[0], ln1_g[0], ln1_b[0], moe_w_group[0], moe_b_group[0],
                            moe_w_expert[0], moe_b_expert[0], moe_w_gate[0], moe_w_up[0], moe_w_down[0],
                            ln2_g[0], ln2_b[0], tm_proj)

    q, k, v, fh, qm, km, vm = _proj_odd(x2, od_w_in[0], od_f_bias[0], od_q_norm_g[0], od_w_q_up[0],
                                        od_kv_norm_g[0], od_w_kv_up[0], seq, tm_proj)
    y_mla = _attention("mla", qm, km, vm, batch, seq)
    y_fox = _attention("fox", q, k, v, batch, seq, fh=fh)
    x2 = _mixer_out_and_moe(y_mla, y_fox, x2, od_w_out[0], ln1_g[1], ln1_b[1], moe_w_group[1], moe_b_group[1],
                            moe_w_expert[1], moe_b_expert[1], moe_w_gate[1], moe_w_up[1], moe_w_down[1],
                            ln2_g[1], ln2_b[1], tm_proj)
    return x2.reshape(batch, seq, d)
```
